```python
import jax, jax.numpy as jnp
from jax import lax
import numpy as np

D_MODEL = 4096
BATCH = 2
SEQ = 4096
DEPTH = 1
DEC_BATCH = 128
DEC_SEQ = 8
PAST_LEN = 8192
PAGE_SIZE = 128

WIDTH_A = D_MODEL // 2
HEAD_DIM = 64
N_HEADS_A = WIDTH_A // HEAD_DIM
N_KV = N_HEADS_A // 8
GROUP = N_HEADS_A // N_KV
WINDOW = 128
ATTN_BLOCK = 128
WIDTH_B = D_MODEL // 2
N_HEADS_B = 4
DV_B = WIDTH_B // N_HEADS_B
KEY_B = WIDTH_B // 2
DK_B = KEY_B // N_HEADS_B
GATE_RANK = 16
GLA_TAU = 16.0
GLA_CHUNK = 64
EPS = 1e-6
NEG_INF = -1e30
PROJ_SIZES = (WIDTH_A, N_KV * HEAD_DIM, N_KV * HEAD_DIM, WIDTH_A, KEY_B, KEY_B, WIDTH_B, WIDTH_B, GATE_RANK, D_MODEL, D_MODEL)
PROJ_DIM = sum(PROJ_SIZES)

kernel_name = "hybrid_swa_sink_gla_adaln_step"

F32 = jnp.float32


def rms_norm(x, g):
    xf = x.astype(F32)
    y = xf * lax.rsqrt(jnp.mean(xf * xf, axis=-1, keepdims=True) + EPS)
    return (y * g.astype(F32)).astype(x.dtype)


def split_columns(p):
    idx = np.cumsum(PROJ_SIZES)[:-1].tolist()
    return jnp.split(p, idx, axis=-1)


def mixer_input(x, c, w_ada, b_ada, norm_gain, w_in):
    mod = jax.nn.silu(c) @ w_ada + b_ada
    shift, scale, gate = jnp.split(mod, 3, axis=-1)
    h = rms_norm(x, norm_gain) * (1 + scale[:, None, :]) + shift[:, None, :]
    return split_columns(h @ w_in), gate


def branch_tensors(parts, w_alpha2, b_alpha):
    qa, ka, va, za, qb, kb, vb, zb, r, ga, gb = parts
    N, L = qa.shape[:2]
    qa = qa.reshape(N, L, N_KV, GROUP, HEAD_DIM)
    ka = ka.reshape(N, L, N_KV, HEAD_DIM)
    va = va.reshape(N, L, N_KV, HEAD_DIM)
    qb = qb.reshape(N, L, N_HEADS_B, DK_B)
    kb = kb.reshape(N, L, N_HEADS_B, DK_B)
    vb = vb.reshape(N, L, N_HEADS_B, DV_B)
    log_a = (jax.nn.log_sigmoid((r @ w_alpha2 + b_alpha).astype(F32)) / GLA_TAU).reshape(N, L, N_HEADS_B, DK_B)
    return qa, ka, va, za, qb, kb, vb, zb, log_a, ga, gb


def window_mask(qpos, kpos):
    d = qpos[..., :, None] - kpos[..., None, :]
    return (d >= 0) & (d <= WINDOW) & (kpos[..., None, :] >= 0)


def sink_softmax(scores, mask, sink):
    s = jnp.where(mask, scores, NEG_INF)
    sink_col = jnp.broadcast_to(sink.astype(F32).reshape(N_KV, GROUP, 1, 1), s.shape[:-1] + (1,))
    p = jax.nn.softmax(jnp.concatenate([s, sink_col], axis=-1), axis=-1)
    return p[..., :-1]


def swa_prompt(q, k, v, sink):
    B, L = q.shape[:2]
    nb = L // ATTN_BLOCK
    qb = q.reshape(B, nb, ATTN_BLOCK, N_KV, GROUP, HEAD_DIM)

    def band(t):
        tb = t.reshape(B, nb, ATTN_BLOCK, N_KV, HEAD_DIM)
        prev = jnp.concatenate([jnp.zeros_like(tb[:, :1]), tb[:, :-1]], axis=1)
        return jnp.concatenate([prev, tb], axis=2)

    kb, vb = band(k), band(v)
    start = jnp.arange(nb)[:, None] * ATTN_BLOCK
    qpos = start + jnp.arange(ATTN_BLOCK)
    kpos = start - ATTN_BLOCK + jnp.arange(2 * ATTN_BLOCK)
    mask = window_mask(qpos, kpos)[None, :, None, None]
    scores = jnp.einsum('bnqhgd,bnkhd->bnhgqk', qb, kb, preferred_element_type=F32) * (HEAD_DIM ** -0.5)
    p = sink_softmax(scores, mask, sink).astype(v.dtype)
    o = jnp.einsum('bnhgqk,bnkhd->bnqhgd', p, vb)
    w = min(WINDOW, L)
    return o.reshape(B, L, WIDTH_A), k[:, -w:], v[:, -w:]


def swa_sample(q, k, v, k_cache, v_cache, sink):
    N, T = q.shape[:2]
    w = k_cache.shape[1]
    k_all = jnp.concatenate([k_cache.astype(k.dtype), k], axis=1)
    v_all = jnp.concatenate([v_cache.astype(v.dtype), v], axis=1)
    qpos = PAST_LEN + jnp.arange(T)
    kpos = PAST_LEN - w + jnp.arange(w + T)
    mask = window_mask(qpos, kpos)
    scores = jnp.einsum('bqhgd,bkhd->bhgqk', q, k_all, preferred_element_type=F32) * (HEAD_DIM ** -0.5)
    p = sink_softmax(scores, mask, sink).astype(v.dtype)
    o = jnp.einsum('bhgqk,bkhd->bqhgd', p, v_all)
    return o.reshape(N, T, WIDTH_A), k_all[:, -w:], v_all[:, -w:]


def gla_recurrent(q, k, v, log_a, s0, chunk):
    N, L = q.shape[:2]
    n = L // chunk

    def blocks(t):
        return jnp.moveaxis(t.astype(F32).reshape((N, n, chunk) + t.shape[2:]), 1, 0)

    qc, kc, vc, ac = blocks(q), blocks(k), blocks(v), blocks(log_a)
    causal = jnp.tril(jnp.ones((chunk, chunk), dtype=bool))

    def step(S, inp):
        qn, kn, vn, an = inp
        b = jnp.cumsum(an, axis=1)
        b_last = b[:, -1]
        q_t = qn * jnp.exp(b) * (DK_B ** -0.5)
        k_t = kn * jnp.exp(-b)
        k_d = kn * jnp.exp(b_last[:, None] - b)
        att = jnp.where(causal, jnp.einsum('nthd,nshd->nhts', q_t, k_t), 0.0)
        o = jnp.einsum('nhts,nshv->nthv', att, vn) + jnp.einsum('nthd,nhdv->nthv', q_t, S)
        S = jnp.exp(b_last)[..., None] * S + jnp.einsum('nshd,nshv->nhdv', k_d, vn)
        return S, o

    S, o = lax.scan(step, s0.astype(F32), (qc, kc, vc, ac))
    o = jnp.moveaxis(o, 0, 1).reshape(N, L, N_HEADS_B, DV_B)
    return o, S


def mixer_output(x, gate, o_a, za, o_b, zb, ga, gb, gla_gain, w_pa, w_pb, w_o):
    N, L = x.shape[:2]
    u_a = (o_a * jax.nn.silu(za)) @ w_pa
    on = o_b * lax.rsqrt(jnp.mean(o_b * o_b, axis=-1, keepdims=True) + EPS) * gla_gain.astype(F32).reshape(N_HEADS_B, DV_B)
    u_b = (on.reshape(N, L, WIDTH_B).astype(x.dtype) * jax.nn.silu(zb)) @ w_pb
    merged = jax.nn.sigmoid(ga) * u_a + jax.nn.sigmoid(gb) * u_b
    return x + gate[:, None, :] * (merged @ w_o)


def setup_inputs(seed: int = 0) -> dict:
    key = jax.random.key(seed)
    ks = jax.random.split(key, 20)
    w_buf = min(WINDOW, PAST_LEN)
    nrm = jax.random.normal
    return {
        "x_prompt": nrm(ks[0], (BATCH, SEQ, D_MODEL), F32),
        "x_sample": nrm(ks[1], (DEC_BATCH, DEC_SEQ, D_MODEL), F32),
        "cache_k_win": nrm(ks[2], (DEPTH, DEC_BATCH, w_buf, N_KV, HEAD_DIM), F32),
        "cache_v_win": nrm(ks[3], (DEPTH, DEC_BATCH, w_buf, N_KV, HEAD_DIM), F32),
        "state_gla": nrm(ks[4], (DEPTH, DEC_BATCH, N_HEADS_B, DK_B, DV_B), F32),
        "c_prompt": nrm(ks[5], (BATCH, D_MODEL), F32),
        "c_sample": nrm(ks[6], (DEC_BATCH, D_MODEL), F32),
        "w_ada": nrm(ks[7], (DEPTH, D_MODEL, 3 * D_MODEL), F32) * (0.5 * D_MODEL ** -0.5),
        "b_ada": nrm(ks[8], (DEPTH, 3 * D_MODEL), F32) * 0.01,
        "norm_gain": 1.0 + 0.01 * nrm(ks[9], (DEPTH, D_MODEL), F32),
        "w_in": nrm(ks[10], (DEPTH, D_MODEL, PROJ_DIM), F32) * (D_MODEL ** -0.5),
        "attn_sink": nrm(ks[11], (DEPTH, N_HEADS_A), F32),
        "w_alpha2": nrm(ks[12], (DEPTH, GATE_RANK, KEY_B), F32) * (GATE_RANK ** -0.5),
        "b_alpha": nrm(ks[13], (DEPTH, KEY_B), F32) * 0.1,
        "gla_norm_gain": 1.0 + 0.01 * nrm(ks[14], (DEPTH, WIDTH_B), F32),
        "w_proj_a": nrm(ks[15], (DEPTH, WIDTH_A, D_MODEL), F32) * (WIDTH_A ** -0.5),
        "w_proj_b": nrm(ks[16], (DEPTH, WIDTH_B, D_MODEL), F32) * (WIDTH_B ** -0.5),
        "w_out": nrm(ks[17], (DEPTH, D_MODEL, D_MODEL), F32) * (D_MODEL ** -0.5),
        "final_norm_gain": 1.0 + 0.01 * nrm(ks[18], (D_MODEL,), F32),
    }


def reference(x_prompt, x_sample, cache_k_win, cache_v_win, state_gla, c_prompt, c_sample,
              w_ada, b_ada, norm_gain, w_in, attn_sink, w_alpha2, b_alpha, gla_norm_gain,
              w_proj_a, w_proj_b, w_out, final_norm_gain):
    xp, xs = x_prompt, x_sample
    kp_l, vp_l, sp_l, ks_l, vs_l, ss_l = [], [], [], [], [], []
    for l in range(DEPTH):
        parts, gate = mixer_input(xp, c_prompt, w_ada[l], b_ada[l], norm_gain[l], w_in[l])
        qa, ka, va, za, qb, kb, vb, zb, log_a, ga, gb = branch_tensors(parts, w_alpha2[l], b_alpha[l])
        o_a, k_win, v_win = swa_prompt(qa, ka, va, attn_sink[l])
        s0 = jnp.zeros((xp.shape[0], N_HEADS_B, DK_B, DV_B), F32)
        o_b, s_fin = gla_recurrent(qb, kb, vb, log_a, s0, GLA_CHUNK)
        xp = mixer_output(xp, gate, o_a, za, o_b, zb, ga, gb, gla_norm_gain[l], w_proj_a[l], w_proj_b[l], w_out[l])
        kp_l.append(k_win)
        vp_l.append(v_win)
        sp_l.append(s_fin.astype(x_prompt.dtype))
        parts, gate = mixer_input(xs, c_sample, w_ada[l], b_ada[l], norm_gain[l], w_in[l])
        qa, ka, va, za, qb, kb, vb, zb, log_a, ga, gb = branch_tensors(parts, w_alpha2[l], b_alpha[l])
        o_a, k_win, v_win = swa_sample(qa, ka, va, cache_k_win[l], cache_v_win[l], attn_sink[l])
        o_b, s_fin = gla_recurrent(qb, kb, vb, log_a, state_gla[l], xs.shape[1])
        xs = mixer_output(xs, gate, o_a, za, o_b, zb, ga, gb, gla_norm_gain[l], w_proj_a[l], w_proj_b[l], w_out[l])
        ks_l.append(k_win)
        vs_l.append(v_win)
        ss_l.append(s_fin.astype(state_gla.dtype))
    y_prompt = rms_norm(xp, final_norm_gain)
    y_sample = rms_norm(xs, final_norm_gain)
    k_win_prompt = jnp.stack(kp_l, axis=0)
    v_win_prompt = jnp.stack(vp_l, axis=0)
    gla_state_prompt = jnp.stack(sp_l, axis=0)
    k_win_sample = jnp.stack(ks_l, axis=0)
    v_win_sample = jnp.stack(vs_l, axis=0)
    gla_state_sample = jnp.stack(ss_l, axis=0)
    return (y_prompt, y_sample, k_win_prompt, v_win_prompt, gla_state_prompt, k_win_sample, v_win_sample, gla_state_sample)
```

```python
import functools

import jax
import jax.numpy as jnp
from jax import lax
from jax.experimental import pallas as pl
from jax.experimental.pallas import tpu as pltpu

F32 = jnp.float32
BF16 = jnp.bfloat16

D_MODEL = 4096
WIDTH_A = 2048
HEAD_DIM = 64
N_HEADS_A = 32
N_KV = 4
GROUP = 8
WINDOW = 128
ATTN_BLOCK = 128
WIDTH_B = 2048
N_HEADS_B = 4
DV_B = 512
KEY_B = 1024
DK_B = 256
GATE_RANK = 16
GLA_TAU = 16.0
GLA_CHUNK = 64
EPS = 1e-6
NEG_INF = -1e30
KV_W = N_KV * HEAD_DIM

V7X_VMEM_BYTES = 64 * 1024 * 1024
VMEM_LIMIT = 56 * 1024 * 1024
LANE = 128

OFF_QA = 0
OFF_ZA = 2048
OFF_VB = 4096
OFF_ZB = 6144
OFF_QB = 8192
OFF_KB = 9216
OFF_GA = 10240
OFF_GB = 14336
OFF_KA = 18432
OFF_VA = 18688
PROJ_COLS = 18944


def _cparams(sem):
    return pltpu.CompilerParams(dimension_semantics=sem, vmem_limit_bytes=VMEM_LIMIT)


def _silu(x):
    return x * jax.nn.sigmoid(x)


def _adaln_kernel(c_ref, w_ref, b_ref, o_ref):
    a = _silu(c_ref[...]).astype(BF16)
    w = w_ref[...].astype(BF16)
    o_ref[...] = jnp.dot(a, w, preferred_element_type=F32) + b_ref[...]


def _adaln(c_all, w_ada, b_ada, tn=512):
    m, k = c_all.shape
    n = w_ada.shape[1]
    return pl.pallas_call(
        _adaln_kernel,
        out_shape=jax.ShapeDtypeStruct((m, n), F32),
        grid=(n // tn,),
        in_specs=[
            pl.BlockSpec((m, k), lambda j: (0, 0)),
            pl.BlockSpec((k, tn), lambda j: (0, j)),
            pl.BlockSpec((1, tn), lambda j: (0, j)),
        ],
        out_specs=pl.BlockSpec((m, tn), lambda j: (0, j)),
        compiler_params=_cparams(("arbitrary",)),
        name="adaln",
    )(c_all, w_ada, b_ada)


def _prologue_kernel(x_ref, g_ref, scale_ref, shift_ref, wr_ref, h_ref, r_ref):
    x = x_ref[...]
    ms = jnp.mean(x * x, axis=-1, keepdims=True)
    y = x * lax.rsqrt(ms + EPS) * g_ref[...]
    h = y * (1.0 + scale_ref[...]) + shift_ref[...]
    hb = h.reshape(h.shape[0] * h.shape[1], h.shape[2]).astype(BF16)
    h_ref[...] = hb
    r_ref[...] = jnp.dot(hb, wr_ref[...], preferred_element_type=F32)


def _prologue(x3, gain, scale3, shift3, w_r, g_blk, r_blk):
    n, l, d = x3.shape
    rows = g_blk * r_blk
    grid = (n // g_blk, l // r_blk)
    lb = l // r_blk
    return pl.pallas_call(
        _prologue_kernel,
        out_shape=(jax.ShapeDtypeStruct((n * l, d), BF16),
                   jax.ShapeDtypeStruct((n * l, LANE), F32)),
        grid=grid,
        in_specs=[
            pl.BlockSpec((g_blk, r_blk, d), lambda i, j: (i, j, 0)),
            pl.BlockSpec((1, 1, d), lambda i, j: (0, 0, 0)),
            pl.BlockSpec((g_blk, 1, d), lambda i, j: (i, 0, 0)),
            pl.BlockSpec((g_blk, 1, d), lambda i, j: (i, 0, 0)),
            pl.BlockSpec((d, LANE), lambda i, j: (0, 0)),
        ],
        out_specs=(pl.BlockSpec((rows, d), lambda i, j: (i * lb + j, 0)),
                   pl.BlockSpec((rows, LANE), lambda i, j: (i * lb + j, 0))),
        compiler_params=_cparams(("arbitrary", "arbitrary")),
        name="prologue",
    )(x3, gain, scale3, shift3, w_r)


def _mm_kernel(a_ref, b_ref, o_ref):
    o_ref[...] = jnp.dot(a_ref[...], b_ref[...], preferred_element_type=F32).astype(o_ref.dtype)


def _matmul(a, b, tm, tn, out_dtype=F32):
    m, k = a.shape
    n = b.shape[1]
    return pl.pallas_call(
        _mm_kernel,
        out_shape=jax.ShapeDtypeStruct((m, n), out_dtype),
        grid=(m // tm, n // tn),
        in_specs=[
            pl.BlockSpec((tm, k), lambda i, j: (i, 0)),
            pl.BlockSpec((k, tn), lambda i, j: (0, j)),
        ],
        out_specs=pl.BlockSpec((tm, tn), lambda i, j: (i, j)),
        compiler_params=_cparams(("arbitrary", "arbitrary")),
        name="in_proj",
    )(a, b)


def _sink_attend(q, k, v, mask, sink):
    s = lax.dot_general(q, k, (((1,), (1,)), ((), ())), preferred_element_type=F32)
    s = jnp.where(mask, s * (HEAD_DIM ** -0.5), NEG_INF)
    m = jnp.maximum(jnp.max(s, axis=-1, keepdims=True), sink)
    p = jnp.exp(s - m)
    denom = jnp.sum(p, axis=-1, keepdims=True) + jnp.exp(sink - m)
    o = jnp.dot(p.astype(BF16), v, preferred_element_type=F32)
    return o / denom


def _swa_prompt_kernel(sink_ref, q_ref, za_ref, kp_ref, kc_ref, vp_ref, vc_ref, o_ref):
    n = pl.program_id(1)
    q = q_ref[0].astype(BF16)
    k = jnp.concatenate([kp_ref[0], kc_ref[0]], axis=0).astype(BF16)
    v = jnp.concatenate([vp_ref[0], vc_ref[0]], axis=0).astype(BF16)
    i = lax.broadcasted_iota(jnp.int32, (ATTN_BLOCK, 2 * ATTN_BLOCK), 0)
    j = lax.broadcasted_iota(jnp.int32, (ATTN_BLOCK, 2 * ATTN_BLOCK), 1)
    mask = (j >= i) & (j <= i + WINDOW) & ((j >= ATTN_BLOCK) | (n > 0))
    outs = []
    for h in range(N_KV):
        kh = k[:, h * HEAD_DIM:(h + 1) * HEAD_DIM]
        vh = v[:, h * HEAD_DIM:(h + 1) * HEAD_DIM]
        for g in range(GROUP):
            hd = h * GROUP + g
            qg = q[:, hd * HEAD_DIM:(hd + 1) * HEAD_DIM]
            outs.append(_sink_attend(qg, kh, vh, mask, sink_ref[hd]))
    o = jnp.concatenate(outs, axis=1)
    o_ref[0] = (o * _silu(za_ref[0])).astype(BF16)


def _swa_prompt(sink, p3):
    b, l, _ = p3.shape
    nb = l // ATTN_BLOCK
    kcol = OFF_KA // KV_W
    vcol = OFF_VA // KV_W
    blk_q = (1, ATTN_BLOCK, WIDTH_A)
    blk_kv = (1, ATTN_BLOCK, KV_W)
    prev = lambda n: jnp.maximum(n - 1, 0)
    return pl.pallas_call(
        _swa_prompt_kernel,
        out_shape=jax.ShapeDtypeStruct((b, l, WIDTH_A), BF16),
        grid=(b, nb),
        in_specs=[
            pl.BlockSpec(memory_space=pltpu.SMEM),
            pl.BlockSpec(blk_q, lambda bi, n: (bi, n, OFF_QA // WIDTH_A)),
            pl.BlockSpec(blk_q, lambda bi, n: (bi, n, OFF_ZA // WIDTH_A)),
            pl.BlockSpec(blk_kv, lambda bi, n: (bi, prev(n), kcol)),
            pl.BlockSpec(blk_kv, lambda bi, n: (bi, n, kcol)),
            pl.BlockSpec(blk_kv, lambda bi, n: (bi, prev(n), vcol)),
            pl.BlockSpec(blk_kv, lambda bi, n: (bi, n, vcol)),
        ],
        out_specs=pl.BlockSpec(blk_q, lambda bi, n: (bi, n, 0)),
        compiler_params=_cparams(("arbitrary", "arbitrary")),
        name="swa_prompt",
    )(sink, p3, p3, p3, p3, p3, p3)


def _swa_sample_kernel(g_blk, t_len, w_len, sink_ref, q_ref, za_ref, kn_ref, vn_ref,
                       kc_ref, vc_ref, o_ref, kw_ref, vw_ref):
    kpad = 2 * WINDOW - w_len - t_len
    rows = GROUP * t_len
    r = lax.broadcasted_iota(jnp.int32, (rows, 2 * WINDOW), 0)
    j = lax.broadcasted_iota(jnp.int32, (rows, 2 * WINDOW), 1)
    t = r % t_len
    mask = (j <= w_len + t) & (j >= w_len + t - WINDOW) & (j < w_len + t_len)
    rg = lax.broadcasted_iota(jnp.int32, (rows, 1), 0) // t_len
    sink_cols = []
    for h in range(N_KV):
        col = jnp.zeros((rows, 1), F32)
        for g in range(GROUP):
            col = jnp.where(rg == g, sink_ref[h * GROUP + g], col)
        sink_cols.append(col)
    zpad = jnp.zeros((kpad, KV_W), F32)
    seq_outs = []
    for s in range(g_blk):
        kc = kc_ref[s]
        vc = vc_ref[s]
        kn = kn_ref[s]
        vn = vn_ref[s]
        kw_ref[s] = jnp.concatenate([kc[t_len:], kn], axis=0)
        vw_ref[s] = jnp.concatenate([vc[t_len:], vn], axis=0)
        k_all = jnp.concatenate([kc, kn, zpad], axis=0).astype(BF16)
        v_all = jnp.concatenate([vc, vn, zpad], axis=0).astype(BF16)
        q = q_ref[s].astype(BF16)
        pieces = []
        for h in range(N_KV):
            kh = k_all[:, h * HEAD_DIM:(h + 1) * HEAD_DIM]
            vh = v_all[:, h * HEAD_DIM:(h + 1) * HEAD_DIM]
            qh = jnp.concatenate(
                [q[:, (h * GROUP + g) * HEAD_DIM:(h * GROUP + g + 1) * HEAD_DIM] for g in range(GROUP)],
                axis=0)
            oh = _sink_attend(qh, kh, vh, mask, sink_cols[h])
            for g in range(GROUP):
                pieces.append(oh[g * t_len:(g + 1) * t_len])
        o = jnp.concatenate(pieces, axis=1)
        seq_outs.append(o * _silu(za_ref[s]))
    o_ref[...] = jnp.concatenate(seq_outs, axis=0).astype(BF16)


def _swa_sample(sink, p3, cache_k, cache_v, g_blk=2):
    n, t_len, _ = p3.shape
    w_len = cache_k.shape[1]
    kcol = OFF_KA // KV_W
    vcol = OFF_VA // KV_W
    blk_q = (g_blk, t_len, WIDTH_A)
    blk_n = (g_blk, t_len, KV_W)
    blk_c = (g_blk, w_len, KV_W)
    return pl.pallas_call(
        functools.partial(_swa_sample_kernel, g_blk, t_len, w_len),
        out_shape=(jax.ShapeDtypeStruct((n * t_len, WIDTH_A), BF16),
                   jax.ShapeDtypeStruct((n, w_len, KV_W), F32),
                   jax.ShapeDtypeStruct((n, w_len, KV_W), F32)),
        grid=(n // g_blk,),
        in_specs=[
            pl.BlockSpec(memory_space=pltpu.SMEM),
            pl.BlockSpec(blk_q, lambda i: (i, 0, OFF_QA // WIDTH_A)),
            pl.BlockSpec(blk_q, lambda i: (i, 0, OFF_ZA // WIDTH_A)),
            pl.BlockSpec(blk_n, lambda i: (i, 0, kcol)),
            pl.BlockSpec(blk_n, lambda i: (i, 0, vcol)),
            pl.BlockSpec(blk_c, lambda i: (i, 0, 0)),
            pl.BlockSpec(blk_c, lambda i: (i, 0, 0)),
        ],
        out_specs=(pl.BlockSpec((g_blk * t_len, WIDTH_A), lambda i: (i, 0)),
                   pl.BlockSpec(blk_c, lambda i: (i, 0, 0)),
                   pl.BlockSpec(blk_c, lambda i: (i, 0, 0))),
        compiler_params=_cparams(("arbitrary",)),
        name="swa_sample",
    )(sink, p3, p3, p3, p3, cache_k, cache_v)


def _split_hi_lo(x):
    hi = x.astype(BF16)
    lo = (x - hi.astype(F32)).astype(BF16)
    return hi, lo


def _gla_chunk(q, k, v, la, s_old, tri, ones_cl, causal):
    la_hi, la_lo = _split_hi_lo(la)
    b = (jnp.dot(tri, la_hi, preferred_element_type=F32)
         + jnp.dot(tri, la_lo, preferred_element_type=F32))
    c = b.shape[0]
    b_last = b[c - 1:c, :]
    tn = (((0,), (0,)), ((), ()))
    bl_col = (lax.dot_general(la_hi, ones_cl, tn, preferred_element_type=F32)
              + lax.dot_general(la_lo, ones_cl, tn, preferred_element_type=F32))
    q_t = (q * jnp.exp(b) * (DK_B ** -0.5)).astype(BF16)
    k_t = (k * jnp.exp(-b)).astype(BF16)
    k_d = (k * jnp.exp(b_last - b)).astype(BF16)
    vb = v.astype(BF16)
    att = lax.dot_general(q_t, k_t, (((1,), (1,)), ((), ())), preferred_element_type=F32)
    att = jnp.where(causal, att, 0.0).astype(BF16)
    o = (jnp.dot(att, vb, preferred_element_type=F32)
         + jnp.dot(q_t, s_old.astype(BF16), preferred_element_type=F32))
    decay = jnp.exp(bl_col)
    decay = jnp.concatenate([decay] * (DV_B // LANE), axis=1)
    s_new = decay * s_old + lax.dot_general(k_d, vb, tn, preferred_element_type=F32)
    return o, s_new


def _log_decay(r, w2, b2):
    z = jnp.dot(r.astype(BF16), w2, preferred_element_type=F32) + b2
    ls = jnp.minimum(z, 0.0) - jnp.log(1.0 + jnp.exp(-jnp.abs(z)))
    return ls / GLA_TAU


def _gla_finish(o, zb, gain):
    on = o * lax.rsqrt(jnp.mean(o * o, axis=-1, keepdims=True) + EPS) * gain
    return on * _silu(zb)


def _tri_consts(c):
    ri = lax.broadcasted_iota(jnp.int32, (c, c), 0)
    ci = lax.broadcasted_iota(jnp.int32, (c, c), 1)
    causal = ri >= ci
    tri = jnp.where(causal, 1.0, 0.0).astype(BF16)
    ones_cl = jnp.ones((c, LANE), BF16)
    return tri, ones_cl, causal


def _gla_prompt_kernel(q_ref, k_ref, v_ref, zb_ref, r_ref, w2_ref, b2_ref, gain_ref, o_ref, s_ref):
    cidx = pl.program_id(1)

    @pl.when(cidx == 0)
    def _():
        s_ref[...] = jnp.zeros_like(s_ref)

    c = q_ref.shape[1]
    tri, ones_cl, causal = _tri_consts(c)
    la_all = _log_decay(r_ref[0], w2_ref[...], b2_ref[...])
    outs = []
    for h in range(N_HEADS_B):
        dk = slice(h * DK_B, (h + 1) * DK_B)
        dv = slice(h * DV_B, (h + 1) * DV_B)
        o, s_new = _gla_chunk(q_ref[0, :, dk], k_ref[0, :, dk], v_ref[0, :, dv], la_all[:, dk],
                              s_ref[0, h], tri, ones_cl, causal)
        s_ref[0, h] = s_new
        outs.append(_gla_finish(o, zb_ref[0, :, dv], gain_ref[:, dv]))
    o_ref[0] = jnp.concatenate(outs, axis=1).astype(BF16)


def _gla_prompt(p3, r3, w2, b2, gain):
    b, l, _ = p3.shape
    c = GLA_CHUNK
    return pl.pallas_call(
        _gla_prompt_kernel,
        out_shape=(jax.ShapeDtypeStruct((b, l, WIDTH_B), BF16),
                   jax.ShapeDtypeStruct((b, N_HEADS_B, DK_B, DV_B), F32)),
        grid=(b, l // c),
        in_specs=[
            pl.BlockSpec((1, c, KEY_B), lambda bi, ci: (bi, ci, OFF_QB // KEY_B)),
            pl.BlockSpec((1, c, KEY_B), lambda bi, ci: (bi, ci, OFF_KB // KEY_B)),
            pl.BlockSpec((1, c, WIDTH_B), lambda bi, ci: (bi, ci, OFF_VB // WIDTH_B)),
            pl.BlockSpec((1, c, WIDTH_B), lambda bi, ci: (bi, ci, OFF_ZB // WIDTH_B)),
            pl.BlockSpec((1, c, LANE), lambda bi, ci: (bi, ci, 0)),
            pl.BlockSpec((LANE, KEY_B), lambda bi, ci: (0, 0)),
            pl.BlockSpec((1, KEY_B), lambda bi, ci: (0, 0)),
            pl.BlockSpec((1, WIDTH_B), lambda bi, ci: (0, 0)),
        ],
        out_specs=(pl.BlockSpec((1, c, WIDTH_B), lambda bi, ci: (bi, ci, 0)),
                   pl.BlockSpec((1, N_HEADS_B, DK_B, DV_B), lambda bi, ci: (bi, 0, 0, 0))),
        compiler_params=_cparams(("arbitrary", "arbitrary")),
        name="gla_prompt",
    )(p3, p3, p3, p3, r3, w2, b2, gain)


def _gla_sample_kernel(g_blk, q_ref, k_ref, v_ref, zb_ref, r_ref, w2_ref, b2_ref, gain_ref, s0_ref,
                       o_ref, s_ref):
    t_len = q_ref.shape[1]
    c = 2 * t_len
    tri, ones_cl, causal = _tri_consts(c)

    def pad(x):
        return jnp.concatenate([x, jnp.zeros_like(x)], axis=0)

    live = lax.broadcasted_iota(jnp.int32, (c, 1), 0) < t_len
    seq_outs = []
    for s in range(g_blk):
        la_all = jnp.where(live, _log_decay(pad(r_ref[s]), w2_ref[...], b2_ref[...]), 0.0)
        outs = []
        for h in range(N_HEADS_B):
            dk = slice(h * DK_B, (h + 1) * DK_B)
            dv = slice(h * DV_B, (h + 1) * DV_B)
            o, s_new = _gla_chunk(pad(q_ref[s, :, dk]), pad(k_ref[s, :, dk]), pad(v_ref[s, :, dv]),
                                  la_all[:, dk], s0_ref[s, h], tri, ones_cl, causal)
            s_ref[s, h] = s_new
            outs.append(_gla_finish(o[:t_len], zb_ref[s, :, dv], gain_ref[:, dv]))
        seq_outs.append(jnp.concatenate(outs, axis=1))
    o_ref[...] = jnp.concatenate(seq_outs, axis=0).astype(BF16)


def _gla_sample(p3, r3, w2, b2, gain, state, g_blk=2):
    n, t_len, _ = p3.shape
    st_blk = (g_blk, N_HEADS_B, DK_B, DV_B)
    return pl.pallas_call(
        functools.partial(_gla_sample_kernel, g_blk),
        out_shape=(jax.ShapeDtypeStruct((n * t_len, WIDTH_B), BF16),
                   jax.ShapeDtypeStruct(state.shape, F32)),
        grid=(n // g_blk,),
        in_specs=[
            pl.BlockSpec((g_blk, t_len, KEY_B), lambda i: (i, 0, OFF_QB // KEY_B)),
            pl.BlockSpec((g_blk, t_len, KEY_B), lambda i: (i, 0, OFF_KB // KEY_B)),
            pl.BlockSpec((g_blk, t_len, WIDTH_B), lambda i: (i, 0, OFF_VB // WIDTH_B)),
            pl.BlockSpec((g_blk, t_len, WIDTH_B), lambda i: (i, 0, OFF_ZB // WIDTH_B)),
            pl.BlockSpec((g_blk, t_len, LANE), lambda i: (i, 0, 0)),
            pl.BlockSpec((LANE, KEY_B), lambda i: (0, 0)),
            pl.BlockSpec((1, KEY_B), lambda i: (0, 0)),
            pl.BlockSpec((1, WIDTH_B), lambda i: (0, 0)),
            pl.BlockSpec(st_blk, lambda i: (i, 0, 0, 0)),
        ],
        out_specs=(pl.BlockSpec((g_blk * t_len, WIDTH_B), lambda i: (i, 0)),
                   pl.BlockSpec(st_blk, lambda i: (i, 0, 0, 0))),
        compiler_params=_cparams(("arbitrary",)),
        name="gla_sample",
    )(p3, p3, p3, p3, r3, w2, b2, gain, state)


def _merge_kernel(a_ref, b_ref, wa_ref, wb_ref, ga_ref, gb_ref, o_ref):
    ua = jnp.dot(a_ref[...], wa_ref[...], preferred_element_type=F32)
    ub = jnp.dot(b_ref[...], wb_ref[...], preferred_element_type=F32)
    merged = jax.nn.sigmoid(ga_ref[...]) * ua + jax.nn.sigmoid(gb_ref[...]) * ub
    o_ref[...] = merged.astype(BF16)


def _merge(a_in, b_in, w_pa, w_pb, p2, tm, tn):
    m = a_in.shape[0]
    return pl.pallas_call(
        _merge_kernel,
        out_shape=jax.ShapeDtypeStruct((m, D_MODEL), BF16),
        grid=(m // tm, D_MODEL // tn),
        in_specs=[
            pl.BlockSpec((tm, WIDTH_A), lambda i, j: (i, 0)),
            pl.BlockSpec((tm, WIDTH_B), lambda i, j: (i, 0)),
            pl.BlockSpec((WIDTH_A, tn), lambda i, j: (0, j)),
            pl.BlockSpec((WIDTH_B, tn), lambda i, j: (0, j)),
            pl.BlockSpec((tm, tn), lambda i, j: (i, OFF_GA // tn + j)),
            pl.BlockSpec((tm, tn), lambda i, j: (i, OFF_GB // tn + j)),
        ],
        out_specs=pl.BlockSpec((tm, tn), lambda i, j: (i, j)),
        compiler_params=_cparams(("arbitrary", "arbitrary")),
        name="merge",
    )(a_in, b_in, w_pa, w_pb, p2, p2)


def _out_kernel(tn, m_ref, w_ref, x_ref, gate_ref, fg_ref, o_ref):
    j = pl.program_id(2)
    y = jnp.dot(m_ref[...], w_ref[...], preferred_element_type=F32)
    g_blk, r_blk = x_ref.shape[0], x_ref.shape[1]
    y3 = y.reshape(g_blk, r_blk, tn)
    col = pl.multiple_of(j * tn, tn)
    o_ref[:, :, pl.ds(col, tn)] = x_ref[...] + gate_ref[...] * y3

    @pl.when(j == pl.num_programs(2) - 1)
    def _():
        xn = o_ref[...]
        ms = jnp.mean(xn * xn, axis=-1, keepdims=True)
        o_ref[...] = xn * lax.rsqrt(ms + EPS) * fg_ref[...]


def _out_proj(merged, w_o, x3, gate3, fin_gain, g_blk, r_blk, tn):
    n, l, d = x3.shape
    rows = g_blk * r_blk
    lb = l // r_blk
    return pl.pallas_call(
        functools.partial(_out_kernel, tn),
        out_shape=jax.ShapeDtypeStruct((n, l, d), F32),
        grid=(n // g_blk, lb, d // tn),
        in_specs=[
            pl.BlockSpec((rows, d), lambda i, r, j: (i * lb + r, 0)),
            pl.BlockSpec((d, tn), lambda i, r, j: (0, j)),
            pl.BlockSpec((g_blk, r_blk, tn), lambda i, r, j: (i, r, j)),
            pl.BlockSpec((g_blk, 1, tn), lambda i, r, j: (i, 0, j)),
            pl.BlockSpec((1, 1, d), lambda i, r, j: (0, 0, 0)),
        ],
        out_specs=pl.BlockSpec((g_blk, r_blk, d), lambda i, r, j: (i, r, 0)),
        compiler_params=_cparams(("arbitrary", "arbitrary", "arbitrary")),
        name="out_proj",
    )(merged, w_o, x3, gate3, fin_gain)


def kernel(x_prompt, x_sample, cache_k_win, cache_v_win, state_gla, c_prompt, c_sample,
           w_ada, b_ada, norm_gain, w_in, attn_sink, w_alpha2, b_alpha, gla_norm_gain,
           w_proj_a, w_proj_b, w_out, final_norm_gain):
    assert w_ada.shape[0] == 1, "single-layer step"
    bp, lp, d = x_prompt.shape
    ns, ts, _ = x_sample.shape
    w_len = cache_k_win.shape[2]

    wi = w_in[0]
    o = [0]
    for sz in (WIDTH_A, KV_W, KV_W, WIDTH_A, KEY_B, KEY_B, WIDTH_B, WIDTH_B, GATE_RANK, D_MODEL, D_MODEL):
        o.append(o[-1] + sz)
    qa, ka, va, za, qb, kb, vb, zb, rr, ga, gb = [slice(o[i], o[i + 1]) for i in range(11)]
    w_cat = jnp.concatenate(
        [wi[:, qa], wi[:, za], wi[:, vb], wi[:, zb], wi[:, qb], wi[:, kb], wi[:, ga], wi[:, gb],
         wi[:, ka], wi[:, va]], axis=1).astype(BF16)
    w_r = jnp.pad(wi[:, rr], ((0, 0), (0, LANE - GATE_RANK))).astype(BF16)
    w2 = jnp.pad(w_alpha2[0], ((0, LANE - GATE_RANK), (0, 0))).astype(BF16)
    b2 = b_alpha[0].reshape(1, KEY_B)
    gla_gain = gla_norm_gain[0].reshape(1, WIDTH_B)
    w_pa = w_proj_a[0].astype(BF16)
    w_pb = w_proj_b[0].astype(BF16)
    w_o = w_out[0].astype(BF16)
    gain3 = norm_gain[0].reshape(1, 1, d)
    fin3 = final_norm_gain.reshape(1, 1, d)
    sink = attn_sink[0]

    m_all = ns + bp
    m_pad = -(-m_all // 8) * 8
    c_all = jnp.concatenate([c_sample, c_prompt, jnp.zeros((m_pad - m_all, d), F32)], axis=0)
    mod = _adaln(c_all, w_ada[0], b_ada[0].reshape(1, 3 * d))
    shift, scale, gate = mod[:, :d], mod[:, d:2 * d], mod[:, 2 * d:]

    def group(x3, mod_rows, g_blk, r_blk, mm_tm):
        n, l, _ = x3.shape
        sh = shift[mod_rows].reshape(n, 1, d)
        sc = scale[mod_rows].reshape(n, 1, d)
        gt = gate[mod_rows].reshape(n, 1, d)
        h, r = _prologue(x3, gain3, sc, sh, w_r, g_blk, r_blk)
        p = _matmul(h, w_cat, mm_tm, 512)
        return p, r, gt

    p_p, r_p, gate_p = group(x_prompt, slice(ns, ns + bp), 1, 512, 1024)
    p_p3 = p_p.reshape(bp, lp, PROJ_COLS)
    ua_p = _swa_prompt(sink, p_p3)
    ub_p, s_p = _gla_prompt(p_p3, r_p.reshape(bp, lp, LANE), w2, b2, gla_gain)
    merged_p = _merge(ua_p.reshape(bp * lp, WIDTH_A), ub_p.reshape(bp * lp, WIDTH_B), w_pa, w_pb, p_p, 1024, 512)
    y_p = _out_proj(merged_p, w_o, x_prompt, gate_p, fin3, 1, 512, 512)
    k_win_p = p_p3[:, lp - WINDOW:, OFF_KA:OFF_KA + KV_W].reshape(1, bp, WINDOW, N_KV, HEAD_DIM)
    v_win_p = p_p3[:, lp - WINDOW:, OFF_VA:OFF_VA + KV_W].reshape(1, bp, WINDOW, N_KV, HEAD_DIM)

    p_s, r_s, gate_s = group(x_sample, slice(0, ns), 64, ts, ns * ts)
    p_s3 = p_s.reshape(ns, ts, PROJ_COLS)
    ua_s, k_win_s, v_win_s = _swa_sample(sink, p_s3, cache_k_win[0].reshape(ns, w_len, KV_W),
                                         cache_v_win[0].reshape(ns, w_len, KV_W))
    ub_s, s_s = _gla_sample(p_s3, r_s.reshape(ns, ts, LANE), w2, b2, gla_gain, state_gla[0])
    merged_s = _merge(ua_s, ub_s, w_pa, w_pb, p_s, ns * ts, 512)
    y_s = _out_proj(merged_s, w_o, x_sample, gate_s, fin3, 64, ts, 512)

    return (y_p, y_s, k_win_p, v_win_p, s_p[None],
            k_win_s.reshape(1, ns, w_len, N_KV, HEAD_DIM), v_win_s.reshape(1, ns, w_len, N_KV, HEAD_DIM),
            s_s[None])
```

```python
import functools

import jax
import jax.numpy as jnp
from jax import lax
from jax.experimental import pallas as pl
from jax.experimental.pallas import tpu as pltpu

F32 = jnp.float32
BF16 = jnp.bfloat16

D_MODEL = 4096
WIDTH_A = 2048
HEAD_DIM = 64
N_HEADS_A = 32
N_KV = 4
GROUP = 8
WINDOW = 128
ATTN_BLOCK = 128
WIDTH_B = 2048
N_HEADS_B = 4
DV_B = 512
KEY_B = 1024
DK_B = 256
GATE_RANK = 16
GLA_TAU = 16.0
GLA_CHUNK = 64
EPS = 1e-6
NEG_INF = -1e30
KV_W = N_KV * HEAD_DIM

V7X_VMEM_BYTES = 64 * 1024 * 1024
VMEM_LIMIT = 56 * 1024 * 1024
LANE = 128

SRC_SIZES = (WIDTH_A, KV_W, KV_W, WIDTH_A, KEY_B, KEY_B, WIDTH_B, WIDTH_B, GATE_RANK, D_MODEL, D_MODEL)
SRC_MAIN = WIDTH_A + 2 * KV_W + WIDTH_A + 2 * KEY_B + 2 * WIDTH_B
SRC_GATES = SRC_MAIN + GATE_RANK
PROJ_TN = 512
OFF_QA = 0
OFF_ZA = 2048
OFF_VB = 4096
OFF_ZB = 6144
OFF_QB = 8192
OFF_KB = 9216
OFF_KA = 10240
OFF_VA = 10496
PROJ_DEST = (0, 1, 2, 3, 20, 4, 5, 6, 7, 16, 17, 18, 19, 8, 9, 10, 11, 12, 13, 14, 15)


def _cparams(sem):
    return pltpu.CompilerParams(dimension_semantics=sem, vmem_limit_bytes=VMEM_LIMIT)


def _silu(x):
    return x * jax.nn.sigmoid(x)


def _adaln_kernel(c_ref, w_ref, b_ref, o_ref):
    a = _silu(c_ref[...]).astype(BF16)
    w = w_ref[...].astype(BF16)
    o_ref[...] = jnp.dot(a, w, preferred_element_type=F32) + b_ref[...]


def _adaln(c_all, w_ada, b_ada, tn=512):
    m, k = c_all.shape
    n = w_ada.shape[1]
    return pl.pallas_call(
        _adaln_kernel,
        out_shape=jax.ShapeDtypeStruct((m, n), F32),
        grid=(n // tn,),
        in_specs=[
            pl.BlockSpec((m, k), lambda j: (0, 0)),
            pl.BlockSpec((k, tn), lambda j: (0, j)),
            pl.BlockSpec((1, tn), lambda j: (0, j)),
        ],
        out_specs=pl.BlockSpec((m, tn), lambda j: (0, j)),
        compiler_params=_cparams(("arbitrary",)),
        name="adaln",
    )(c_all, w_ada, b_ada)


def _prologue_kernel(n_ptiles, xp_ref, xs_ref, g_ref, scp_ref, shp_ref, scs_ref, shs_ref, wr_ref,
                     h_ref, r_ref):
    i = pl.program_id(0)

    def emit(x, scale, shift):
        ms = jnp.mean(x * x, axis=-1, keepdims=True)
        y = x * lax.rsqrt(ms + EPS) * g_ref[...]
        h = y * (1.0 + scale) + shift
        hb = h.reshape(h.shape[0] * h.shape[1], h.shape[2]).astype(BF16)
        h_ref[...] = hb
        r_ref[...] = jnp.dot(hb, wr_ref[...], preferred_element_type=F32)

    @pl.when(i < n_ptiles)
    def _():
        emit(xp_ref[...], scp_ref[...], shp_ref[...])

    @pl.when(i >= n_ptiles)
    def _():
        emit(xs_ref[...], scs_ref[...], shs_ref[...])


def _prologue(xp, xs, gain, scp, shp, scs, shs, w_r, rows):
    b, l, d = xp.shape
    n, t, _ = xs.shape
    lb = l // rows
    n_pt = b * lb
    sg = rows // t
    n_st = n // sg
    pt = lambda i: jnp.minimum(i, n_pt - 1)
    st = lambda i: jnp.maximum(i - n_pt, 0)
    return pl.pallas_call(
        functools.partial(_prologue_kernel, n_pt),
        out_shape=(jax.ShapeDtypeStruct((b * l + n * t, d), BF16),
                   jax.ShapeDtypeStruct((b * l + n * t, LANE), F32)),
        grid=(n_pt + n_st,),
        in_specs=[
            pl.BlockSpec((1, rows, d), lambda i: (pt(i) // lb, pt(i) % lb, 0)),
            pl.BlockSpec((sg, t, d), lambda i: (st(i), 0, 0)),
            pl.BlockSpec((1, 1, d), lambda i: (0, 0, 0)),
            pl.BlockSpec((1, 1, d), lambda i: (pt(i) // lb, 0, 0)),
            pl.BlockSpec((1, 1, d), lambda i: (pt(i) // lb, 0, 0)),
            pl.BlockSpec((sg, 1, d), lambda i: (st(i), 0, 0)),
            pl.BlockSpec((sg, 1, d), lambda i: (st(i), 0, 0)),
            pl.BlockSpec((d, LANE), lambda i: (0, 0)),
        ],
        out_specs=(pl.BlockSpec((rows, d), lambda i: (i, 0)),
                   pl.BlockSpec((rows, LANE), lambda i: (i, 0))),
        compiler_params=_cparams(("arbitrary",)),
        name="prologue",
    )(xp, xs, gain, scp, shp, scs, shs, w_r)


def _mm_kernel(dest_ref, a_ref, b_ref, o_ref):
    del dest_ref
    b = b_ref[...]
    if b.dtype != BF16:
        b = b.astype(BF16)
    o_ref[...] = jnp.dot(a_ref[...], b, preferred_element_type=F32).astype(o_ref.dtype)


def _matmul(a, w, dest, tm, tn, out_dtype, name):
    m, k = a.shape
    nt = len(dest)
    return pl.pallas_call(
        _mm_kernel,
        out_shape=jax.ShapeDtypeStruct((m, nt * tn), out_dtype),
        grid_spec=pltpu.PrefetchScalarGridSpec(
            num_scalar_prefetch=1,
            grid=(m // tm, nt),
            in_specs=[
                pl.BlockSpec((tm, k), lambda i, j, d: (i, 0), pipeline_mode=pl.Buffered(1)),
                pl.BlockSpec((k, tn), lambda i, j, d: (0, j)),
            ],
            out_specs=pl.BlockSpec((tm, tn), lambda i, j, d: (i, d[j])),
        ),
        compiler_params=_cparams(("arbitrary", "arbitrary")),
        name=name,
    )(jnp.asarray(dest, jnp.int32), a, w)


def _sink_attend(q, k, v, mask, sink):
    s = lax.dot_general(q, k, (((1,), (1,)), ((), ())), preferred_element_type=F32)
    s = jnp.where(mask, s * (HEAD_DIM ** -0.5), NEG_INF)
    m = jnp.maximum(jnp.max(s, axis=-1, keepdims=True), sink)
    p = jnp.exp(s - m)
    denom = jnp.sum(p, axis=-1, keepdims=True) + jnp.exp(sink - m)
    o = jnp.dot(p.astype(BF16), v, preferred_element_type=F32)
    return o / denom


def _swa_prompt_kernel(sink_ref, q_ref, za_ref, kp_ref, kc_ref, vp_ref, vc_ref, o_ref):
    n = pl.program_id(1)
    q = q_ref[...].astype(BF16)
    k = jnp.concatenate([kp_ref[...], kc_ref[...]], axis=0).astype(BF16)
    v = jnp.concatenate([vp_ref[...], vc_ref[...]], axis=0).astype(BF16)
    i = lax.broadcasted_iota(jnp.int32, (ATTN_BLOCK, 2 * ATTN_BLOCK), 0)
    j = lax.broadcasted_iota(jnp.int32, (ATTN_BLOCK, 2 * ATTN_BLOCK), 1)
    mask = (j >= i) & (j <= i + WINDOW) & ((j >= ATTN_BLOCK) | (n > 0))
    outs = []
    for h in range(N_KV):
        kh = k[:, h * HEAD_DIM:(h + 1) * HEAD_DIM]
        vh = v[:, h * HEAD_DIM:(h + 1) * HEAD_DIM]
        for g in range(GROUP):
            hd = h * GROUP + g
            qg = q[:, hd * HEAD_DIM:(hd + 1) * HEAD_DIM]
            outs.append(_sink_attend(qg, kh, vh, mask, sink_ref[hd]))
    o = jnp.concatenate(outs, axis=1)
    o_ref[...] = (o * _silu(za_ref[...])).astype(BF16)


def _swa_prompt(sink, p, b, l):
    nb = l // ATTN_BLOCK
    kcol = OFF_KA // KV_W
    vcol = OFF_VA // KV_W
    blk_q = (ATTN_BLOCK, WIDTH_A)
    blk_kv = (ATTN_BLOCK, KV_W)
    cur = lambda bi, n: bi * nb + n
    prev = lambda bi, n: bi * nb + jnp.maximum(n - 1, 0)
    return pl.pallas_call(
        _swa_prompt_kernel,
        out_shape=jax.ShapeDtypeStruct((b * l, WIDTH_A), BF16),
        grid=(b, nb),
        in_specs=[
            pl.BlockSpec(memory_space=pltpu.SMEM),
            pl.BlockSpec(blk_q, lambda bi, n: (cur(bi, n), OFF_QA // WIDTH_A)),
            pl.BlockSpec(blk_q, lambda bi, n: (cur(bi, n), OFF_ZA // WIDTH_A)),
            pl.BlockSpec(blk_kv, lambda bi, n: (prev(bi, n), kcol)),
            pl.BlockSpec(blk_kv, lambda bi, n: (cur(bi, n), kcol)),
            pl.BlockSpec(blk_kv, lambda bi, n: (prev(bi, n), vcol)),
            pl.BlockSpec(blk_kv, lambda bi, n: (cur(bi, n), vcol)),
        ],
        out_specs=pl.BlockSpec(blk_q, lambda bi, n: (cur(bi, n), 0)),
        compiler_params=_cparams(("arbitrary", "arbitrary")),
        name="swa_prompt",
    )(sink, p, p, p, p, p, p)


def _swa_sample_kernel(g_blk, t_len, w_len, sink_ref, q_ref, za_ref, kn_ref, vn_ref,
                       kc_ref, vc_ref, o_ref, kw_ref, vw_ref):
    kpad = 2 * WINDOW - w_len - t_len
    rows = GROUP * t_len
    r = lax.broadcasted_iota(jnp.int32, (rows, 2 * WINDOW), 0)
    j = lax.broadcasted_iota(jnp.int32, (rows, 2 * WINDOW), 1)
    t = r % t_len
    mask = (j <= w_len + t) & (j >= w_len + t - WINDOW) & (j < w_len + t_len)
    rg = lax.broadcasted_iota(jnp.int32, (rows, 1), 0) // t_len
    sink_cols = []
    for h in range(N_KV):
        col = jnp.zeros((rows, 1), F32)
        for g in range(GROUP):
            col = jnp.where(rg == g, sink_ref[h * GROUP + g], col)
        sink_cols.append(col)
    zpad = jnp.zeros((kpad, KV_W), F32)
    seq_outs = []
    for s in range(g_blk):
        kc = kc_ref[s]
        vc = vc_ref[s]
        kn = kn_ref[s]
        vn = vn_ref[s]
        kw_ref[s] = jnp.concatenate([kc[t_len:], kn], axis=0)
        vw_ref[s] = jnp.concatenate([vc[t_len:], vn], axis=0)
        k_all = jnp.concatenate([kc, kn, zpad], axis=0).astype(BF16)
        v_all = jnp.concatenate([vc, vn, zpad], axis=0).astype(BF16)
        q = q_ref[s].astype(BF16)
        pieces = []
        for h in range(N_KV):
            kh = k_all[:, h * HEAD_DIM:(h + 1) * HEAD_DIM]
            vh = v_all[:, h * HEAD_DIM:(h + 1) * HEAD_DIM]
            qh = jnp.concatenate(
                [q[:, (h * GROUP + g) * HEAD_DIM:(h * GROUP + g + 1) * HEAD_DIM] for g in range(GROUP)],
                axis=0)
            oh = _sink_attend(qh, kh, vh, mask, sink_cols[h])
            for g in range(GROUP):
                pieces.append(oh[g * t_len:(g + 1) * t_len])
        o = jnp.concatenate(pieces, axis=1)
        seq_outs.append(o * _silu(za_ref[s]))
    o_ref[...] = jnp.concatenate(seq_outs, axis=0).astype(BF16)


def _swa_sample(sink, p3, seq0, n, cache_k, cache_v, g_blk=2):
    t_len = p3.shape[1]
    w_len = cache_k.shape[1]
    kcol = OFF_KA // KV_W
    vcol = OFF_VA // KV_W
    blk_q = (g_blk, t_len, WIDTH_A)
    blk_n = (g_blk, t_len, KV_W)
    blk_c = (g_blk, w_len, KV_W)
    s0 = seq0 // g_blk
    return pl.pallas_call(
        functools.partial(_swa_sample_kernel, g_blk, t_len, w_len),
        out_shape=(jax.ShapeDtypeStruct((n * t_len, WIDTH_A), BF16),
                   jax.ShapeDtypeStruct((n, w_len, KV_W), F32),
                   jax.ShapeDtypeStruct((n, w_len, KV_W), F32)),
        grid=(n // g_blk,),
        in_specs=[
            pl.BlockSpec(memory_space=pltpu.SMEM),
            pl.BlockSpec(blk_q, lambda i: (s0 + i, 0, OFF_QA // WIDTH_A)),
            pl.BlockSpec(blk_q, lambda i: (s0 + i, 0, OFF_ZA // WIDTH_A)),
            pl.BlockSpec(blk_n, lambda i: (s0 + i, 0, kcol)),
            pl.BlockSpec(blk_n, lambda i: (s0 + i, 0, vcol)),
            pl.BlockSpec(blk_c, lambda i: (i, 0, 0)),
            pl.BlockSpec(blk_c, lambda i: (i, 0, 0)),
        ],
        out_specs=(pl.BlockSpec((g_blk * t_len, WIDTH_A), lambda i: (i, 0)),
                   pl.BlockSpec(blk_c, lambda i: (i, 0, 0)),
                   pl.BlockSpec(blk_c, lambda i: (i, 0, 0))),
        compiler_params=_cparams(("arbitrary",)),
        name="swa_sample",
    )(sink, p3, p3, p3, p3, cache_k, cache_v)


def _split_hi_lo(x):
    hi = x.astype(BF16)
    lo = (x - hi.astype(F32)).astype(BF16)
    return hi, lo


def _gla_chunk(q, k, v, la, s_old, tri, ones_cl, causal):
    la_hi, la_lo = _split_hi_lo(la)
    b = (jnp.dot(tri, la_hi, preferred_element_type=F32)
         + jnp.dot(tri, la_lo, preferred_element_type=F32))
    c = b.shape[0]
    b_last = b[c - 1:c, :]
    tn = (((0,), (0,)), ((), ()))
    bl_col = (lax.dot_general(la_hi, ones_cl, tn, preferred_element_type=F32)
              + lax.dot_general(la_lo, ones_cl, tn, preferred_element_type=F32))
    q_t = (q * jnp.exp(b) * (DK_B ** -0.5)).astype(BF16)
    k_t = (k * jnp.exp(-b)).astype(BF16)
    k_d = (k * jnp.exp(b_last - b)).astype(BF16)
    vb = v.astype(BF16)
    att = lax.dot_general(q_t, k_t, (((1,), (1,)), ((), ())), preferred_element_type=F32)
    att = jnp.where(causal, att, 0.0).astype(BF16)
    o = (jnp.dot(att, vb, preferred_element_type=F32)
         + jnp.dot(q_t, s_old.astype(BF16), preferred_element_type=F32))
    decay = jnp.exp(bl_col)
    decay = jnp.concatenate([decay] * (DV_B // LANE), axis=1)
    s_new = decay * s_old + lax.dot_general(k_d, vb, tn, preferred_element_type=F32)
    return o, s_new


def _log_decay(r, w2, b2):
    z = jnp.dot(r.astype(BF16), w2, preferred_element_type=F32) + b2
    ls = jnp.minimum(z, 0.0) - jnp.log(1.0 + jnp.exp(-jnp.abs(z)))
    return ls / GLA_TAU


def _gla_finish(o, zb, gain):
    on = o * lax.rsqrt(jnp.mean(o * o, axis=-1, keepdims=True) + EPS) * gain
    return on * _silu(zb)


def _tri_consts(c):
    ri = lax.broadcasted_iota(jnp.int32, (c, c), 0)
    ci = lax.broadcasted_iota(jnp.int32, (c, c), 1)
    causal = ri >= ci
    tri = jnp.where(causal, 1.0, 0.0).astype(BF16)
    ones_cl = jnp.ones((c, LANE), BF16)
    return tri, ones_cl, causal


def _gla_prompt_kernel(q_ref, k_ref, v_ref, zb_ref, r_ref, w2_ref, b2_ref, gain_ref, o_ref, s_ref):
    cidx = pl.program_id(1)

    @pl.when(cidx == 0)
    def _():
        s_ref[...] = jnp.zeros_like(s_ref)

    c = q_ref.shape[0]
    tri, ones_cl, causal = _tri_consts(c)
    la_all = _log_decay(r_ref[...], w2_ref[...], b2_ref[...])
    outs = []
    for h in range(N_HEADS_B):
        dk = slice(h * DK_B, (h + 1) * DK_B)
        dv = slice(h * DV_B, (h + 1) * DV_B)
        o, s_new = _gla_chunk(q_ref[:, dk], k_ref[:, dk], v_ref[:, dv], la_all[:, dk],
                              s_ref[0, h], tri, ones_cl, causal)
        s_ref[0, h] = s_new
        outs.append(_gla_finish(o, zb_ref[:, dv], gain_ref[:, dv]))
    o_ref[...] = jnp.concatenate(outs, axis=1).astype(BF16)


def _gla_prompt(p, r, w2, b2, gain, b, l):
    c = GLA_CHUNK
    nc = l // c
    row = lambda bi, ci: bi * nc + ci
    return pl.pallas_call(
        _gla_prompt_kernel,
        out_shape=(jax.ShapeDtypeStruct((b * l, WIDTH_B), BF16),
                   jax.ShapeDtypeStruct((b, N_HEADS_B, DK_B, DV_B), F32)),
        grid=(b, nc),
        in_specs=[
            pl.BlockSpec((c, KEY_B), lambda bi, ci: (row(bi, ci), OFF_QB // KEY_B)),
            pl.BlockSpec((c, KEY_B), lambda bi, ci: (row(bi, ci), OFF_KB // KEY_B)),
            pl.BlockSpec((c, WIDTH_B), lambda bi, ci: (row(bi, ci), OFF_VB // WIDTH_B)),
            pl.BlockSpec((c, WIDTH_B), lambda bi, ci: (row(bi, ci), OFF_ZB // WIDTH_B)),
            pl.BlockSpec((c, LANE), lambda bi, ci: (row(bi, ci), 0)),
            pl.BlockSpec((LANE, KEY_B), lambda bi, ci: (0, 0)),
            pl.BlockSpec((1, KEY_B), lambda bi, ci: (0, 0)),
            pl.BlockSpec((1, WIDTH_B), lambda bi, ci: (0, 0)),
        ],
        out_specs=(pl.BlockSpec((c, WIDTH_B), lambda bi, ci: (row(bi, ci), 0)),
                   pl.BlockSpec((1, N_HEADS_B, DK_B, DV_B), lambda bi, ci: (bi, 0, 0, 0))),
        compiler_params=_cparams(("arbitrary", "arbitrary")),
        name="gla_prompt",
    )(p, p, p, p, r, w2, b2, gain)


def _gla_sample_kernel(g_blk, q_ref, k_ref, v_ref, zb_ref, r_ref, w2_ref, b2_ref, gain_ref, s0_ref,
                       o_ref, s_ref):
    t_len = q_ref.shape[1]
    c = 2 * t_len
    tri, ones_cl, causal = _tri_consts(c)

    def pad(x):
        return jnp.concatenate([x, jnp.zeros_like(x)], axis=0)

    live = lax.broadcasted_iota(jnp.int32, (c, 1), 0) < t_len
    seq_outs = []
    for s in range(g_blk):
        la_all = jnp.where(live, _log_decay(pad(r_ref[s]), w2_ref[...], b2_ref[...]), 0.0)
        outs = []
        for h in range(N_HEADS_B):
            dk = slice(h * DK_B, (h + 1) * DK_B)
            dv = slice(h * DV_B, (h + 1) * DV_B)
            o, s_new = _gla_chunk(pad(q_ref[s, :, dk]), pad(k_ref[s, :, dk]), pad(v_ref[s, :, dv]),
                                  la_all[:, dk], s0_ref[s, h], tri, ones_cl, causal)
            s_ref[s, h] = s_new
            outs.append(_gla_finish(o[:t_len], zb_ref[s, :, dv], gain_ref[:, dv]))
        seq_outs.append(jnp.concatenate(outs, axis=1))
    o_ref[...] = jnp.concatenate(seq_outs, axis=0).astype(BF16)


def _gla_sample(p3, r3, seq0, w2, b2, gain, state, g_blk=2):
    t_len = p3.shape[1]
    n = state.shape[0]
    st_blk = (g_blk, N_HEADS_B, DK_B, DV_B)
    s0 = seq0 // g_blk
    return pl.pallas_call(
        functools.partial(_gla_sample_kernel, g_blk),
        out_shape=(jax.ShapeDtypeStruct((n * t_len, WIDTH_B), BF16),
                   jax.ShapeDtypeStruct(state.shape, F32)),
        grid=(n // g_blk,),
        in_specs=[
            pl.BlockSpec((g_blk, t_len, KEY_B), lambda i: (s0 + i, 0, OFF_QB // KEY_B)),
            pl.BlockSpec((g_blk, t_len, KEY_B), lambda i: (s0 + i, 0, OFF_KB // KEY_B)),
            pl.BlockSpec((g_blk, t_len, WIDTH_B), lambda i: (s0 + i, 0, OFF_VB // WIDTH_B)),
            pl.BlockSpec((g_blk, t_len, WIDTH_B), lambda i: (s0 + i, 0, OFF_ZB // WIDTH_B)),
            pl.BlockSpec((g_blk, t_len, LANE), lambda i: (s0 + i, 0, 0)),
            pl.BlockSpec((LANE, KEY_B), lambda i: (0, 0)),
            pl.BlockSpec((1, KEY_B), lambda i: (0, 0)),
            pl.BlockSpec((1, WIDTH_B), lambda i: (0, 0)),
            pl.BlockSpec(st_blk, lambda i: (i, 0, 0, 0)),
        ],
        out_specs=(pl.BlockSpec((g_blk * t_len, WIDTH_B), lambda i: (i, 0)),
                   pl.BlockSpec(st_blk, lambda i: (i, 0, 0, 0))),
        compiler_params=_cparams(("arbitrary",)),
        name="gla_sample",
    )(p3, p3, p3, p3, r3, w2, b2, gain, state)


def _merge_kernel(a_ref, b_ref, wa_ref, wb_ref, ga_ref, gb_ref, o_ref):
    ua = jnp.dot(a_ref[...], wa_ref[...], preferred_element_type=F32)
    ub = jnp.dot(b_ref[...], wb_ref[...], preferred_element_type=F32)
    merged = jax.nn.sigmoid(ga_ref[...]) * ua + jax.nn.sigmoid(gb_ref[...]) * ub
    o_ref[...] = merged.astype(BF16)


def _merge(a_in, b_in, w_pa, w_pb, gates, row0, tm, tn):
    m = a_in.shape[0]
    r0 = row0 // tm
    return pl.pallas_call(
        _merge_kernel,
        out_shape=jax.ShapeDtypeStruct((m, D_MODEL), BF16),
        grid=(m // tm, D_MODEL // tn),
        in_specs=[
            pl.BlockSpec((tm, WIDTH_A), lambda i, j: (i, 0)),
            pl.BlockSpec((tm, WIDTH_B), lambda i, j: (i, 0)),
            pl.BlockSpec((WIDTH_A, tn), lambda i, j: (0, j)),
            pl.BlockSpec((WIDTH_B, tn), lambda i, j: (0, j)),
            pl.BlockSpec((tm, tn), lambda i, j: (r0 + i, j)),
            pl.BlockSpec((tm, tn), lambda i, j: (r0 + i, D_MODEL // tn + j)),
        ],
        out_specs=pl.BlockSpec((tm, tn), lambda i, j: (i, j)),
        compiler_params=_cparams(("arbitrary", "arbitrary")),
        name="merge",
    )(a_in, b_in, w_pa, w_pb, gates, gates)


def _out_kernel(tn, m_ref, w_ref, x_ref, gate_ref, fg_ref, o_ref):
    j = pl.program_id(2)
    y = jnp.dot(m_ref[...], w_ref[...], preferred_element_type=F32)
    g_blk, r_blk = x_ref.shape[0], x_ref.shape[1]
    y3 = y.reshape(g_blk, r_blk, tn)
    col = pl.multiple_of(j * tn, tn)
    o_ref[:, :, pl.ds(col, tn)] = x_ref[...] + gate_ref[...] * y3

    @pl.when(j == pl.num_programs(2) - 1)
    def _():
        xn = o_ref[...]
        ms = jnp.mean(xn * xn, axis=-1, keepdims=True)
        o_ref[...] = xn * lax.rsqrt(ms + EPS) * fg_ref[...]


def _out_proj(merged, w_o, x3, gate3, fin_gain, g_blk, r_blk, tn):
    n, l, d = x3.shape
    rows = g_blk * r_blk
    lb = l // r_blk
    return pl.pallas_call(
        functools.partial(_out_kernel, tn),
        out_shape=jax.ShapeDtypeStruct((n, l, d), F32),
        grid=(n // g_blk, lb, d // tn),
        in_specs=[
            pl.BlockSpec((rows, d), lambda i, r, j: (i * lb + r, 0)),
            pl.BlockSpec((d, tn), lambda i, r, j: (0, j)),
            pl.BlockSpec((g_blk, r_blk, tn), lambda i, r, j: (i, r, j)),
            pl.BlockSpec((g_blk, 1, tn), lambda i, r, j: (i, 0, j)),
            pl.BlockSpec((1, 1, d), lambda i, r, j: (0, 0, 0)),
        ],
        out_specs=pl.BlockSpec((g_blk, r_blk, d), lambda i, r, j: (i, r, 0)),
        compiler_params=_cparams(("arbitrary", "arbitrary", "arbitrary")),
        name="out_proj",
    )(merged, w_o, x3, gate3, fin_gain)


def kernel(x_prompt, x_sample, cache_k_win, cache_v_win, state_gla, c_prompt, c_sample,
           w_ada, b_ada, norm_gain, w_in, attn_sink, w_alpha2, b_alpha, gla_norm_gain,
           w_proj_a, w_proj_b, w_out, final_norm_gain):
    assert w_ada.shape[0] == 1, "single-layer step"
    bp, lp, d = x_prompt.shape
    ns, ts, _ = x_sample.shape
    w_len = cache_k_win.shape[2]
    rows_p = bp * lp
    rows_s = ns * ts

    wi = w_in[0]
    w_g = wi[:, SRC_GATES:].astype(BF16)
    w_r = jnp.pad(wi[:, SRC_MAIN:SRC_GATES], ((0, 0), (0, LANE - GATE_RANK))).astype(BF16)
    w2 = jnp.pad(w_alpha2[0], ((0, LANE - GATE_RANK), (0, 0))).astype(BF16)
    b2 = b_alpha[0].reshape(1, KEY_B)
    gla_gain = gla_norm_gain[0].reshape(1, WIDTH_B)
    w_pa = w_proj_a[0].astype(BF16)
    w_pb = w_proj_b[0].astype(BF16)
    w_o = w_out[0].astype(BF16)
    gain3 = norm_gain[0].reshape(1, 1, d)
    fin3 = final_norm_gain.reshape(1, 1, d)
    sink = attn_sink[0]

    m_all = ns + bp
    m_pad = -(-m_all // 8) * 8
    c_all = jnp.concatenate([c_sample, c_prompt, jnp.zeros((m_pad - m_all, d), F32)], axis=0)
    mod = _adaln(c_all, w_ada[0], b_ada[0].reshape(1, 3 * d))
    shift, scale, gate = mod[:, :d], mod[:, d:2 * d], mod[:, 2 * d:]
    sl_s, sl_p = slice(0, ns), slice(ns, ns + bp)
    as3 = lambda a, n: a.reshape(n, 1, d)

    h, r = _prologue(x_prompt, x_sample, gain3, as3(scale[sl_p], bp), as3(shift[sl_p], bp),
                     as3(scale[sl_s], ns), as3(shift[sl_s], ns), w_r, 512)
    rows = rows_p + rows_s
    tm = rows // 4
    p = _matmul(h, wi, PROJ_DEST, tm, PROJ_TN, F32, "in_proj")
    gates = _matmul(h, w_g, tuple(range(2 * d // PROJ_TN)), tm, PROJ_TN, F32, "gate_proj")
    p3 = p.reshape(rows // ts, ts, p.shape[1])
    r3 = r.reshape(rows // ts, ts, LANE)
    seq0 = rows_p // ts

    ua_p = _swa_prompt(sink, p, bp, lp)
    ub_p, s_p = _gla_prompt(p, r, w2, b2, gla_gain, bp, lp)
    merged_p = _merge(ua_p, ub_p, w_pa, w_pb, gates, 0, 1024, 512)
    y_p = _out_proj(merged_p, w_o, x_prompt, as3(gate[sl_p], bp), fin3, 1, 512, 512)
    pp3 = p[:rows_p].reshape(bp, lp, p.shape[1])
    k_win_p = pp3[:, lp - WINDOW:, OFF_KA:OFF_KA + KV_W].reshape(1, bp, WINDOW, N_KV, HEAD_DIM)
    v_win_p = pp3[:, lp - WINDOW:, OFF_VA:OFF_VA + KV_W].reshape(1, bp, WINDOW, N_KV, HEAD_DIM)

    ua_s, k_win_s, v_win_s = _swa_sample(sink, p3, seq0, ns, cache_k_win[0].reshape(ns, w_len, KV_W),
                                         cache_v_win[0].reshape(ns, w_len, KV_W))
    ub_s, s_s = _gla_sample(p3, r3, seq0, w2, b2, gla_gain, state_gla[0])
    merged_s = _merge(ua_s, ub_s, w_pa, w_pb, gates, rows_p, rows_s, 512)
    y_s = _out_proj(merged_s, w_o, x_sample, as3(gate[sl_s], ns), fin3, 64, ts, 512)

    return (y_p, y_s, k_win_p, v_win_p, s_p[None],
            k_win_s.reshape(1, ns, w_len, N_KV, HEAD_DIM), v_win_s.reshape(1, ns, w_len, N_KV, HEAD_DIM),
            s_s[None])
```

```python
import functools

import jax
import jax.numpy as jnp
from jax import lax
from jax.experimental import pallas as pl
from jax.experimental.pallas import tpu as pltpu

F32 = jnp.float32
BF16 = jnp.bfloat16

D_MODEL = 4096
WIDTH_A = 2048
HEAD_DIM = 64
N_HEADS_A = 32
N_KV = 4
GROUP = 8
WINDOW = 128
ATTN_BLOCK = 128
WIDTH_B = 2048
N_HEADS_B = 4
DV_B = 512
KEY_B = 1024
DK_B = 256
GATE_RANK = 16
GLA_TAU = 16.0
GLA_CHUNK = 64
EPS = 1e-6
NEG_INF = -1e30
KV_W = N_KV * HEAD_DIM

V7X_VMEM_BYTES = 64 * 1024 * 1024
VMEM_LIMIT = V7X_VMEM_BYTES - 8 * 1024 * 1024
VMEM_LIMIT_RESIDENT = V7X_VMEM_BYTES - 4 * 1024 * 1024
LANE = 128
BF16_ROWS = 16

SRC_KA = WIDTH_A
SRC_MAIN = WIDTH_A + 2 * KV_W + WIDTH_A + 2 * KEY_B + 2 * WIDTH_B
SRC_GATES = SRC_MAIN + GATE_RANK
PROJ_TN = 512
OFF_GA = 0
OFF_GB = 4096
OFF_QA = 8192
OFF_ZA = 10240
OFF_VB = 12288
OFF_ZB = 14336
OFF_QB = 16384
OFF_KB = 17408
OFF_KA = 18432
OFF_VA = 18688
PROJ_COLS = 18944
PROJ_DEST = tuple(16 + t for t in (0, 1, 2, 3, 20, 4, 5, 6, 7, 16, 17, 18, 19, 8, 9, 10, 11, 12, 13, 14, 15)) \
    + tuple(range(16))
PROJ_SRC = tuple(j * PROJ_TN for j in range(21)) + tuple(SRC_GATES + j * PROJ_TN for j in range(16))

NT_DIMS = (((1,), (1,)), ((), ()))
TN_DIMS = (((0,), (0,)), ((), ()))


def _cparams(sem, vmem_limit=VMEM_LIMIT):
    return pltpu.CompilerParams(dimension_semantics=sem, vmem_limit_bytes=vmem_limit)


def _silu(x):
    return x * jax.nn.sigmoid(x)


def _adaln_kernel(c_ref, w_ref, b_ref, o_ref):
    a = _silu(c_ref[...]).astype(BF16)
    w = w_ref[...].astype(BF16)
    o_ref[...] = jnp.dot(a, w, preferred_element_type=F32) + b_ref[...]


def _adaln(c_all, w_ada, b_ada, tn=512):
    m, k = c_all.shape
    n = w_ada.shape[1]
    return pl.pallas_call(
        _adaln_kernel,
        out_shape=jax.ShapeDtypeStruct((m, n), F32),
        grid=(n // tn,),
        in_specs=[
            pl.BlockSpec((m, k), lambda j: (0, 0)),
            pl.BlockSpec((k, tn), lambda j: (0, j)),
            pl.BlockSpec((1, tn), lambda j: (0, j)),
        ],
        out_specs=pl.BlockSpec((m, tn), lambda j: (0, j)),
        compiler_params=_cparams(("arbitrary",)),
        name="adaln",
    )(c_all, w_ada, b_ada)


def _prologue_kernel(n_ptiles, xp_ref, xs_ref, g_ref, scp_ref, shp_ref, scs_ref, shs_ref, wr_ref,
                     h_ref, r_ref):
    i = pl.program_id(0)

    def emit(x, scale, shift):
        ms = jnp.mean(x * x, axis=-1, keepdims=True)
        y = x * lax.rsqrt(ms + EPS) * g_ref[...]
        h = y * (1.0 + scale) + shift
        hb = h.reshape(h.shape[0] * h.shape[1], h.shape[2]).astype(BF16)
        h_ref[...] = hb
        r_ref[...] = lax.dot_general(hb, wr_ref[...], NT_DIMS, preferred_element_type=F32)

    @pl.when(i < n_ptiles)
    def _():
        emit(xp_ref[...], scp_ref[...], shp_ref[...])

    @pl.when(i >= n_ptiles)
    def _():
        emit(xs_ref[...], scs_ref[...], shs_ref[...])


def _prologue(xp, xs, gain, scp, shp, scs, shs, w_rt, rows):
    b, l, d = xp.shape
    n, t, _ = xs.shape
    lb = l // rows
    n_pt = b * lb
    sg = rows // t
    n_st = n // sg
    pt = lambda i: jnp.minimum(i, n_pt - 1)
    st = lambda i: jnp.maximum(i - n_pt, 0)
    return pl.pallas_call(
        functools.partial(_prologue_kernel, n_pt),
        out_shape=(jax.ShapeDtypeStruct((b * l + n * t, d), BF16),
                   jax.ShapeDtypeStruct((b * l + n * t, LANE), F32)),
        grid=(n_pt + n_st,),
        in_specs=[
            pl.BlockSpec((1, rows, d), lambda i: (pt(i) // lb, pt(i) % lb, 0)),
            pl.BlockSpec((sg, t, d), lambda i: (st(i), 0, 0)),
            pl.BlockSpec((1, 1, d), lambda i: (0, 0, 0)),
            pl.BlockSpec((1, 1, d), lambda i: (pt(i) // lb, 0, 0)),
            pl.BlockSpec((1, 1, d), lambda i: (pt(i) // lb, 0, 0)),
            pl.BlockSpec((sg, 1, d), lambda i: (st(i), 0, 0)),
            pl.BlockSpec((sg, 1, d), lambda i: (st(i), 0, 0)),
            pl.BlockSpec((LANE, d), lambda i: (0, 0)),
        ],
        out_specs=(pl.BlockSpec((rows, d), lambda i: (i, 0)),
                   pl.BlockSpec((rows, LANE), lambda i: (i, 0))),
        compiler_params=_cparams(("arbitrary",)),
        name="prologue",
    )(xp, xs, gain, scp, shp, scs, shs, w_rt)


def _inproj_kernel(dest_ref, src_ref, a_ref, w_ref, o_ref):
    del dest_ref, src_ref
    w = w_ref[...].astype(BF16)
    o_ref[...] = lax.dot_general(a_ref[...], w, NT_DIMS, preferred_element_type=F32).astype(o_ref.dtype)


def _in_proj(a, wt, dest, src, tm, tn, out_dtype):
    m, k = a.shape
    nt = len(dest)
    return pl.pallas_call(
        _inproj_kernel,
        out_shape=jax.ShapeDtypeStruct((m, nt * tn), out_dtype),
        grid_spec=pltpu.PrefetchScalarGridSpec(
            num_scalar_prefetch=2,
            grid=(m // tm, nt),
            in_specs=[
                pl.BlockSpec((tm, k), lambda i, j, d, s: (i, 0), pipeline_mode=pl.Buffered(1)),
                pl.BlockSpec((pl.Element(tn), pl.Element(k)),
                             lambda i, j, d, s: (pl.multiple_of(s[j], GATE_RANK), 0)),
            ],
            out_specs=pl.BlockSpec((tm, tn), lambda i, j, d, s: (i, d[j])),
        ),
        compiler_params=_cparams(("arbitrary", "arbitrary")),
        name="in_proj",
    )(jnp.asarray(dest, jnp.int32), jnp.asarray(src, jnp.int32), a, wt)


def _kv_rows_kernel(a_ref, w_ref, o_ref):
    w = w_ref[...].astype(BF16)
    o_ref[...] = lax.dot_general(a_ref[...], w, NT_DIMS, preferred_element_type=F32)


def _kv_rows(h, wt, b, l, rows_s):
    k = h.shape[1]
    blk = WINDOW
    lb = l // blk
    n_s = rows_s // blk
    row_blk = lambda t: jnp.where(t < b, (t + 1) * lb - 1, b * lb + t - b)
    return pl.pallas_call(
        _kv_rows_kernel,
        out_shape=jax.ShapeDtypeStruct(((b + n_s) * blk, 2 * KV_W), F32),
        grid=(b + n_s,),
        in_specs=[
            pl.BlockSpec((blk, k), lambda t: (row_blk(t), 0)),
            pl.BlockSpec((2 * KV_W, k), lambda t: (SRC_KA // (2 * KV_W), 0)),
        ],
        out_specs=pl.BlockSpec((blk, 2 * KV_W), lambda t: (t, 0)),
        compiler_params=_cparams(("arbitrary",)),
        name="kv_rows",
    )(h, wt)


def _sink_attend(q, k, v, mask, sink):
    s = lax.dot_general(q, k, NT_DIMS, preferred_element_type=F32)
    s = jnp.where(mask, s * (HEAD_DIM ** -0.5), NEG_INF)
    m = jnp.maximum(jnp.max(s, axis=-1, keepdims=True), sink)
    p = jnp.exp(s - m)
    denom = jnp.sum(p, axis=-1, keepdims=True) + jnp.exp(sink - m)
    o = jnp.dot(p.astype(BF16), v, preferred_element_type=F32)
    return o / denom


def _swa_prompt_kernel(sink_ref, q_ref, za_ref, kp_ref, kc_ref, vp_ref, vc_ref, o_ref):
    n = pl.program_id(1)
    q = q_ref[...]
    k = jnp.concatenate([kp_ref[...], kc_ref[...]], axis=0)
    v = jnp.concatenate([vp_ref[...], vc_ref[...]], axis=0)
    i = lax.broadcasted_iota(jnp.int32, (ATTN_BLOCK, 2 * ATTN_BLOCK), 0)
    j = lax.broadcasted_iota(jnp.int32, (ATTN_BLOCK, 2 * ATTN_BLOCK), 1)
    mask = (j >= i) & (j <= i + WINDOW) & ((j >= ATTN_BLOCK) | (n > 0))
    outs = []
    for h in range(N_KV):
        kh = k[:, h * HEAD_DIM:(h + 1) * HEAD_DIM]
        vh = v[:, h * HEAD_DIM:(h + 1) * HEAD_DIM]
        for g in range(GROUP):
            hd = h * GROUP + g
            qg = q[:, hd * HEAD_DIM:(hd + 1) * HEAD_DIM]
            outs.append(_sink_attend(qg, kh, vh, mask, sink_ref[hd]))
    o = jnp.concatenate(outs, axis=1)
    o_ref[...] = (o * _silu(za_ref[...].astype(F32))).astype(BF16)


def _swa_prompt(sink, p, b, l):
    nb = l // ATTN_BLOCK
    kcol = OFF_KA // KV_W
    vcol = OFF_VA // KV_W
    blk_q = (ATTN_BLOCK, WIDTH_A)
    blk_kv = (ATTN_BLOCK, KV_W)
    cur = lambda bi, n: bi * nb + n
    prev = lambda bi, n: bi * nb + jnp.maximum(n - 1, 0)
    return pl.pallas_call(
        _swa_prompt_kernel,
        out_shape=jax.ShapeDtypeStruct((b * l, WIDTH_A), BF16),
        grid=(b, nb),
        in_specs=[
            pl.BlockSpec(memory_space=pltpu.SMEM),
            pl.BlockSpec(blk_q, lambda bi, n: (cur(bi, n), OFF_QA // WIDTH_A)),
            pl.BlockSpec(blk_q, lambda bi, n: (cur(bi, n), OFF_ZA // WIDTH_A)),
            pl.BlockSpec(blk_kv, lambda bi, n: (prev(bi, n), kcol)),
            pl.BlockSpec(blk_kv, lambda bi, n: (cur(bi, n), kcol)),
            pl.BlockSpec(blk_kv, lambda bi, n: (prev(bi, n), vcol)),
            pl.BlockSpec(blk_kv, lambda bi, n: (cur(bi, n), vcol)),
        ],
        out_specs=pl.BlockSpec(blk_q, lambda bi, n: (cur(bi, n), 0)),
        compiler_params=_cparams(("arbitrary", "arbitrary")),
        name="swa_prompt",
    )(sink, p, p, p, p, p, p)


def _swa_sample_kernel(g_blk, t_len, w_len, sink_ref, q_ref, za_ref, kv_ref, kc_ref, vc_ref,
                       o_ref, kw_ref, vw_ref):
    kpad = 2 * WINDOW - w_len - t_len
    rows = GROUP * t_len
    r = lax.broadcasted_iota(jnp.int32, (rows, 2 * WINDOW), 0)
    j = lax.broadcasted_iota(jnp.int32, (rows, 2 * WINDOW), 1)
    t = r % t_len
    mask = (j <= w_len + t) & (j >= w_len + t - WINDOW) & (j < w_len + t_len)
    rg = lax.broadcasted_iota(jnp.int32, (rows, 1), 0) // t_len
    sink_cols = []
    for h in range(N_KV):
        col = jnp.zeros((rows, 1), F32)
        for g in range(GROUP):
            col = jnp.where(rg == g, sink_ref[h * GROUP + g], col)
        sink_cols.append(col)
    zpad = jnp.zeros((kpad, KV_W), F32)
    q_all = q_ref[...].astype(F32)
    za_all = za_ref[...].astype(F32)
    seq_outs = []
    for s in range(g_blk):
        kc = kc_ref[s]
        vc = vc_ref[s]
        kn = kv_ref[s, :, :KV_W]
        vn = kv_ref[s, :, KV_W:]
        kw_ref[s] = jnp.concatenate([kc[t_len:], kn], axis=0)
        vw_ref[s] = jnp.concatenate([vc[t_len:], vn], axis=0)
        k_all = jnp.concatenate([kc, kn, zpad], axis=0).astype(BF16)
        v_all = jnp.concatenate([vc, vn, zpad], axis=0).astype(BF16)
        q = q_all[s * t_len:(s + 1) * t_len]
        pieces = []
        for h in range(N_KV):
            kh = k_all[:, h * HEAD_DIM:(h + 1) * HEAD_DIM]
            vh = v_all[:, h * HEAD_DIM:(h + 1) * HEAD_DIM]
            qh = jnp.concatenate(
                [q[:, (h * GROUP + g) * HEAD_DIM:(h * GROUP + g + 1) * HEAD_DIM] for g in range(GROUP)],
                axis=0).astype(BF16)
            oh = _sink_attend(qh, kh, vh, mask, sink_cols[h])
            for g in range(GROUP):
                pieces.append(oh[g * t_len:(g + 1) * t_len])
        o = jnp.concatenate(pieces, axis=1)
        seq_outs.append(o * _silu(za_all[s * t_len:(s + 1) * t_len]))
    o_ref[...] = jnp.concatenate(seq_outs, axis=0).astype(BF16)


def _swa_sample(sink, p, row0, n, t_len, kv3, kv_seq0, cache_k, cache_v):
    g_blk = BF16_ROWS // t_len
    rows = g_blk * t_len
    w_len = cache_k.shape[1]
    blk_q = (rows, WIDTH_A)
    blk_c = (g_blk, w_len, KV_W)
    r0 = row0 // rows
    s0 = kv_seq0 // g_blk
    return pl.pallas_call(
        functools.partial(_swa_sample_kernel, g_blk, t_len, w_len),
        out_shape=(jax.ShapeDtypeStruct((n * t_len, WIDTH_A), BF16),
                   jax.ShapeDtypeStruct((n, w_len, KV_W), F32),
                   jax.ShapeDtypeStruct((n, w_len, KV_W), F32)),
        grid=(n // g_blk,),
        in_specs=[
            pl.BlockSpec(memory_space=pltpu.SMEM),
            pl.BlockSpec(blk_q, lambda i: (r0 + i, OFF_QA // WIDTH_A)),
            pl.BlockSpec(blk_q, lambda i: (r0 + i, OFF_ZA // WIDTH_A)),
            pl.BlockSpec((g_blk, t_len, 2 * KV_W), lambda i: (s0 + i, 0, 0)),
            pl.BlockSpec(blk_c, lambda i: (i, 0, 0)),
            pl.BlockSpec(blk_c, lambda i: (i, 0, 0)),
        ],
        out_specs=(pl.BlockSpec(blk_q, lambda i: (i, 0)),
                   pl.BlockSpec(blk_c, lambda i: (i, 0, 0)),
                   pl.BlockSpec(blk_c, lambda i: (i, 0, 0))),
        compiler_params=_cparams(("arbitrary",)),
        name="swa_sample",
    )(sink, p, p, kv3, cache_k, cache_v)


def _split_hi_lo(x):
    hi = x.astype(BF16)
    lo = (x - hi.astype(F32)).astype(BF16)
    return hi, lo


def _gla_chunk(q, k, v, la, s_old, tri, ones_cl, causal):
    la_hi, la_lo = _split_hi_lo(la)
    b = (jnp.dot(tri, la_hi, preferred_element_type=F32)
         + jnp.dot(tri, la_lo, preferred_element_type=F32))
    c = b.shape[0]
    b_last = b[c - 1:c, :]
    bl_col = (lax.dot_general(la_hi, ones_cl, TN_DIMS, preferred_element_type=F32)
              + lax.dot_general(la_lo, ones_cl, TN_DIMS, preferred_element_type=F32))
    q_t = (q * jnp.exp(b) * (DK_B ** -0.5)).astype(BF16)
    k_t = (k * jnp.exp(-b)).astype(BF16)
    k_d = (k * jnp.exp(b_last - b)).astype(BF16)
    att = lax.dot_general(q_t, k_t, NT_DIMS, preferred_element_type=F32)
    att = jnp.where(causal, att, 0.0).astype(BF16)
    o = (jnp.dot(att, v, preferred_element_type=F32)
         + jnp.dot(q_t, s_old.astype(BF16), preferred_element_type=F32))
    decay = jnp.exp(bl_col)
    decay = jnp.concatenate([decay] * (DV_B // LANE), axis=1)
    s_new = decay * s_old + lax.dot_general(k_d, v, TN_DIMS, preferred_element_type=F32)
    return o, s_new


def _log_decay(r, w2, b2):
    z = jnp.dot(r.astype(BF16), w2, preferred_element_type=F32) + b2
    ls = jnp.minimum(z, 0.0) - jnp.log(1.0 + jnp.exp(-jnp.abs(z)))
    return ls / GLA_TAU


def _gla_finish(o, zb, gain):
    on = o * lax.rsqrt(jnp.mean(o * o, axis=-1, keepdims=True) + EPS) * gain
    return on * _silu(zb)


def _tri_consts(c):
    ri = lax.broadcasted_iota(jnp.int32, (c, c), 0)
    ci = lax.broadcasted_iota(jnp.int32, (c, c), 1)
    causal = ri >= ci
    tri = jnp.where(causal, 1.0, 0.0).astype(BF16)
    ones_cl = jnp.ones((c, LANE), BF16)
    return tri, ones_cl, causal


def _gla_prompt_kernel(q_ref, k_ref, v_ref, zb_ref, r_ref, w2_ref, b2_ref, gain_ref, o_ref, s_ref):
    cidx = pl.program_id(1)

    @pl.when(cidx == 0)
    def _():
        s_ref[...] = jnp.zeros_like(s_ref)

    c = q_ref.shape[0]
    tri, ones_cl, causal = _tri_consts(c)
    la_all = _log_decay(r_ref[...], w2_ref[...], b2_ref[...])
    outs = []
    for h in range(N_HEADS_B):
        dk = slice(h * DK_B, (h + 1) * DK_B)
        dv = slice(h * DV_B, (h + 1) * DV_B)
        o, s_new = _gla_chunk(q_ref[:, dk].astype(F32), k_ref[:, dk].astype(F32), v_ref[:, dv],
                              la_all[:, dk], s_ref[0, h], tri, ones_cl, causal)
        s_ref[0, h] = s_new
        outs.append(_gla_finish(o, zb_ref[:, dv].astype(F32), gain_ref[:, dv]))
    o_ref[...] = jnp.concatenate(outs, axis=1).astype(BF16)


def _gla_prompt(p, r, w2, b2, gain, b, l):
    c = GLA_CHUNK
    nc = l // c
    row = lambda bi, ci: bi * nc + ci
    return pl.pallas_call(
        _gla_prompt_kernel,
        out_shape=(jax.ShapeDtypeStruct((b * l, WIDTH_B), BF16),
                   jax.ShapeDtypeStruct((b, N_HEADS_B, DK_B, DV_B), F32)),
        grid=(b, nc),
        in_specs=[
            pl.BlockSpec((c, KEY_B), lambda bi, ci: (row(bi, ci), OFF_QB // KEY_B)),
            pl.BlockSpec((c, KEY_B), lambda bi, ci: (row(bi, ci), OFF_KB // KEY_B)),
            pl.BlockSpec((c, WIDTH_B), lambda bi, ci: (row(bi, ci), OFF_VB // WIDTH_B)),
            pl.BlockSpec((c, WIDTH_B), lambda bi, ci: (row(bi, ci), OFF_ZB // WIDTH_B)),
            pl.BlockSpec((c, LANE), lambda bi, ci: (row(bi, ci), 0)),
            pl.BlockSpec((LANE, KEY_B), lambda bi, ci: (0, 0)),
            pl.BlockSpec((1, KEY_B), lambda bi, ci: (0, 0)),
            pl.BlockSpec((1, WIDTH_B), lambda bi, ci: (0, 0)),
        ],
        out_specs=(pl.BlockSpec((c, WIDTH_B), lambda bi, ci: (row(bi, ci), 0)),
                   pl.BlockSpec((1, N_HEADS_B, DK_B, DV_B), lambda bi, ci: (bi, 0, 0, 0))),
        compiler_params=_cparams(("arbitrary", "arbitrary")),
        name="gla_prompt",
    )(p, p, p, p, r, w2, b2, gain)


def _gla_sample_kernel(g_blk, t_len, q_ref, k_ref, v_ref, zb_ref, r_ref, w2_ref, b2_ref, gain_ref, s0_ref,
                       o_ref, s_ref):
    c = 2 * t_len
    tri, ones_cl, causal = _tri_consts(c)

    def pad(x):
        return jnp.concatenate([x, jnp.zeros_like(x)], axis=0)

    live = lax.broadcasted_iota(jnp.int32, (c, 1), 0) < t_len
    q_all = q_ref[...].astype(F32)
    k_all = k_ref[...].astype(F32)
    v_all = v_ref[...].astype(F32)
    zb_all = zb_ref[...].astype(F32)
    seq_outs = []
    for s in range(g_blk):
        rs = slice(s * t_len, (s + 1) * t_len)
        la_all = jnp.where(live, _log_decay(pad(r_ref[s]), w2_ref[...], b2_ref[...]), 0.0)
        outs = []
        for h in range(N_HEADS_B):
            dk = slice(h * DK_B, (h + 1) * DK_B)
            dv = slice(h * DV_B, (h + 1) * DV_B)
            o, s_new = _gla_chunk(pad(q_all[rs, dk]), pad(k_all[rs, dk]), pad(v_all[rs, dv]).astype(BF16),
                                  la_all[:, dk], s0_ref[s, h], tri, ones_cl, causal)
            s_ref[s, h] = s_new
            outs.append(_gla_finish(o[:t_len], zb_all[rs, dv], gain_ref[:, dv]))
        seq_outs.append(jnp.concatenate(outs, axis=1))
    o_ref[...] = jnp.concatenate(seq_outs, axis=0).astype(BF16)


def _gla_sample(p, r3, row0, t_len, w2, b2, gain, state):
    n = state.shape[0]
    g_blk = BF16_ROWS // t_len
    rows = g_blk * t_len
    st_blk = (g_blk, N_HEADS_B, DK_B, DV_B)
    r0 = row0 // rows
    return pl.pallas_call(
        functools.partial(_gla_sample_kernel, g_blk, t_len),
        out_shape=(jax.ShapeDtypeStruct((n * t_len, WIDTH_B), BF16),
                   jax.ShapeDtypeStruct(state.shape, F32)),
        grid=(n // g_blk,),
        in_specs=[
            pl.BlockSpec((rows, KEY_B), lambda i: (r0 + i, OFF_QB // KEY_B)),
            pl.BlockSpec((rows, KEY_B), lambda i: (r0 + i, OFF_KB // KEY_B)),
            pl.BlockSpec((rows, WIDTH_B), lambda i: (r0 + i, OFF_VB // WIDTH_B)),
            pl.BlockSpec((rows, WIDTH_B), lambda i: (r0 + i, OFF_ZB // WIDTH_B)),
            pl.BlockSpec((g_blk, t_len, LANE), lambda i: (r0 + i, 0, 0)),
            pl.BlockSpec((LANE, KEY_B), lambda i: (0, 0)),
            pl.BlockSpec((1, KEY_B), lambda i: (0, 0)),
            pl.BlockSpec((1, WIDTH_B), lambda i: (0, 0)),
            pl.BlockSpec(st_blk, lambda i: (i, 0, 0, 0)),
        ],
        out_specs=(pl.BlockSpec((rows, WIDTH_B), lambda i: (i, 0)),
                   pl.BlockSpec(st_blk, lambda i: (i, 0, 0, 0))),
        compiler_params=_cparams(("arbitrary",)),
        name="gla_sample",
    )(p, p, p, p, r3, w2, b2, gain, state)


def _merge_kernel(nchunk, a_ref, b_ref, wa_ref, wb_ref, ga_ref, gb_ref, o_ref):
    cw = o_ref.shape[1] // nchunk
    for c in range(nchunk):
        cs = slice(c * cw, (c + 1) * cw)
        ua = jnp.dot(a_ref[...], wa_ref[:, cs], preferred_element_type=F32)
        ub = jnp.dot(b_ref[...], wb_ref[:, cs], preferred_element_type=F32)
        merged = (jax.nn.sigmoid(ga_ref[:, cs].astype(F32)) * ua
                  + jax.nn.sigmoid(gb_ref[:, cs].astype(F32)) * ub)
        o_ref[:, cs] = merged.astype(BF16)


def _merge(a_in, b_in, w_pa, w_pb, p, row0, tm):
    m = a_in.shape[0]
    r0 = row0 // tm
    d = D_MODEL
    resident = dict(pipeline_mode=pl.Buffered(1))
    return pl.pallas_call(
        functools.partial(_merge_kernel, 4),
        out_shape=jax.ShapeDtypeStruct((m, d), BF16),
        grid=(m // tm,),
        in_specs=[
            pl.BlockSpec((tm, WIDTH_A), lambda i: (i, 0)),
            pl.BlockSpec((tm, WIDTH_B), lambda i: (i, 0)),
            pl.BlockSpec((WIDTH_A, d), lambda i: (0, 0), **resident),
            pl.BlockSpec((WIDTH_B, d), lambda i: (0, 0), **resident),
            pl.BlockSpec((tm, d), lambda i: (r0 + i, OFF_GA // d)),
            pl.BlockSpec((tm, d), lambda i: (r0 + i, OFF_GB // d)),
        ],
        out_specs=pl.BlockSpec((tm, d), lambda i: (i, 0)),
        compiler_params=_cparams(("arbitrary",), VMEM_LIMIT_RESIDENT),
        name="merge",
    )(a_in, b_in, w_pa, w_pb, p, p)


def _out_kernel(nchunk, m_ref, w_ref, x_ref, gate_ref, fg_ref, o_ref):
    g_blk, r_blk, d = x_ref.shape
    cw = d // nchunk
    ssq = jnp.zeros((g_blk, r_blk, 1), F32)
    for c in range(nchunk):
        cs = slice(c * cw, (c + 1) * cw)
        y = jnp.dot(m_ref[...], w_ref[:, cs], preferred_element_type=F32)
        xn = x_ref[:, :, cs] + gate_ref[:, :, cs] * y.reshape(g_blk, r_blk, cw)
        o_ref[:, :, cs] = xn
        ssq = ssq + jnp.sum(xn * xn, axis=-1, keepdims=True)
    inv = lax.rsqrt(ssq * (1.0 / d) + EPS)
    for c in range(nchunk):
        cs = slice(c * cw, (c + 1) * cw)
        o_ref[:, :, cs] = o_ref[:, :, cs] * inv * fg_ref[:, :, cs]


def _out_proj(merged, w_o, x3, gate3, fin_gain, g_blk, r_blk):
    n, l, d = x3.shape
    rows = g_blk * r_blk
    lb = l // r_blk
    return pl.pallas_call(
        functools.partial(_out_kernel, 4),
        out_shape=jax.ShapeDtypeStruct((n, l, d), F32),
        grid=(n // g_blk, lb),
        in_specs=[
            pl.BlockSpec((rows, d), lambda i, r: (i * lb + r, 0)),
            pl.BlockSpec((d, d), lambda i, r: (0, 0), pipeline_mode=pl.Buffered(1)),
            pl.BlockSpec((g_blk, r_blk, d), lambda i, r: (i, r, 0)),
            pl.BlockSpec((g_blk, 1, d), lambda i, r: (i, 0, 0)),
            pl.BlockSpec((1, 1, d), lambda i, r: (0, 0, 0)),
        ],
        out_specs=pl.BlockSpec((g_blk, r_blk, d), lambda i, r: (i, r, 0)),
        compiler_params=_cparams(("arbitrary", "arbitrary"), VMEM_LIMIT_RESIDENT),
        name="out_proj",
    )(merged, w_o, x3, gate3, fin_gain)


def kernel(x_prompt, x_sample, cache_k_win, cache_v_win, state_gla, c_prompt, c_sample,
           w_ada, b_ada, norm_gain, w_in, attn_sink, w_alpha2, b_alpha, gla_norm_gain,
           w_proj_a, w_proj_b, w_out, final_norm_gain):
    assert w_ada.shape[0] == 1, "single-layer step"
    bp, lp, d = x_prompt.shape
    ns, ts, _ = x_sample.shape
    w_len = cache_k_win.shape[2]
    rows_p = bp * lp
    rows_s = ns * ts
    rows = rows_p + rows_s

    wt = jnp.transpose(w_in[0])
    w_rt = jnp.pad(wt[SRC_MAIN:SRC_GATES], ((0, LANE - GATE_RANK), (0, 0))).astype(BF16)
    w2 = jnp.pad(w_alpha2[0], ((0, LANE - GATE_RANK), (0, 0))).astype(BF16)
    b2 = b_alpha[0].reshape(1, KEY_B)
    gla_gain = gla_norm_gain[0].reshape(1, WIDTH_B)
    w_pa = w_proj_a[0].astype(BF16)
    w_pb = w_proj_b[0].astype(BF16)
    w_o = w_out[0].astype(BF16)
    gain3 = norm_gain[0].reshape(1, 1, d)
    fin3 = final_norm_gain.reshape(1, 1, d)
    sink = attn_sink[0]

    m_all = ns + bp
    m_pad = -(-m_all // 8) * 8
    c_all = jnp.concatenate([c_sample, c_prompt, jnp.zeros((m_pad - m_all, d), F32)], axis=0)
    mod = _adaln(c_all, w_ada[0], b_ada[0].reshape(1, 3 * d))
    shift, scale, gate = mod[:, :d], mod[:, d:2 * d], mod[:, 2 * d:]
    sl_s, sl_p = slice(0, ns), slice(ns, ns + bp)
    as3 = lambda a, n: a.reshape(n, 1, d)

    h, r = _prologue(x_prompt, x_sample, gain3, as3(scale[sl_p], bp), as3(shift[sl_p], bp),
                     as3(scale[sl_s], ns), as3(shift[sl_s], ns), w_rt, 512)
    p = _in_proj(h, wt, PROJ_DEST, PROJ_SRC, rows // 4, PROJ_TN, BF16)
    kv = _kv_rows(h, wt, bp, lp, rows_s)
    kv_p = kv[:bp * WINDOW].reshape(bp, WINDOW, 2, N_KV, HEAD_DIM)
    k_win_p = kv_p[:, :, 0][None]
    v_win_p = kv_p[:, :, 1][None]
    kv3 = kv.reshape(kv.shape[0] // ts, ts, 2 * KV_W)
    r3 = r.reshape(rows // ts, ts, LANE)

    ua_p = _swa_prompt(sink, p, bp, lp)
    ub_p, s_p = _gla_prompt(p, r, w2, b2, gla_gain, bp, lp)
    merged_p = _merge(ua_p, ub_p, w_pa, w_pb, p, 0, 256)
    y_p = _out_proj(merged_p, w_o, x_prompt, as3(gate[sl_p], bp), fin3, 1, 256)

    ua_s, k_win_s, v_win_s = _swa_sample(sink, p, rows_p, ns, ts, kv3, bp * WINDOW // ts,
                                         cache_k_win[0].reshape(ns, w_len, KV_W),
                                         cache_v_win[0].reshape(ns, w_len, KV_W))
    ub_s, s_s = _gla_sample(p, r3, rows_p, ts, w2, b2, gla_gain, state_gla[0])
    merged_s = _merge(ua_s, ub_s, w_pa, w_pb, p, rows_p, 256)
    y_s = _out_proj(merged_s, w_o, x_sample, as3(gate[sl_s], ns), fin3, 256 // ts, ts)

    return (y_p, y_s, k_win_p, v_win_p, s_p[None],
            k_win_s.reshape(1, ns, w_len, N_KV, HEAD_DIM), v_win_s.reshape(1, ns, w_len, N_KV, HEAD_DIM),
            s_s[None])
```

```python
import functools

import jax
import jax.numpy as jnp
from jax import lax
from jax.experimental import pallas as pl
from jax.experimental.pallas import tpu as pltpu

F32 = jnp.float32
BF16 = jnp.bfloat16

D_MODEL = 4096
WIDTH_A = 2048
HEAD_DIM = 64
N_HEADS_A = 32
N_KV = 4
GROUP = 8
WINDOW = 128
ATTN_BLOCK = 128
WIDTH_B = 2048
N_HEADS_B = 4
DV_B = 512
KEY_B = 1024
DK_B = 256
GATE_RANK = 16
GLA_TAU = 16.0
GLA_CHUNK = 64
EPS = 1e-6
NEG_INF = -1e30
KV_W = N_KV * HEAD_DIM

V7X_VMEM_BYTES = 64 * 1024 * 1024
VMEM_LIMIT = V7X_VMEM_BYTES - 8 * 1024 * 1024
VMEM_LIMIT_RESIDENT = V7X_VMEM_BYTES - 4 * 1024 * 1024
LANE = 128
BF16_ROWS = 16

SRC_KA = WIDTH_A
SRC_MAIN = WIDTH_A + 2 * KV_W + WIDTH_A + 2 * KEY_B + 2 * WIDTH_B
SRC_GATES = SRC_MAIN + GATE_RANK
PROJ_TN = 512
OFF_GA = 0
OFF_GB = 4096
OFF_QA = 8192
OFF_ZA = 10240
OFF_VB = 12288
OFF_ZB = 14336
OFF_QB = 16384
OFF_KB = 17408
OFF_KA = 18432
OFF_VA = 18688
PROJ_COLS = 18944
PROJ_DEST = tuple(16 + t for t in (0, 1, 2, 3, 20, 4, 5, 6, 7, 16, 17, 18, 19, 8, 9, 10, 11, 12, 13, 14, 15)) \
    + tuple(range(16))
PROJ_SRC = tuple(j * PROJ_TN for j in range(21)) + tuple(SRC_GATES + j * PROJ_TN for j in range(16))

NT_DIMS = (((1,), (1,)), ((), ()))
TN_DIMS = (((0,), (0,)), ((), ()))


def _cparams(sem, vmem_limit=VMEM_LIMIT):
    return pltpu.CompilerParams(dimension_semantics=sem, vmem_limit_bytes=vmem_limit)


def _silu(x):
    return x * jax.nn.sigmoid(x)


def _adaln_kernel(c_ref, w_ref, b_ref, o_ref):
    a = _silu(c_ref[...]).astype(BF16)
    w = w_ref[...].astype(BF16)
    o_ref[...] = jnp.dot(a, w, preferred_element_type=F32) + b_ref[...]


def _adaln(c_all, w_ada, b_ada, tn=512):
    m, k = c_all.shape
    n = w_ada.shape[1]
    return pl.pallas_call(
        _adaln_kernel,
        out_shape=jax.ShapeDtypeStruct((m, n), F32),
        grid=(n // tn,),
        in_specs=[
            pl.BlockSpec((m, k), lambda j: (0, 0)),
            pl.BlockSpec((k, tn), lambda j: (0, j)),
            pl.BlockSpec((1, tn), lambda j: (0, j)),
        ],
        out_specs=pl.BlockSpec((m, tn), lambda j: (0, j)),
        compiler_params=_cparams(("arbitrary",)),
        name="adaln",
    )(c_all, w_ada, b_ada)


def _prologue_kernel(n_ptiles, xp_ref, xs_ref, g_ref, scp_ref, shp_ref, scs_ref, shs_ref, wr_ref,
                     h_ref, r_ref):
    i = pl.program_id(0)

    def emit(x, scale, shift):
        ms = jnp.mean(x * x, axis=-1, keepdims=True)
        y = x * lax.rsqrt(ms + EPS) * g_ref[...]
        h = y * (1.0 + scale) + shift
        hb = h.reshape(h.shape[0] * h.shape[1], h.shape[2]).astype(BF16)
        h_ref[...] = hb
        r_ref[...] = lax.dot_general(hb, wr_ref[...], NT_DIMS, preferred_element_type=F32)

    @pl.when(i < n_ptiles)
    def _():
        emit(xp_ref[...], scp_ref[...], shp_ref[...])

    @pl.when(i >= n_ptiles)
    def _():
        emit(xs_ref[...], scs_ref[...], shs_ref[...])


def _prologue(xp, xs, gain, scp, shp, scs, shs, w_rt, rows):
    b, l, d = xp.shape
    n, t, _ = xs.shape
    lb = l // rows
    n_pt = b * lb
    sg = rows // t
    n_st = n // sg
    pt = lambda i: jnp.minimum(i, n_pt - 1)
    st = lambda i: jnp.maximum(i - n_pt, 0)
    return pl.pallas_call(
        functools.partial(_prologue_kernel, n_pt),
        out_shape=(jax.ShapeDtypeStruct((b * l + n * t, d), BF16),
                   jax.ShapeDtypeStruct((b * l + n * t, LANE), F32)),
        grid=(n_pt + n_st,),
        in_specs=[
            pl.BlockSpec((1, rows, d), lambda i: (pt(i) // lb, pt(i) % lb, 0)),
            pl.BlockSpec((sg, t, d), lambda i: (st(i), 0, 0)),
            pl.BlockSpec((1, 1, d), lambda i: (0, 0, 0)),
            pl.BlockSpec((1, 1, d), lambda i: (pt(i) // lb, 0, 0)),
            pl.BlockSpec((1, 1, d), lambda i: (pt(i) // lb, 0, 0)),
            pl.BlockSpec((sg, 1, d), lambda i: (st(i), 0, 0)),
            pl.BlockSpec((sg, 1, d), lambda i: (st(i), 0, 0)),
            pl.BlockSpec((LANE, d), lambda i: (0, 0)),
        ],
        out_specs=(pl.BlockSpec((rows, d), lambda i: (i, 0)),
                   pl.BlockSpec((rows, LANE), lambda i: (i, 0))),
        compiler_params=_cparams(("arbitrary",)),
        name="prologue",
    )(xp, xs, gain, scp, shp, scs, shs, w_rt)


def _inproj_kernel(dest_ref, src_ref, a_ref, w_ref, o_ref):
    del dest_ref, src_ref
    w = w_ref[...].astype(BF16)
    o_ref[...] = lax.dot_general(a_ref[...], w, NT_DIMS, preferred_element_type=F32).astype(o_ref.dtype)


def _in_proj(a, wt, dest, src, tm, tn, out_dtype):
    m, k = a.shape
    nt = len(dest)
    return pl.pallas_call(
        _inproj_kernel,
        out_shape=jax.ShapeDtypeStruct((m, nt * tn), out_dtype),
        grid_spec=pltpu.PrefetchScalarGridSpec(
            num_scalar_prefetch=2,
            grid=(m // tm, nt),
            in_specs=[
                pl.BlockSpec((tm, k), lambda i, j, d, s: (i, 0), pipeline_mode=pl.Buffered(1)),
                pl.BlockSpec((pl.Element(tn), pl.Element(k)),
                             lambda i, j, d, s: (pl.multiple_of(s[j], GATE_RANK), 0)),
            ],
            out_specs=pl.BlockSpec((tm, tn), lambda i, j, d, s: (i, d[j])),
        ),
        compiler_params=_cparams(("arbitrary", "arbitrary")),
        name="in_proj",
    )(jnp.asarray(dest, jnp.int32), jnp.asarray(src, jnp.int32), a, wt)


def _kv_rows_kernel(a_ref, w_ref, o_ref):
    w = w_ref[...].astype(BF16)
    o_ref[...] = lax.dot_general(a_ref[...], w, NT_DIMS, preferred_element_type=F32)


def _kv_rows(h, wt, b, l, rows_s):
    k = h.shape[1]
    blk = WINDOW
    lb = l // blk
    n_s = rows_s // blk
    row_blk = lambda t: jnp.where(t < b, (t + 1) * lb - 1, b * lb + t - b)
    return pl.pallas_call(
        _kv_rows_kernel,
        out_shape=jax.ShapeDtypeStruct(((b + n_s) * blk, 2 * KV_W), F32),
        grid=(b + n_s,),
        in_specs=[
            pl.BlockSpec((blk, k), lambda t: (row_blk(t), 0)),
            pl.BlockSpec((2 * KV_W, k), lambda t: (SRC_KA // (2 * KV_W), 0)),
        ],
        out_specs=pl.BlockSpec((blk, 2 * KV_W), lambda t: (t, 0)),
        compiler_params=_cparams(("arbitrary",)),
        name="kv_rows",
    )(h, wt)


def _sink_attend(q, k, v, mask, sink):
    s = lax.dot_general(q, k, NT_DIMS, preferred_element_type=F32)
    s = jnp.where(mask, s * (HEAD_DIM ** -0.5), NEG_INF)
    m = jnp.maximum(jnp.max(s, axis=-1, keepdims=True), sink)
    p = jnp.exp(s - m)
    denom = jnp.sum(p, axis=-1, keepdims=True) + jnp.exp(sink - m)
    o = jnp.dot(p.astype(BF16), v, preferred_element_type=F32)
    return o / denom


def _swa_prompt_kernel(sink_ref, q_ref, za_ref, kp_ref, kc_ref, vp_ref, vc_ref, o_ref):
    n = pl.program_id(1)
    blk = ATTN_BLOCK
    pairs = GROUP // 2
    pw = 2 * HEAD_DIM
    hw = GROUP * HEAD_DIM
    k = jnp.concatenate([kp_ref[...], kc_ref[...]], axis=0)
    v = jnp.concatenate([vp_ref[...], vc_ref[...]], axis=0)
    i = lax.broadcasted_iota(jnp.int32, (blk, 2 * blk), 0)
    j = lax.broadcasted_iota(jnp.int32, (blk, 2 * blk), 1)
    visible = (j >= i) & (j <= i + WINDOW) & ((j >= blk) | (n > 0))
    bias = jnp.where(visible, 0.0, NEG_INF)[None]
    pair_id = lax.broadcasted_iota(jnp.int32, (pairs, 1, 1), 0)
    zeros = jnp.zeros((2 * blk, HEAD_DIM), BF16)
    for h in range(N_KV):
        kh = k[:, h * HEAD_DIM:(h + 1) * HEAD_DIM] * (HEAD_DIM ** -0.5)
        vh = v[:, h * HEAD_DIM:(h + 1) * HEAD_DIM]
        q2 = jnp.concatenate([q_ref[:, h * hw + jj * pw:h * hw + (jj + 1) * pw] for jj in range(pairs)], axis=0)
        acc = None
        for half in range(2):
            kv_parts = (lambda t: [t, zeros]) if half == 0 else (lambda t: [zeros, t])
            k_pad = jnp.concatenate(kv_parts(kh), axis=1)
            v_pad = jnp.concatenate(kv_parts(vh), axis=1)
            sink = jnp.zeros((pairs, 1, 1), F32)
            for jj in range(pairs):
                sink = jnp.where(pair_id == jj, sink_ref[h * GROUP + 2 * jj + half], sink)
            s = lax.dot_general(q2, k_pad, NT_DIMS, preferred_element_type=F32)
            s = s.reshape(pairs, blk, 2 * blk) + bias
            m = jnp.maximum(jnp.max(s, axis=-1, keepdims=True), sink)
            p = jnp.exp(s - m)
            denom = jnp.sum(p, axis=-1, keepdims=True) + jnp.exp(sink - m)
            o = jnp.dot(p.reshape(pairs * blk, 2 * blk).astype(BF16), v_pad, preferred_element_type=F32)
            o = o.reshape(pairs, blk, pw) / denom
            acc = o if acc is None else acc + o
        o_h = jnp.concatenate([acc[jj] for jj in range(pairs)], axis=1)
        za = za_ref[:, h * hw:(h + 1) * hw].astype(F32)
        o_ref[:, h * hw:(h + 1) * hw] = (o_h * _silu(za)).astype(BF16)


def _swa_prompt(sink, p, b, l):
    nb = l // ATTN_BLOCK
    kcol = OFF_KA // KV_W
    vcol = OFF_VA // KV_W
    blk_q = (ATTN_BLOCK, WIDTH_A)
    blk_kv = (ATTN_BLOCK, KV_W)
    cur = lambda bi, n: bi * nb + n
    prev = lambda bi, n: bi * nb + jnp.maximum(n - 1, 0)
    return pl.pallas_call(
        _swa_prompt_kernel,
        out_shape=jax.ShapeDtypeStruct((b * l, WIDTH_A), BF16),
        grid=(b, nb),
        in_specs=[
            pl.BlockSpec(memory_space=pltpu.SMEM),
            pl.BlockSpec(blk_q, lambda bi, n: (cur(bi, n), OFF_QA // WIDTH_A)),
            pl.BlockSpec(blk_q, lambda bi, n: (cur(bi, n), OFF_ZA // WIDTH_A)),
            pl.BlockSpec(blk_kv, lambda bi, n: (prev(bi, n), kcol)),
            pl.BlockSpec(blk_kv, lambda bi, n: (cur(bi, n), kcol)),
            pl.BlockSpec(blk_kv, lambda bi, n: (prev(bi, n), vcol)),
            pl.BlockSpec(blk_kv, lambda bi, n: (cur(bi, n), vcol)),
        ],
        out_specs=pl.BlockSpec(blk_q, lambda bi, n: (cur(bi, n), 0)),
        compiler_params=_cparams(("arbitrary", "arbitrary")),
        name="swa_prompt",
    )(sink, p, p, p, p, p, p)


def _swa_sample_kernel(g_blk, t_len, w_len, sink_ref, q_ref, za_ref, kv_ref, kc_ref, vc_ref,
                       o_ref, kw_ref, vw_ref):
    kpad = 2 * WINDOW - w_len - t_len
    rows = GROUP * t_len
    r = lax.broadcasted_iota(jnp.int32, (rows, 2 * WINDOW), 0)
    j = lax.broadcasted_iota(jnp.int32, (rows, 2 * WINDOW), 1)
    t = r % t_len
    mask = (j <= w_len + t) & (j >= w_len + t - WINDOW) & (j < w_len + t_len)
    rg = lax.broadcasted_iota(jnp.int32, (rows, 1), 0) // t_len
    sink_cols = []
    for h in range(N_KV):
        col = jnp.zeros((rows, 1), F32)
        for g in range(GROUP):
            col = jnp.where(rg == g, sink_ref[h * GROUP + g], col)
        sink_cols.append(col)
    zpad = jnp.zeros((kpad, KV_W), F32)
    q_all = q_ref[...].astype(F32)
    za_all = za_ref[...].astype(F32)
    seq_outs = []
    for s in range(g_blk):
        kc = kc_ref[s]
        vc = vc_ref[s]
        kn = kv_ref[s, :, :KV_W]
        vn = kv_ref[s, :, KV_W:]
        kw_ref[s] = jnp.concatenate([kc[t_len:], kn], axis=0)
        vw_ref[s] = jnp.concatenate([vc[t_len:], vn], axis=0)
        k_all = jnp.concatenate([kc, kn, zpad], axis=0).astype(BF16)
        v_all = jnp.concatenate([vc, vn, zpad], axis=0).astype(BF16)
        q = q_all[s * t_len:(s + 1) * t_len]
        pieces = []
        for h in range(N_KV):
            kh = k_all[:, h * HEAD_DIM:(h + 1) * HEAD_DIM]
            vh = v_all[:, h * HEAD_DIM:(h + 1) * HEAD_DIM]
            qh = jnp.concatenate(
                [q[:, (h * GROUP + g) * HEAD_DIM:(h * GROUP + g + 1) * HEAD_DIM] for g in range(GROUP)],
                axis=0).astype(BF16)
            oh = _sink_attend(qh, kh, vh, mask, sink_cols[h])
            for g in range(GROUP):
                pieces.append(oh[g * t_len:(g + 1) * t_len])
        o = jnp.concatenate(pieces, axis=1)
        seq_outs.append(o * _silu(za_all[s * t_len:(s + 1) * t_len]))
    o_ref[...] = jnp.concatenate(seq_outs, axis=0).astype(BF16)


def _swa_sample(sink, p, row0, n, t_len, kv3, kv_seq0, cache_k, cache_v, g_blk=4):
    rows = g_blk * t_len
    assert rows % BF16_ROWS == 0
    w_len = cache_k.shape[1]
    blk_q = (rows, WIDTH_A)
    blk_c = (g_blk, w_len, KV_W)
    r0 = row0 // rows
    s0 = kv_seq0 // g_blk
    return pl.pallas_call(
        functools.partial(_swa_sample_kernel, g_blk, t_len, w_len),
        out_shape=(jax.ShapeDtypeStruct((n * t_len, WIDTH_A), BF16),
                   jax.ShapeDtypeStruct((n, w_len, KV_W), F32),
                   jax.ShapeDtypeStruct((n, w_len, KV_W), F32)),
        grid=(n // g_blk,),
        in_specs=[
            pl.BlockSpec(memory_space=pltpu.SMEM),
            pl.BlockSpec(blk_q, lambda i: (r0 + i, OFF_QA // WIDTH_A)),
            pl.BlockSpec(blk_q, lambda i: (r0 + i, OFF_ZA // WIDTH_A)),
            pl.BlockSpec((g_blk, t_len, 2 * KV_W), lambda i: (s0 + i, 0, 0)),
            pl.BlockSpec(blk_c, lambda i: (i, 0, 0)),
            pl.BlockSpec(blk_c, lambda i: (i, 0, 0)),
        ],
        out_specs=(pl.BlockSpec(blk_q, lambda i: (i, 0)),
                   pl.BlockSpec(blk_c, lambda i: (i, 0, 0)),
                   pl.BlockSpec(blk_c, lambda i: (i, 0, 0))),
        compiler_params=_cparams(("arbitrary",)),
        name="swa_sample",
    )(sink, p, p, kv3, cache_k, cache_v)


def _split_hi_lo(x):
    hi = x.astype(BF16)
    lo = (x - hi.astype(F32)).astype(BF16)
    return hi, lo


def _cumsum_rows(x):
    c = x.shape[0]
    row = lax.broadcasted_iota(jnp.int32, (c, 1), 0)
    sh = 1
    while sh < c:
        x = x + jnp.where(row >= sh, pltpu.roll(x, sh, 0), 0.0)
        sh *= 2
    return x


def _gla_chunk(q, k, v, la, s_old, ones_cl, causal):
    c = q.shape[0]
    b = _cumsum_rows(la)
    b_last = b[c - 1:c, :]
    la_hi, la_lo = _split_hi_lo(la)
    bl_col = (lax.dot_general(la_hi, ones_cl, TN_DIMS, preferred_element_type=F32)
              + lax.dot_general(la_lo, ones_cl, TN_DIMS, preferred_element_type=F32))
    decay_col = jnp.exp(bl_col)
    q_t = (q * jnp.exp(b) * (DK_B ** -0.5)).astype(BF16)
    k_t = (k * jnp.exp(-b)).astype(BF16)
    k_d = (k * jnp.exp(b_last - b)).astype(BF16)
    outs, states = [], []
    for h in range(N_HEADS_B):
        dk = slice(h * DK_B, (h + 1) * DK_B)
        dv = slice(h * DV_B, (h + 1) * DV_B)
        att = lax.dot_general(q_t[:, dk], k_t[:, dk], NT_DIMS, preferred_element_type=F32)
        att = jnp.where(causal, att, 0.0).astype(BF16)
        outs.append(jnp.dot(att, v[:, dv], preferred_element_type=F32)
                    + jnp.dot(q_t[:, dk], s_old[h].astype(BF16), preferred_element_type=F32))
        decay = jnp.concatenate([decay_col[dk]] * (DV_B // LANE), axis=1)
        states.append(decay * s_old[h] + lax.dot_general(k_d[:, dk], v[:, dv], TN_DIMS,
                                                         preferred_element_type=F32))
    return outs, states


def _log_decay(r, w2, b2):
    z = jnp.dot(r.astype(BF16), w2, preferred_element_type=F32) + b2
    ls = jnp.minimum(z, 0.0) - jnp.log(1.0 + jnp.exp(-jnp.abs(z)))
    return ls / GLA_TAU


def _gla_finish(o, zb, gain):
    on = o * lax.rsqrt(jnp.mean(o * o, axis=-1, keepdims=True) + EPS) * gain
    return on * _silu(zb)


def _chunk_consts(c):
    ri = lax.broadcasted_iota(jnp.int32, (c, c), 0)
    ci = lax.broadcasted_iota(jnp.int32, (c, c), 1)
    return jnp.ones((c, LANE), BF16), ri >= ci


def _gla_prompt_kernel(nb, *refs):
    w2_ref, b2_ref, gain_ref, o_ref, s_ref = refs[5 * nb:]

    @pl.when(pl.program_id(0) == 0)
    def _():
        s_ref[...] = jnp.zeros_like(s_ref)

    ones_cl, causal = _chunk_consts(o_ref.shape[1])
    for bi in range(nb):
        q_ref, k_ref, v_ref, zb_ref, r_ref = refs[5 * bi:5 * bi + 5]
        la = _log_decay(r_ref[...], w2_ref[...], b2_ref[...])
        outs, states = _gla_chunk(q_ref[...].astype(F32), k_ref[...].astype(F32), v_ref[...], la,
                                  [s_ref[bi, h] for h in range(N_HEADS_B)], ones_cl, causal)
        for h in range(N_HEADS_B):
            dv = slice(h * DV_B, (h + 1) * DV_B)
            s_ref[bi, h] = states[h]
            o_ref[bi, :, dv] = _gla_finish(outs[h], zb_ref[:, dv].astype(F32), gain_ref[:, dv]).astype(BF16)


def _gla_prompt(p, r, w2, b2, gain, b, l):
    c = GLA_CHUNK
    nc = l // c
    in_specs, args = [], []
    for bi in range(b):
        for arr, width, off in ((p, KEY_B, OFF_QB), (p, KEY_B, OFF_KB), (p, WIDTH_B, OFF_VB),
                                (p, WIDTH_B, OFF_ZB), (r, LANE, 0)):
            in_specs.append(pl.BlockSpec((c, width), functools.partial(
                lambda ci, row0, col: (row0 + ci, col), row0=bi * nc, col=off // width)))
            args.append(arr)
    in_specs += [pl.BlockSpec((LANE, KEY_B), lambda ci: (0, 0)),
                 pl.BlockSpec((1, KEY_B), lambda ci: (0, 0)),
                 pl.BlockSpec((1, WIDTH_B), lambda ci: (0, 0))]
    return pl.pallas_call(
        functools.partial(_gla_prompt_kernel, b),
        out_shape=(jax.ShapeDtypeStruct((b, l, WIDTH_B), BF16),
                   jax.ShapeDtypeStruct((b, N_HEADS_B, DK_B, DV_B), F32)),
        grid=(nc,),
        in_specs=in_specs,
        out_specs=(pl.BlockSpec((b, c, WIDTH_B), lambda ci: (0, ci, 0)),
                   pl.BlockSpec((b, N_HEADS_B, DK_B, DV_B), lambda ci: (0, 0, 0, 0))),
        compiler_params=_cparams(("arbitrary",)),
        name="gla_prompt",
    )(*args, w2, b2, gain)


def _gla_sample_kernel(g_blk, t_len, q_ref, k_ref, v_ref, zb_ref, r_ref, w2_ref, b2_ref, gain_ref, s0_ref,
                       o_ref, s_ref):
    c = 2 * t_len
    ones_cl, causal = _chunk_consts(c)

    def pad(x):
        return jnp.concatenate([x, jnp.zeros_like(x)], axis=0)

    live = lax.broadcasted_iota(jnp.int32, (c, 1), 0) < t_len
    q_all = q_ref[...].astype(F32)
    k_all = k_ref[...].astype(F32)
    v_all = v_ref[...].astype(F32)
    zb_all = zb_ref[...].astype(F32)
    seq_outs = []
    for s in range(g_blk):
        rs = slice(s * t_len, (s + 1) * t_len)
        la = jnp.where(live, _log_decay(pad(r_ref[s]), w2_ref[...], b2_ref[...]), 0.0)
        outs, states = _gla_chunk(pad(q_all[rs]), pad(k_all[rs]), pad(v_all[rs]).astype(BF16), la,
                                  [s0_ref[s, h] for h in range(N_HEADS_B)], ones_cl, causal)
        fin = []
        for h in range(N_HEADS_B):
            dv = slice(h * DV_B, (h + 1) * DV_B)
            s_ref[s, h] = states[h]
            fin.append(_gla_finish(outs[h][:t_len], zb_all[rs, dv], gain_ref[:, dv]))
        seq_outs.append(jnp.concatenate(fin, axis=1))
    o_ref[...] = jnp.concatenate(seq_outs, axis=0).astype(BF16)


def _gla_sample(p, r3, row0, t_len, w2, b2, gain, state, g_blk=4):
    n = state.shape[0]
    rows = g_blk * t_len
    assert rows % BF16_ROWS == 0
    st_blk = (g_blk, N_HEADS_B, DK_B, DV_B)
    r0 = row0 // rows
    return pl.pallas_call(
        functools.partial(_gla_sample_kernel, g_blk, t_len),
        out_shape=(jax.ShapeDtypeStruct((n * t_len, WIDTH_B), BF16),
                   jax.ShapeDtypeStruct(state.shape, F32)),
        grid=(n // g_blk,),
        in_specs=[
            pl.BlockSpec((rows, KEY_B), lambda i: (r0 + i, OFF_QB // KEY_B)),
            pl.BlockSpec((rows, KEY_B), lambda i: (r0 + i, OFF_KB // KEY_B)),
            pl.BlockSpec((rows, WIDTH_B), lambda i: (r0 + i, OFF_VB // WIDTH_B)),
            pl.BlockSpec((rows, WIDTH_B), lambda i: (r0 + i, OFF_ZB // WIDTH_B)),
            pl.BlockSpec((g_blk, t_len, LANE), lambda i: (r0 + i, 0, 0)),
            pl.BlockSpec((LANE, KEY_B), lambda i: (0, 0)),
            pl.BlockSpec((1, KEY_B), lambda i: (0, 0)),
            pl.BlockSpec((1, WIDTH_B), lambda i: (0, 0)),
            pl.BlockSpec(st_blk, lambda i: (i, 0, 0, 0)),
        ],
        out_specs=(pl.BlockSpec((rows, WIDTH_B), lambda i: (i, 0)),
                   pl.BlockSpec(st_blk, lambda i: (i, 0, 0, 0))),
        compiler_params=_cparams(("arbitrary",)),
        name="gla_sample",
    )(p, p, p, p, r3, w2, b2, gain, state)


def _merge_kernel(nchunk, a_ref, b_ref, wa_ref, wb_ref, ga_ref, gb_ref, o_ref):
    cw = o_ref.shape[1] // nchunk
    for c in range(nchunk):
        cs = slice(c * cw, (c + 1) * cw)
        ua = jnp.dot(a_ref[...], wa_ref[:, cs], preferred_element_type=F32)
        ub = jnp.dot(b_ref[...], wb_ref[:, cs], preferred_element_type=F32)
        merged = (jax.nn.sigmoid(ga_ref[:, cs].astype(F32)) * ua
                  + jax.nn.sigmoid(gb_ref[:, cs].astype(F32)) * ub)
        o_ref[:, cs] = merged.astype(BF16)


def _merge(a_in, b_in, w_pa, w_pb, p, row0, tm):
    m = a_in.shape[0]
    r0 = row0 // tm
    d = D_MODEL
    resident = dict(pipeline_mode=pl.Buffered(1))
    return pl.pallas_call(
        functools.partial(_merge_kernel, 4),
        out_shape=jax.ShapeDtypeStruct((m, d), BF16),
        grid=(m // tm,),
        in_specs=[
            pl.BlockSpec((tm, WIDTH_A), lambda i: (i, 0)),
            pl.BlockSpec((tm, WIDTH_B), lambda i: (i, 0)),
            pl.BlockSpec((WIDTH_A, d), lambda i: (0, 0), **resident),
            pl.BlockSpec((WIDTH_B, d), lambda i: (0, 0), **resident),
            pl.BlockSpec((tm, d), lambda i: (r0 + i, OFF_GA // d)),
            pl.BlockSpec((tm, d), lambda i: (r0 + i, OFF_GB // d)),
        ],
        out_specs=pl.BlockSpec((tm, d), lambda i: (i, 0)),
        compiler_params=_cparams(("arbitrary",), VMEM_LIMIT_RESIDENT),
        name="merge",
    )(a_in, b_in, w_pa, w_pb, p, p)


def _out_kernel(nchunk, m_ref, w_ref, x_ref, gate_ref, fg_ref, o_ref):
    g_blk, r_blk, d = x_ref.shape
    cw = d // nchunk
    ssq = jnp.zeros((g_blk, r_blk, 1), F32)
    for c in range(nchunk):
        cs = slice(c * cw, (c + 1) * cw)
        y = jnp.dot(m_ref[...], w_ref[:, cs], preferred_element_type=F32)
        xn = x_ref[:, :, cs] + gate_ref[:, :, cs] * y.reshape(g_blk, r_blk, cw)
        o_ref[:, :, cs] = xn
        ssq = ssq + jnp.sum(xn * xn, axis=-1, keepdims=True)
    inv = lax.rsqrt(ssq * (1.0 / d) + EPS)
    for c in range(nchunk):
        cs = slice(c * cw, (c + 1) * cw)
        o_ref[:, :, cs] = o_ref[:, :, cs] * inv * fg_ref[:, :, cs]


def _out_proj(merged, w_o, x3, gate3, fin_gain, g_blk, r_blk):
    n, l, d = x3.shape
    rows = g_blk * r_blk
    lb = l // r_blk
    return pl.pallas_call(
        functools.partial(_out_kernel, 4),
        out_shape=jax.ShapeDtypeStruct((n, l, d), F32),
        grid=(n // g_blk, lb),
        in_specs=[
            pl.BlockSpec((rows, d), lambda i, r: (i * lb + r, 0)),
            pl.BlockSpec((d, d), lambda i, r: (0, 0), pipeline_mode=pl.Buffered(1)),
            pl.BlockSpec((g_blk, r_blk, d), lambda i, r: (i, r, 0)),
            pl.BlockSpec((g_blk, 1, d), lambda i, r: (i, 0, 0)),
            pl.BlockSpec((1, 1, d), lambda i, r: (0, 0, 0)),
        ],
        out_specs=pl.BlockSpec((g_blk, r_blk, d), lambda i, r: (i, r, 0)),
        compiler_params=_cparams(("arbitrary", "arbitrary"), VMEM_LIMIT_RESIDENT),
        name="out_proj",
    )(merged, w_o, x3, gate3, fin_gain)


def kernel(x_prompt, x_sample, cache_k_win, cache_v_win, state_gla, c_prompt, c_sample,
           w_ada, b_ada, norm_gain, w_in, attn_sink, w_alpha2, b_alpha, gla_norm_gain,
           w_proj_a, w_proj_b, w_out, final_norm_gain):
    assert w_ada.shape[0] == 1, "single-layer step"
    bp, lp, d = x_prompt.shape
    ns, ts, _ = x_sample.shape
    w_len = cache_k_win.shape[2]
    rows_p = bp * lp
    rows_s = ns * ts
    rows = rows_p + rows_s

    wt = jnp.transpose(w_in[0])
    w_rt = jnp.pad(wt[SRC_MAIN:SRC_GATES], ((0, LANE - GATE_RANK), (0, 0))).astype(BF16)
    w2 = jnp.pad(w_alpha2[0], ((0, LANE - GATE_RANK), (0, 0))).astype(BF16)
    b2 = b_alpha[0].reshape(1, KEY_B)
    gla_gain = gla_norm_gain[0].reshape(1, WIDTH_B)
    w_pa = w_proj_a[0].astype(BF16)
    w_pb = w_proj_b[0].astype(BF16)
    w_o = w_out[0].astype(BF16)
    gain3 = norm_gain[0].reshape(1, 1, d)
    fin3 = final_norm_gain.reshape(1, 1, d)
    sink = attn_sink[0]

    m_all = ns + bp
    m_pad = -(-m_all // 8) * 8
    c_all = jnp.concatenate([c_sample, c_prompt, jnp.zeros((m_pad - m_all, d), F32)], axis=0)
    mod = _adaln(c_all, w_ada[0], b_ada[0].reshape(1, 3 * d))
    shift, scale, gate = mod[:, :d], mod[:, d:2 * d], mod[:, 2 * d:]
    sl_s, sl_p = slice(0, ns), slice(ns, ns + bp)
    as3 = lambda a, n: a.reshape(n, 1, d)

    h, r = _prologue(x_prompt, x_sample, gain3, as3(scale[sl_p], bp), as3(shift[sl_p], bp),
                     as3(scale[sl_s], ns), as3(shift[sl_s], ns), w_rt, 512)
    p = _in_proj(h, wt, PROJ_DEST, PROJ_SRC, rows // 4, PROJ_TN, BF16)
    kv = _kv_rows(h, wt, bp, lp, rows_s)
    kv_p = kv[:bp * WINDOW].reshape(bp, WINDOW, 2, N_KV, HEAD_DIM)
    k_win_p = kv_p[:, :, 0][None]
    v_win_p = kv_p[:, :, 1][None]
    kv3 = kv.reshape(kv.shape[0] // ts, ts, 2 * KV_W)
    r3 = r.reshape(rows // ts, ts, LANE)

    ua_p = _swa_prompt(sink, p, bp, lp)
    ub_p, s_p = _gla_prompt(p, r, w2, b2, gla_gain, bp, lp)
    merged_p = _merge(ua_p, ub_p.reshape(rows_p, WIDTH_B), w_pa, w_pb, p, 0, 256)
    y_p = _out_proj(merged_p, w_o, x_prompt, as3(gate[sl_p], bp), fin3, 1, 256)

    ua_s, k_win_s, v_win_s = _swa_sample(sink, p, rows_p, ns, ts, kv3, bp * WINDOW // ts,
                                         cache_k_win[0].reshape(ns, w_len, KV_W),
                                         cache_v_win[0].reshape(ns, w_len, KV_W))
    ub_s, s_s = _gla_sample(p, r3, rows_p, ts, w2, b2, gla_gain, state_gla[0])
    merged_s = _merge(ua_s, ub_s, w_pa, w_pb, p, rows_p, 256)
    y_s = _out_proj(merged_s, w_o, x_sample, as3(gate[sl_s], ns), fin3, 256 // ts, ts)

    return (y_p, y_s, k_win_p, v_win_p, s_p[None],
            k_win_s.reshape(1, ns, w_len, N_KV, HEAD_DIM), v_win_s.reshape(1, ns, w_len, N_KV, HEAD_DIM),
            s_s[None])
```

```python
import functools

import jax
import jax.numpy as jnp
from jax import lax
from jax.experimental import pallas as pl
from jax.experimental.pallas import tpu as pltpu

F32 = jnp.float32
BF16 = jnp.bfloat16

D_MODEL = 4096
WIDTH_A = 2048
HEAD_DIM = 64
N_HEADS_A = 32
N_KV = 4
GROUP = 8
WINDOW = 128
ATTN_BLOCK = 128
WIDTH_B = 2048
N_HEADS_B = 4
DV_B = 512
KEY_B = 1024
DK_B = 256
GATE_RANK = 16
GLA_TAU = 16.0
GLA_CHUNK = 64
EPS = 1e-6
NEG_INF = -1e30
KV_W = N_KV * HEAD_DIM

V7X_VMEM_BYTES = 64 * 1024 * 1024
VMEM_LIMIT = V7X_VMEM_BYTES - 8 * 1024 * 1024
VMEM_LIMIT_RESIDENT = V7X_VMEM_BYTES - 4 * 1024 * 1024
LANE = 128
BF16_ROWS = 16

SRC_KA = WIDTH_A
SRC_MAIN = WIDTH_A + 2 * KV_W + WIDTH_A + 2 * KEY_B + 2 * WIDTH_B
SRC_GATES = SRC_MAIN + GATE_RANK
PROJ_TN = 512
OFF_GA = 0
OFF_GB = 4096
OFF_QA = 8192
OFF_ZA = 10240
OFF_VB = 12288
OFF_ZB = 14336
OFF_QB = 16384
OFF_KB = 17408
OFF_KA = 18432
OFF_VA = 18688
PROJ_COLS = 18944
PROJ_DEST = tuple(16 + t for t in (0, 1, 2, 3, 20, 4, 5, 6, 7, 16, 17, 18, 19, 8, 9, 10, 11, 12, 13, 14, 15)) \
    + tuple(range(16))
PROJ_SRC = tuple(j * PROJ_TN for j in range(21)) + tuple(SRC_GATES + j * PROJ_TN for j in range(16))

NT_DIMS = (((1,), (1,)), ((), ()))
TN_DIMS = (((0,), (0,)), ((), ()))


def _cparams(sem, vmem_limit=VMEM_LIMIT):
    return pltpu.CompilerParams(dimension_semantics=sem, vmem_limit_bytes=vmem_limit)


def _silu(x):
    return x * jax.nn.sigmoid(x)


def _adaln_kernel(c_ref, w_ref, b_ref, o_ref):
    a = _silu(c_ref[...]).astype(BF16)
    w = w_ref[...].astype(BF16)
    o_ref[...] = jnp.dot(a, w, preferred_element_type=F32) + b_ref[...]


def _adaln(c_all, w_ada, b_ada, tn=512):
    m, k = c_all.shape
    n = w_ada.shape[1]
    return pl.pallas_call(
        _adaln_kernel,
        out_shape=jax.ShapeDtypeStruct((m, n), F32),
        grid=(n // tn,),
        in_specs=[
            pl.BlockSpec((m, k), lambda j: (0, 0)),
            pl.BlockSpec((k, tn), lambda j: (0, j)),
            pl.BlockSpec((1, tn), lambda j: (0, j)),
        ],
        out_specs=pl.BlockSpec((m, tn), lambda j: (0, j)),
        compiler_params=_cparams(("arbitrary",)),
        name="adaln",
    )(c_all, w_ada, b_ada)


def _prologue_kernel(n_ptiles, xp_ref, xs_ref, g_ref, scp_ref, shp_ref, scs_ref, shs_ref, wr_ref,
                     h_ref, r_ref):
    i = pl.program_id(0)

    def emit(x, scale, shift):
        ms = jnp.mean(x * x, axis=-1, keepdims=True)
        y = x * lax.rsqrt(ms + EPS) * g_ref[...]
        h = y * (1.0 + scale) + shift
        hb = h.reshape(h.shape[0] * h.shape[1], h.shape[2]).astype(BF16)
        h_ref[...] = hb
        r_ref[...] = lax.dot_general(hb, wr_ref[...], NT_DIMS, preferred_element_type=F32)

    @pl.when(i < n_ptiles)
    def _():
        emit(xp_ref[...], scp_ref[...], shp_ref[...])

    @pl.when(i >= n_ptiles)
    def _():
        emit(xs_ref[...], scs_ref[...], shs_ref[...])


def _prologue(xp, xs, gain, scp, shp, scs, shs, w_rt, rows):
    b, l, d = xp.shape
    n, t, _ = xs.shape
    lb = l // rows
    n_pt = b * lb
    sg = rows // t
    n_st = n // sg
    pt = lambda i: jnp.minimum(i, n_pt - 1)
    st = lambda i: jnp.maximum(i - n_pt, 0)
    return pl.pallas_call(
        functools.partial(_prologue_kernel, n_pt),
        out_shape=(jax.ShapeDtypeStruct((b * l + n * t, d), BF16),
                   jax.ShapeDtypeStruct((b * l + n * t, LANE), F32)),
        grid=(n_pt + n_st,),
        in_specs=[
            pl.BlockSpec((1, rows, d), lambda i: (pt(i) // lb, pt(i) % lb, 0)),
            pl.BlockSpec((sg, t, d), lambda i: (st(i), 0, 0)),
            pl.BlockSpec((1, 1, d), lambda i: (0, 0, 0)),
            pl.BlockSpec((1, 1, d), lambda i: (pt(i) // lb, 0, 0)),
            pl.BlockSpec((1, 1, d), lambda i: (pt(i) // lb, 0, 0)),
            pl.BlockSpec((sg, 1, d), lambda i: (st(i), 0, 0)),
            pl.BlockSpec((sg, 1, d), lambda i: (st(i), 0, 0)),
            pl.BlockSpec((LANE, d), lambda i: (0, 0)),
        ],
        out_specs=(pl.BlockSpec((rows, d), lambda i: (i, 0)),
                   pl.BlockSpec((rows, LANE), lambda i: (i, 0))),
        compiler_params=_cparams(("arbitrary",)),
        name="prologue",
    )(xp, xs, gain, scp, shp, scs, shs, w_rt)


def _inproj_kernel(n_cast, n_xpose, dest_ref, src_ref, a_ref, w_ref, *refs):
    del dest_ref, src_ref
    side_in, o_ref, side_out = refs[:n_cast + n_xpose], refs[n_cast + n_xpose], refs[n_cast + n_xpose + 1:]
    w = w_ref[...].astype(BF16)
    o_ref[...] = lax.dot_general(a_ref[...], w, NT_DIMS, preferred_element_type=F32).astype(o_ref.dtype)

    for src, dst in zip(side_in[:n_cast], side_out[:n_cast]):
        dst[...] = src[...].astype(dst.dtype)
    for src, dst in zip(side_in[n_cast:], side_out[n_cast:]):
        dst[0] = jnp.transpose(src[0])


def _in_proj(a, wt, dest, src, tm, tn, out_dtype, casts, xposes, n_side):
    m, k = a.shape
    nt = len(dest)
    assert (m // tm) * nt >= n_side
    chunk = lambda i, j, d, s: jnp.minimum(i * nt + j, n_side - 1)
    side_in, side_out, side_shapes = [], [], []
    for arr in casts:
        blk = (arr.shape[0] // n_side, arr.shape[1])
        assert blk[0] % BF16_ROWS == 0
        spec = pl.BlockSpec(blk, lambda i, j, d, s: (chunk(i, j, d, s), 0))
        side_in.append(spec)
        side_out.append(spec)
        side_shapes.append(jax.ShapeDtypeStruct(arr.shape, BF16))
    for arr in xposes:
        _, r, c = arr.shape
        side_in.append(pl.BlockSpec((1, r, c), lambda i, j, d, s: (chunk(i, j, d, s), 0, 0)))
        side_out.append(pl.BlockSpec((1, c, r), lambda i, j, d, s: (chunk(i, j, d, s), 0, 0)))
        side_shapes.append(jax.ShapeDtypeStruct((n_side, c, r), arr.dtype))
    return pl.pallas_call(
        functools.partial(_inproj_kernel, len(casts), len(xposes)),
        out_shape=[jax.ShapeDtypeStruct((m, nt * tn), out_dtype)] + side_shapes,
        grid_spec=pltpu.PrefetchScalarGridSpec(
            num_scalar_prefetch=2,
            grid=(m // tm, nt),
            in_specs=[
                pl.BlockSpec((tm, k), lambda i, j, d, s: (i, 0), pipeline_mode=pl.Buffered(1)),
                pl.BlockSpec((pl.Element(tn), pl.Element(k)),
                             lambda i, j, d, s: (pl.multiple_of(s[j], GATE_RANK), 0)),
            ] + side_in,
            out_specs=[pl.BlockSpec((tm, tn), lambda i, j, d, s: (i, d[j]))] + side_out,
        ),
        compiler_params=_cparams(("arbitrary", "arbitrary")),
        name="in_proj",
    )(jnp.asarray(dest, jnp.int32), jnp.asarray(src, jnp.int32), a, wt, *casts, *xposes)


def _kv_rows_kernel(a_ref, w_ref, o_ref):
    w = w_ref[...].astype(BF16)
    o_ref[...] = lax.dot_general(a_ref[...], w, NT_DIMS, preferred_element_type=F32)


def _kv_rows(h, wt, b, l, rows_s):
    k = h.shape[1]
    blk = WINDOW
    lb = l // blk
    n_s = rows_s // blk
    row_blk = lambda t: jnp.where(t < b, (t + 1) * lb - 1, b * lb + t - b)
    return pl.pallas_call(
        _kv_rows_kernel,
        out_shape=jax.ShapeDtypeStruct(((b + n_s) * blk, 2 * KV_W), F32),
        grid=(b + n_s,),
        in_specs=[
            pl.BlockSpec((blk, k), lambda t: (row_blk(t), 0)),
            pl.BlockSpec((2 * KV_W, k), lambda t: (SRC_KA // (2 * KV_W), 0)),
        ],
        out_specs=pl.BlockSpec((blk, 2 * KV_W), lambda t: (t, 0)),
        compiler_params=_cparams(("arbitrary",)),
        name="kv_rows",
    )(h, wt)


def _sink_attend(q, k, v, mask, sink):
    s = lax.dot_general(q, k, NT_DIMS, preferred_element_type=F32)
    s = jnp.where(mask, s * (HEAD_DIM ** -0.5), NEG_INF)
    m = jnp.maximum(jnp.max(s, axis=-1, keepdims=True), sink)
    p = jnp.exp(s - m)
    denom = jnp.sum(p, axis=-1, keepdims=True) + jnp.exp(sink - m)
    o = jnp.dot(p.astype(BF16), v, preferred_element_type=F32)
    return o / denom


def _swa_prompt_kernel(sink_ref, q_ref, za_ref, kp_ref, kc_ref, vp_ref, vc_ref, o_ref):
    n = pl.program_id(1)
    blk = ATTN_BLOCK
    pairs = GROUP // 2
    pw = 2 * HEAD_DIM
    hw = GROUP * HEAD_DIM
    k = jnp.concatenate([kp_ref[...], kc_ref[...]], axis=0)
    v = jnp.concatenate([vp_ref[...], vc_ref[...]], axis=0)
    rows = pairs * blk
    i = lax.broadcasted_iota(jnp.int32, (rows, 2 * blk), 0) % blk
    j = lax.broadcasted_iota(jnp.int32, (rows, 2 * blk), 1)
    visible = (j >= i) & (j <= i + WINDOW) & ((j >= blk) | (n > 0))
    bias = jnp.where(visible, 0.0, NEG_INF)
    pair_id = lax.broadcasted_iota(jnp.int32, (rows, 1), 0) // blk
    zeros = jnp.zeros((2 * blk, HEAD_DIM), BF16)
    for h in range(N_KV):
        kh = k[:, h * HEAD_DIM:(h + 1) * HEAD_DIM] * (HEAD_DIM ** -0.5)
        vh = v[:, h * HEAD_DIM:(h + 1) * HEAD_DIM]
        q2 = jnp.concatenate([q_ref[:, h * hw + jj * pw:h * hw + (jj + 1) * pw] for jj in range(pairs)], axis=0)
        acc = None
        for half in range(2):
            kv_parts = (lambda t: [t, zeros]) if half == 0 else (lambda t: [zeros, t])
            k_pad = jnp.concatenate(kv_parts(kh), axis=1)
            v_pad = jnp.concatenate(kv_parts(vh), axis=1)
            sink = jnp.zeros((rows, 1), F32)
            for jj in range(pairs):
                sink = jnp.where(pair_id == jj, sink_ref[h * GROUP + 2 * jj + half], sink)
            s = lax.dot_general(q2, k_pad, NT_DIMS, preferred_element_type=F32) + bias
            m = jnp.maximum(jnp.max(s, axis=-1, keepdims=True), sink)
            p = jnp.exp(s - m)
            denom = jnp.sum(p, axis=-1, keepdims=True) + jnp.exp(sink - m)
            o = jnp.dot(p.astype(BF16), v_pad, preferred_element_type=F32) / denom
            acc = o if acc is None else acc + o
        o_h = jnp.concatenate([acc[jj * blk:(jj + 1) * blk] for jj in range(pairs)], axis=1)
        za = za_ref[:, h * hw:(h + 1) * hw].astype(F32)
        o_ref[:, h * hw:(h + 1) * hw] = (o_h * _silu(za)).astype(BF16)


def _swa_prompt(sink, p, b, l):
    nb = l // ATTN_BLOCK
    kcol = OFF_KA // KV_W
    vcol = OFF_VA // KV_W
    blk_q = (ATTN_BLOCK, WIDTH_A)
    blk_kv = (ATTN_BLOCK, KV_W)
    cur = lambda bi, n: bi * nb + n
    prev = lambda bi, n: bi * nb + jnp.maximum(n - 1, 0)
    return pl.pallas_call(
        _swa_prompt_kernel,
        out_shape=jax.ShapeDtypeStruct((b * l, WIDTH_A), BF16),
        grid=(b, nb),
        in_specs=[
            pl.BlockSpec(memory_space=pltpu.SMEM),
            pl.BlockSpec(blk_q, lambda bi, n: (cur(bi, n), OFF_QA // WIDTH_A)),
            pl.BlockSpec(blk_q, lambda bi, n: (cur(bi, n), OFF_ZA // WIDTH_A)),
            pl.BlockSpec(blk_kv, lambda bi, n: (prev(bi, n), kcol)),
            pl.BlockSpec(blk_kv, lambda bi, n: (cur(bi, n), kcol)),
            pl.BlockSpec(blk_kv, lambda bi, n: (prev(bi, n), vcol)),
            pl.BlockSpec(blk_kv, lambda bi, n: (cur(bi, n), vcol)),
        ],
        out_specs=pl.BlockSpec(blk_q, lambda bi, n: (cur(bi, n), 0)),
        compiler_params=_cparams(("arbitrary", "arbitrary")),
        name="swa_prompt",
    )(sink, p, p, p, p, p, p)


def _swa_sample_kernel(g_blk, t_len, w_len, sink_ref, q_ref, za_ref, kv_ref, kc_ref, vc_ref,
                       o_ref, kw_ref, vw_ref):
    kpad = 2 * WINDOW - w_len - t_len
    rows = GROUP * t_len
    r = lax.broadcasted_iota(jnp.int32, (rows, 2 * WINDOW), 0)
    j = lax.broadcasted_iota(jnp.int32, (rows, 2 * WINDOW), 1)
    t = r % t_len
    mask = (j <= w_len + t) & (j >= w_len + t - WINDOW) & (j < w_len + t_len)
    rg = lax.broadcasted_iota(jnp.int32, (rows, 1), 0) // t_len
    sink_cols = []
    for h in range(N_KV):
        col = jnp.zeros((rows, 1), F32)
        for g in range(GROUP):
            col = jnp.where(rg == g, sink_ref[h * GROUP + g], col)
        sink_cols.append(col)
    zpad = jnp.zeros((kpad, KV_W), F32)
    q_all = q_ref[...].astype(F32)
    za_all = za_ref[...].astype(F32)
    seq_outs = []
    for s in range(g_blk):
        kc = kc_ref[s]
        vc = vc_ref[s]
        kn = kv_ref[s, :, :KV_W]
        vn = kv_ref[s, :, KV_W:]
        kw_ref[s] = jnp.concatenate([kc[t_len:], kn], axis=0)
        vw_ref[s] = jnp.concatenate([vc[t_len:], vn], axis=0)
        k_all = jnp.concatenate([kc, kn, zpad], axis=0).astype(BF16)
        v_all = jnp.concatenate([vc, vn, zpad], axis=0).astype(BF16)
        q = q_all[s * t_len:(s + 1) * t_len]
        pieces = []
        for h in range(N_KV):
            kh = k_all[:, h * HEAD_DIM:(h + 1) * HEAD_DIM]
            vh = v_all[:, h * HEAD_DIM:(h + 1) * HEAD_DIM]
            qh = jnp.concatenate(
                [q[:, (h * GROUP + g) * HEAD_DIM:(h * GROUP + g + 1) * HEAD_DIM] for g in range(GROUP)],
                axis=0).astype(BF16)
            oh = _sink_attend(qh, kh, vh, mask, sink_cols[h])
            for g in range(GROUP):
                pieces.append(oh[g * t_len:(g + 1) * t_len])
        o = jnp.concatenate(pieces, axis=1)
        seq_outs.append(o * _silu(za_all[s * t_len:(s + 1) * t_len]))
    o_ref[...] = jnp.concatenate(seq_outs, axis=0).astype(BF16)


def _swa_sample(sink, p, row0, n, t_len, kv3, kv_seq0, cache_k, cache_v, g_blk=4):
    rows = g_blk * t_len
    assert rows % BF16_ROWS == 0
    w_len = cache_k.shape[1]
    blk_q = (rows, WIDTH_A)
    blk_c = (g_blk, w_len, KV_W)
    r0 = row0 // rows
    s0 = kv_seq0 // g_blk
    return pl.pallas_call(
        functools.partial(_swa_sample_kernel, g_blk, t_len, w_len),
        out_shape=(jax.ShapeDtypeStruct((n * t_len, WIDTH_A), BF16),
                   jax.ShapeDtypeStruct((n, w_len, KV_W), F32),
                   jax.ShapeDtypeStruct((n, w_len, KV_W), F32)),
        grid=(n // g_blk,),
        in_specs=[
            pl.BlockSpec(memory_space=pltpu.SMEM),
            pl.BlockSpec(blk_q, lambda i: (r0 + i, OFF_QA // WIDTH_A)),
            pl.BlockSpec(blk_q, lambda i: (r0 + i, OFF_ZA // WIDTH_A)),
            pl.BlockSpec((g_blk, t_len, 2 * KV_W), lambda i: (s0 + i, 0, 0)),
            pl.BlockSpec(blk_c, lambda i: (i, 0, 0)),
            pl.BlockSpec(blk_c, lambda i: (i, 0, 0)),
        ],
        out_specs=(pl.BlockSpec(blk_q, lambda i: (i, 0)),
                   pl.BlockSpec(blk_c, lambda i: (i, 0, 0)),
                   pl.BlockSpec(blk_c, lambda i: (i, 0, 0))),
        compiler_params=_cparams(("arbitrary",)),
        name="swa_sample",
    )(sink, p, p, kv3, cache_k, cache_v)


def _split_hi_lo(x):
    hi = x.astype(BF16)
    lo = (x - hi.astype(F32)).astype(BF16)
    return hi, lo


def _cumsum_rows(x):
    c = x.shape[0]
    row = lax.broadcasted_iota(jnp.int32, (c, 1), 0)
    sh = 1
    while sh < c:
        x = x + jnp.where(row >= sh, pltpu.roll(x, sh, 0), 0.0)
        sh *= 2
    return x


def _gla_chunk(q, k, v, la, s_old, ones_cl, causal):
    c = q.shape[0]
    b = _cumsum_rows(la)
    b_last = b[c - 1:c, :]
    la_hi, la_lo = _split_hi_lo(la)
    bl_col = (lax.dot_general(la_hi, ones_cl, TN_DIMS, preferred_element_type=F32)
              + lax.dot_general(la_lo, ones_cl, TN_DIMS, preferred_element_type=F32))
    decay_col = jnp.exp(bl_col)
    q_t = (q * jnp.exp(b) * (DK_B ** -0.5)).astype(BF16)
    k_t = (k * jnp.exp(-b)).astype(BF16)
    k_d = (k * jnp.exp(b_last - b)).astype(BF16)
    outs, states = [], []
    for h in range(N_HEADS_B):
        dk = slice(h * DK_B, (h + 1) * DK_B)
        dv = slice(h * DV_B, (h + 1) * DV_B)
        att = lax.dot_general(q_t[:, dk], k_t[:, dk], NT_DIMS, preferred_element_type=F32)
        att = jnp.where(causal, att, 0.0).astype(BF16)
        outs.append(jnp.dot(att, v[:, dv], preferred_element_type=F32)
                    + jnp.dot(q_t[:, dk], s_old[h].astype(BF16), preferred_element_type=F32))
        decay = jnp.concatenate([decay_col[dk]] * (DV_B // LANE), axis=1)
        states.append(decay * s_old[h] + lax.dot_general(k_d[:, dk], v[:, dv], TN_DIMS,
                                                         preferred_element_type=F32))
    return outs, states


def _log_decay(r, w2, b2):
    z = jnp.dot(r.astype(BF16), w2, preferred_element_type=F32) + b2
    ls = jnp.minimum(z, 0.0) - jnp.log(1.0 + jnp.exp(-jnp.abs(z)))
    return ls / GLA_TAU


def _gla_finish(o, zb, gain):
    on = o * lax.rsqrt(jnp.mean(o * o, axis=-1, keepdims=True) + EPS) * gain
    return on * _silu(zb)


def _chunk_consts(c):
    ri = lax.broadcasted_iota(jnp.int32, (c, c), 0)
    ci = lax.broadcasted_iota(jnp.int32, (c, c), 1)
    return jnp.ones((c, LANE), BF16), ri >= ci


def _gla_prompt_kernel(nb, *refs):
    w2_ref, b2_ref, gain_ref, o_ref, s_ref = refs[5 * nb:]

    @pl.when(pl.program_id(0) == 0)
    def _():
        s_ref[...] = jnp.zeros_like(s_ref)

    ones_cl, causal = _chunk_consts(o_ref.shape[1])
    for bi in range(nb):
        q_ref, k_ref, v_ref, zb_ref, r_ref = refs[5 * bi:5 * bi + 5]
        la = _log_decay(r_ref[...], w2_ref[...], b2_ref[...])
        outs, states = _gla_chunk(q_ref[...].astype(F32), k_ref[...].astype(F32), v_ref[...], la,
                                  [s_ref[bi, h] for h in range(N_HEADS_B)], ones_cl, causal)
        for h in range(N_HEADS_B):
            dv = slice(h * DV_B, (h + 1) * DV_B)
            s_ref[bi, h] = states[h]
            o_ref[bi, :, dv] = _gla_finish(outs[h], zb_ref[:, dv].astype(F32), gain_ref[:, dv]).astype(BF16)


def _gla_prompt(p, r, w2, b2, gain, b, l):
    c = GLA_CHUNK
    nc = l // c
    in_specs, args = [], []
    for bi in range(b):
        for arr, width, off in ((p, KEY_B, OFF_QB), (p, KEY_B, OFF_KB), (p, WIDTH_B, OFF_VB),
                                (p, WIDTH_B, OFF_ZB), (r, LANE, 0)):
            in_specs.append(pl.BlockSpec((c, width), functools.partial(
                lambda ci, row0, col: (row0 + ci, col), row0=bi * nc, col=off // width)))
            args.append(arr)
    in_specs += [pl.BlockSpec((LANE, KEY_B), lambda ci: (0, 0)),
                 pl.BlockSpec((1, KEY_B), lambda ci: (0, 0)),
                 pl.BlockSpec((1, WIDTH_B), lambda ci: (0, 0))]
    return pl.pallas_call(
        functools.partial(_gla_prompt_kernel, b),
        out_shape=(jax.ShapeDtypeStruct((b, l, WIDTH_B), BF16),
                   jax.ShapeDtypeStruct((b, N_HEADS_B, DK_B, DV_B), F32)),
        grid=(nc,),
        in_specs=in_specs,
        out_specs=(pl.BlockSpec((b, c, WIDTH_B), lambda ci: (0, ci, 0)),
                   pl.BlockSpec((b, N_HEADS_B, DK_B, DV_B), lambda ci: (0, 0, 0, 0))),
        compiler_params=_cparams(("arbitrary",)),
        name="gla_prompt",
    )(*args, w2, b2, gain)


def _gla_sample_kernel(g_blk, t_len, q_ref, k_ref, v_ref, zb_ref, r_ref, w2_ref, b2_ref, gain_ref, s0_ref,
                       o_ref, s_ref):
    c = 2 * t_len
    ones_cl, causal = _chunk_consts(c)

    def pad(x):
        return jnp.concatenate([x, jnp.zeros_like(x)], axis=0)

    live = lax.broadcasted_iota(jnp.int32, (c, 1), 0) < t_len
    q_all = q_ref[...].astype(F32)
    k_all = k_ref[...].astype(F32)
    v_all = v_ref[...].astype(F32)
    zb_all = zb_ref[...].astype(F32)
    seq_outs = []
    for s in range(g_blk):
        rs = slice(s * t_len, (s + 1) * t_len)
        la = jnp.where(live, _log_decay(pad(r_ref[s]), w2_ref[...], b2_ref[...]), 0.0)
        outs, states = _gla_chunk(pad(q_all[rs]), pad(k_all[rs]), pad(v_all[rs]).astype(BF16), la,
                                  [s0_ref[s, h] for h in range(N_HEADS_B)], ones_cl, causal)
        fin = []
        for h in range(N_HEADS_B):
            dv = slice(h * DV_B, (h + 1) * DV_B)
            s_ref[s, h] = states[h]
            fin.append(_gla_finish(outs[h][:t_len], zb_all[rs, dv], gain_ref[:, dv]))
        seq_outs.append(jnp.concatenate(fin, axis=1))
    o_ref[...] = jnp.concatenate(seq_outs, axis=0).astype(BF16)


def _gla_sample(p, r3, row0, t_len, w2, b2, gain, state, g_blk=4):
    n = state.shape[0]
    rows = g_blk * t_len
    assert rows % BF16_ROWS == 0
    st_blk = (g_blk, N_HEADS_B, DK_B, DV_B)
    r0 = row0 // rows
    return pl.pallas_call(
        functools.partial(_gla_sample_kernel, g_blk, t_len),
        out_shape=(jax.ShapeDtypeStruct((n * t_len, WIDTH_B), BF16),
                   jax.ShapeDtypeStruct(state.shape, F32)),
        grid=(n // g_blk,),
        in_specs=[
            pl.BlockSpec((rows, KEY_B), lambda i: (r0 + i, OFF_QB // KEY_B)),
            pl.BlockSpec((rows, KEY_B), lambda i: (r0 + i, OFF_KB // KEY_B)),
            pl.BlockSpec((rows, WIDTH_B), lambda i: (r0 + i, OFF_VB // WIDTH_B)),
            pl.BlockSpec((rows, WIDTH_B), lambda i: (r0 + i, OFF_ZB // WIDTH_B)),
            pl.BlockSpec((g_blk, t_len, LANE), lambda i: (r0 + i, 0, 0)),
            pl.BlockSpec((LANE, KEY_B), lambda i: (0, 0)),
            pl.BlockSpec((1, KEY_B), lambda i: (0, 0)),
            pl.BlockSpec((1, WIDTH_B), lambda i: (0, 0)),
            pl.BlockSpec(st_blk, lambda i: (i, 0, 0, 0)),
        ],
        out_specs=(pl.BlockSpec((rows, WIDTH_B), lambda i: (i, 0)),
                   pl.BlockSpec(st_blk, lambda i: (i, 0, 0, 0))),
        compiler_params=_cparams(("arbitrary",)),
        name="gla_sample",
    )(p, p, p, p, r3, w2, b2, gain, state)


def _merge_kernel(nchunk, n_ptiles, ap_ref, bp_ref, as_ref, bs_ref, wa_ref, wb_ref, ga_ref, gb_ref, o_ref):
    cw = o_ref.shape[1] // nchunk
    is_prompt = pl.program_id(0) < n_ptiles
    a = jnp.where(is_prompt, ap_ref[...], as_ref[...])
    b = jnp.where(is_prompt, bp_ref[...], bs_ref[...])
    for c in range(nchunk):
        cs = slice(c * cw, (c + 1) * cw)
        ua = jnp.dot(a, wa_ref[:, cs], preferred_element_type=F32)
        ub = jnp.dot(b, wb_ref[:, cs], preferred_element_type=F32)
        merged = (jax.nn.sigmoid(ga_ref[:, cs].astype(F32)) * ua
                  + jax.nn.sigmoid(gb_ref[:, cs].astype(F32)) * ub)
        o_ref[:, cs] = merged.astype(BF16)


def _merge(ap, bp, a_s, b_s, w_pa, w_pb, p, tm):
    n_pt = ap.shape[0] // tm
    n_st = a_s.shape[0] // tm
    d = D_MODEL
    resident = dict(pipeline_mode=pl.Buffered(1))
    pt = lambda i: (jnp.minimum(i, n_pt - 1), 0)
    st = lambda i: (jnp.maximum(i - n_pt, 0), 0)
    return pl.pallas_call(
        functools.partial(_merge_kernel, 4, n_pt),
        out_shape=jax.ShapeDtypeStruct(((n_pt + n_st) * tm, d), BF16),
        grid=(n_pt + n_st,),
        in_specs=[
            pl.BlockSpec((tm, WIDTH_A), pt),
            pl.BlockSpec((tm, WIDTH_B), pt),
            pl.BlockSpec((tm, WIDTH_A), st),
            pl.BlockSpec((tm, WIDTH_B), st),
            pl.BlockSpec((WIDTH_A, d), lambda i: (0, 0), **resident),
            pl.BlockSpec((WIDTH_B, d), lambda i: (0, 0), **resident),
            pl.BlockSpec((tm, d), lambda i: (i, OFF_GA // d)),
            pl.BlockSpec((tm, d), lambda i: (i, OFF_GB // d)),
        ],
        out_specs=pl.BlockSpec((tm, d), lambda i: (i, 0)),
        compiler_params=_cparams(("arbitrary",), VMEM_LIMIT_RESIDENT),
        name="merge",
    )(ap, bp, a_s, b_s, w_pa, w_pb, p, p)


def _out_kernel(nchunk, m_ref, w_ref, x_ref, gate_ref, fg_ref, o_ref):
    g_blk, r_blk, d = x_ref.shape
    cw = d // nchunk
    ssq = jnp.zeros((g_blk, r_blk, 1), F32)
    for c in range(nchunk):
        cs = slice(c * cw, (c + 1) * cw)
        y = jnp.dot(m_ref[...], w_ref[:, cs], preferred_element_type=F32)
        xn = x_ref[:, :, cs] + gate_ref[:, :, cs] * y.reshape(g_blk, r_blk, cw)
        o_ref[:, :, cs] = xn
        ssq = ssq + jnp.sum(xn * xn, axis=-1, keepdims=True)
    inv = lax.rsqrt(ssq * (1.0 / d) + EPS)
    for c in range(nchunk):
        cs = slice(c * cw, (c + 1) * cw)
        o_ref[:, :, cs] = o_ref[:, :, cs] * inv * fg_ref[:, :, cs]


def _out_proj(merged, row0, w_o, x3, gate3, fin_gain, g_blk, r_blk):
    n, l, d = x3.shape
    rows = g_blk * r_blk
    lb = l // r_blk
    t0 = row0 // rows
    return pl.pallas_call(
        functools.partial(_out_kernel, 4),
        out_shape=jax.ShapeDtypeStruct((n, l, d), F32),
        grid=(n // g_blk, lb),
        in_specs=[
            pl.BlockSpec((rows, d), lambda i, r: (t0 + i * lb + r, 0)),
            pl.BlockSpec((d, d), lambda i, r: (0, 0), pipeline_mode=pl.Buffered(1)),
            pl.BlockSpec((g_blk, r_blk, d), lambda i, r: (i, r, 0)),
            pl.BlockSpec((g_blk, 1, d), lambda i, r: (i, 0, 0)),
            pl.BlockSpec((1, 1, d), lambda i, r: (0, 0, 0)),
        ],
        out_specs=pl.BlockSpec((g_blk, r_blk, d), lambda i, r: (i, r, 0)),
        compiler_params=_cparams(("arbitrary", "arbitrary"), VMEM_LIMIT_RESIDENT),
        name="out_proj",
    )(merged, w_o, x3, gate3, fin_gain)


def kernel(x_prompt, x_sample, cache_k_win, cache_v_win, state_gla, c_prompt, c_sample,
           w_ada, b_ada, norm_gain, w_in, attn_sink, w_alpha2, b_alpha, gla_norm_gain,
           w_proj_a, w_proj_b, w_out, final_norm_gain):
    assert w_ada.shape[0] == 1, "single-layer step"
    bp, lp, d = x_prompt.shape
    ns, ts, _ = x_sample.shape
    w_len = cache_k_win.shape[2]
    rows_p = bp * lp
    rows_s = ns * ts
    rows = rows_p + rows_s

    wt = jnp.transpose(w_in[0])
    w_rt = jnp.pad(wt[SRC_MAIN:SRC_GATES], ((0, LANE - GATE_RANK), (0, 0))).astype(BF16)
    w2 = jnp.pad(w_alpha2[0], ((0, LANE - GATE_RANK), (0, 0))).astype(BF16)
    b2 = b_alpha[0].reshape(1, KEY_B)
    gla_gain = gla_norm_gain[0].reshape(1, WIDTH_B)
    cache_t = lambda c: jnp.transpose(c[0], (0, 2, 3, 1)).reshape(ns, KV_W, w_len)
    gain3 = norm_gain[0].reshape(1, 1, d)
    fin3 = final_norm_gain.reshape(1, 1, d)
    sink = attn_sink[0]

    m_all = ns + bp
    m_pad = -(-m_all // 8) * 8
    c_all = jnp.concatenate([c_sample, c_prompt, jnp.zeros((m_pad - m_all, d), F32)], axis=0)
    mod = _adaln(c_all, w_ada[0], b_ada[0].reshape(1, 3 * d))
    shift, scale, gate = mod[:, :d], mod[:, d:2 * d], mod[:, 2 * d:]
    sl_s, sl_p = slice(0, ns), slice(ns, ns + bp)
    as3 = lambda a, n: a.reshape(n, 1, d)

    h, r = _prologue(x_prompt, x_sample, gain3, as3(scale[sl_p], bp), as3(shift[sl_p], bp),
                     as3(scale[sl_s], ns), as3(shift[sl_s], ns), w_rt, 512)
    p, w_pa, w_pb, w_o, cache_k, cache_v = _in_proj(
        h, wt, PROJ_DEST, PROJ_SRC, rows // 4, PROJ_TN, BF16,
        casts=(w_proj_a[0], w_proj_b[0], w_out[0]),
        xposes=(cache_t(cache_k_win), cache_t(cache_v_win)), n_side=ns)
    kv = _kv_rows(h, wt, bp, lp, rows_s)
    kv_p = kv[:bp * WINDOW].reshape(bp, WINDOW, 2, N_KV, HEAD_DIM)
    k_win_p = kv_p[:, :, 0][None]
    v_win_p = kv_p[:, :, 1][None]
    kv3 = kv.reshape(kv.shape[0] // ts, ts, 2 * KV_W)
    r3 = r.reshape(rows // ts, ts, LANE)

    ua_p = _swa_prompt(sink, p, bp, lp)
    ub_p, s_p = _gla_prompt(p, r, w2, b2, gla_gain, bp, lp)

    ua_s, k_win_s, v_win_s = _swa_sample(sink, p, rows_p, ns, ts, kv3, bp * WINDOW // ts, cache_k, cache_v)
    ub_s, s_s = _gla_sample(p, r3, rows_p, ts, w2, b2, gla_gain, state_gla[0])

    merged = _merge(ua_p, ub_p.reshape(rows_p, WIDTH_B), ua_s, ub_s, w_pa, w_pb, p, 256)
    y_p = _out_proj(merged, 0, w_o, x_prompt, as3(gate[sl_p], bp), fin3, 1, 256)
    y_s = _out_proj(merged, rows_p, w_o, x_sample, as3(gate[sl_s], ns), fin3, 256 // ts, ts)

    return (y_p, y_s, k_win_p, v_win_p, s_p[None],
            k_win_s.reshape(1, ns, w_len, N_KV, HEAD_DIM), v_win_s.reshape(1, ns, w_len, N_KV, HEAD_DIM),
            s_s[None])
```

```python
import functools

import jax
import jax.numpy as jnp
from jax import lax
from jax.experimental import pallas as pl
from jax.experimental.pallas import tpu as pltpu

F32 = jnp.float32
BF16 = jnp.bfloat16

D_MODEL = 4096
WIDTH_A = 2048
HEAD_DIM = 64
N_HEADS_A = 32
N_KV = 4
GROUP = 8
WINDOW = 128
ATTN_BLOCK = 128
WIDTH_B = 2048
N_HEADS_B = 4
DV_B = 512
KEY_B = 1024
DK_B = 256
GATE_RANK = 16
GLA_TAU = 16.0
GLA_CHUNK = 64
EPS = 1e-6
NEG_INF = -1e30
KV_W = N_KV * HEAD_DIM

V7X_VMEM_BYTES = 64 * 1024 * 1024
VMEM_LIMIT = V7X_VMEM_BYTES - 8 * 1024 * 1024
VMEM_LIMIT_RESIDENT = V7X_VMEM_BYTES - 4 * 1024 * 1024
LANE = 128
BF16_ROWS = 16

SRC_KA = WIDTH_A
SRC_MAIN = WIDTH_A + 2 * KV_W + WIDTH_A + 2 * KEY_B + 2 * WIDTH_B
SRC_GATES = SRC_MAIN + GATE_RANK
PROJ_TN = 512
OFF_GA = 0
OFF_GB = 4096
OFF_QA = 8192
OFF_ZA = 10240
OFF_VB = 12288
OFF_ZB = 14336
OFF_QB = 16384
OFF_KB = 17408
OFF_KA = 18432
OFF_VA = 18688
PROJ_COLS = 18944
PROJ_DEST = tuple(16 + t for t in (0, 1, 2, 3, 20, 4, 5, 6, 7, 16, 17, 18, 19, 8, 9, 10, 11, 12, 13, 14, 15)) \
    + tuple(range(16))
PROJ_SRC = tuple(j * PROJ_TN for j in range(21)) + tuple(SRC_GATES + j * PROJ_TN for j in range(16))

NT_DIMS = (((1,), (1,)), ((), ()))
TN_DIMS = (((0,), (0,)), ((), ()))


def _cparams(sem, vmem_limit=VMEM_LIMIT):
    return pltpu.CompilerParams(dimension_semantics=sem, vmem_limit_bytes=vmem_limit)


def _silu(x):
    return x * jax.nn.sigmoid(x)


def _adaln_kernel(c_ref, w_ref, b_ref, o_ref):
    a = _silu(c_ref[...]).astype(BF16)
    w = w_ref[...].astype(BF16)
    o_ref[...] = jnp.dot(a, w, preferred_element_type=F32) + b_ref[...]


def _adaln(c_all, w_ada, b_ada, tn=512):
    m, k = c_all.shape
    n = w_ada.shape[1]
    return pl.pallas_call(
        _adaln_kernel,
        out_shape=jax.ShapeDtypeStruct((m, n), F32),
        grid=(n // tn,),
        in_specs=[
            pl.BlockSpec((m, k), lambda j: (0, 0)),
            pl.BlockSpec((k, tn), lambda j: (0, j)),
            pl.BlockSpec((1, tn), lambda j: (0, j)),
        ],
        out_specs=pl.BlockSpec((m, tn), lambda j: (0, j)),
        compiler_params=_cparams(("arbitrary",)),
        name="adaln",
    )(c_all, w_ada, b_ada)


def _prologue_kernel(n_ptiles, xp_ref, xs_ref, g_ref, scp_ref, shp_ref, scs_ref, shs_ref, wr_ref,
                     h_ref, r_ref):
    i = pl.program_id(0)

    def emit(x, scale, shift):
        ms = jnp.mean(x * x, axis=-1, keepdims=True)
        y = x * lax.rsqrt(ms + EPS) * g_ref[...]
        h = y * (1.0 + scale) + shift
        hb = h.reshape(h.shape[0] * h.shape[1], h.shape[2]).astype(BF16)
        h_ref[...] = hb
        r_ref[...] = lax.dot_general(hb, wr_ref[...], NT_DIMS, preferred_element_type=F32)

    @pl.when(i < n_ptiles)
    def _():
        emit(xp_ref[...], scp_ref[...], shp_ref[...])

    @pl.when(i >= n_ptiles)
    def _():
        emit(xs_ref[...], scs_ref[...], shs_ref[...])


def _prologue(xp, xs, gain, scp, shp, scs, shs, w_rt, rows):
    b, l, d = xp.shape
    n, t, _ = xs.shape
    lb = l // rows
    n_pt = b * lb
    sg = rows // t
    n_st = n // sg
    pt = lambda i: jnp.minimum(i, n_pt - 1)
    st = lambda i: jnp.maximum(i - n_pt, 0)
    return pl.pallas_call(
        functools.partial(_prologue_kernel, n_pt),
        out_shape=(jax.ShapeDtypeStruct((b * l + n * t, d), BF16),
                   jax.ShapeDtypeStruct((b * l + n * t, LANE), F32)),
        grid=(n_pt + n_st,),
        in_specs=[
            pl.BlockSpec((1, rows, d), lambda i: (pt(i) // lb, pt(i) % lb, 0)),
            pl.BlockSpec((sg, t, d), lambda i: (st(i), 0, 0)),
            pl.BlockSpec((1, 1, d), lambda i: (0, 0, 0)),
            pl.BlockSpec((1, 1, d), lambda i: (pt(i) // lb, 0, 0)),
            pl.BlockSpec((1, 1, d), lambda i: (pt(i) // lb, 0, 0)),
            pl.BlockSpec((sg, 1, d), lambda i: (st(i), 0, 0)),
            pl.BlockSpec((sg, 1, d), lambda i: (st(i), 0, 0)),
            pl.BlockSpec((LANE, d), lambda i: (0, 0)),
        ],
        out_specs=(pl.BlockSpec((rows, d), lambda i: (i, 0)),
                   pl.BlockSpec((rows, LANE), lambda i: (i, 0))),
        compiler_params=_cparams(("arbitrary",)),
        name="prologue",
    )(xp, xs, gain, scp, shp, scs, shs, w_rt)


def _inproj_kernel(n_cast, n_xpose, dest_ref, src_ref, a_ref, w_ref, *refs):
    del dest_ref, src_ref
    side_in, o_ref, side_out = refs[:n_cast + n_xpose], refs[n_cast + n_xpose], refs[n_cast + n_xpose + 1:]
    w = w_ref[...].astype(BF16)
    o_ref[...] = lax.dot_general(a_ref[...], w, NT_DIMS, preferred_element_type=F32).astype(o_ref.dtype)

    for src, dst in zip(side_in[:n_cast], side_out[:n_cast]):
        dst[...] = src[...].astype(dst.dtype)
    for src, dst in zip(side_in[n_cast:], side_out[n_cast:]):
        dst[0] = jnp.transpose(src[0])


def _in_proj(a, wt, dest, src, tm, tn, out_dtype, casts, xposes, n_side):
    m, k = a.shape
    nt = len(dest)
    assert (m // tm) * nt >= n_side
    chunk = lambda i, j, d, s: jnp.minimum(i * nt + j, n_side - 1)
    side_in, side_out, side_shapes = [], [], []
    for arr in casts:
        blk = (arr.shape[0] // n_side, arr.shape[1])
        assert blk[0] % BF16_ROWS == 0
        spec = pl.BlockSpec(blk, lambda i, j, d, s: (chunk(i, j, d, s), 0))
        side_in.append(spec)
        side_out.append(spec)
        side_shapes.append(jax.ShapeDtypeStruct(arr.shape, BF16))
    for arr in xposes:
        _, r, c = arr.shape
        side_in.append(pl.BlockSpec((1, r, c), lambda i, j, d, s: (chunk(i, j, d, s), 0, 0)))
        side_out.append(pl.BlockSpec((1, c, r), lambda i, j, d, s: (chunk(i, j, d, s), 0, 0)))
        side_shapes.append(jax.ShapeDtypeStruct((n_side, c, r), arr.dtype))
    return pl.pallas_call(
        functools.partial(_inproj_kernel, len(casts), len(xposes)),
        out_shape=[jax.ShapeDtypeStruct((m, nt * tn), out_dtype)] + side_shapes,
        grid_spec=pltpu.PrefetchScalarGridSpec(
            num_scalar_prefetch=2,
            grid=(m // tm, nt),
            in_specs=[
                pl.BlockSpec((tm, k), lambda i, j, d, s: (i, 0), pipeline_mode=pl.Buffered(1)),
                pl.BlockSpec((pl.Element(tn), pl.Element(k)),
                             lambda i, j, d, s: (pl.multiple_of(s[j], GATE_RANK), 0)),
            ] + side_in,
            out_specs=[pl.BlockSpec((tm, tn), lambda i, j, d, s: (i, d[j]))] + side_out,
        ),
        compiler_params=_cparams(("arbitrary", "arbitrary")),
        name="in_proj",
    )(jnp.asarray(dest, jnp.int32), jnp.asarray(src, jnp.int32), a, wt, *casts, *xposes)


def _kv_rows_kernel(a_ref, w_ref, o_ref):
    w = w_ref[...].astype(BF16)
    o_ref[...] = lax.dot_general(a_ref[...], w, NT_DIMS, preferred_element_type=F32)


def _kv_rows(h, wt, b, l, rows_s):
    k = h.shape[1]
    blk = WINDOW
    lb = l // blk
    n_s = rows_s // blk
    row_blk = lambda t: jnp.where(t < b, (t + 1) * lb - 1, b * lb + t - b)
    return pl.pallas_call(
        _kv_rows_kernel,
        out_shape=jax.ShapeDtypeStruct(((b + n_s) * blk, 2 * KV_W), F32),
        grid=(b + n_s,),
        in_specs=[
            pl.BlockSpec((blk, k), lambda t: (row_blk(t), 0)),
            pl.BlockSpec((2 * KV_W, k), lambda t: (SRC_KA // (2 * KV_W), 0)),
        ],
        out_specs=pl.BlockSpec((blk, 2 * KV_W), lambda t: (t, 0)),
        compiler_params=_cparams(("arbitrary",)),
        name="kv_rows",
    )(h, wt)


def _swa_prompt_kernel(sink_ref, q_ref, za_ref, kp_ref, kc_ref, vp_ref, vc_ref, o_ref):
    n = pl.program_id(1)
    blk = ATTN_BLOCK
    pairs = GROUP // 2
    pw = 2 * HEAD_DIM
    hw = GROUP * HEAD_DIM
    k = jnp.concatenate([kp_ref[...], kc_ref[...]], axis=0)
    v = jnp.concatenate([vp_ref[...], vc_ref[...]], axis=0)
    rows = pairs * blk
    i = lax.broadcasted_iota(jnp.int32, (rows, 2 * blk), 0) % blk
    j = lax.broadcasted_iota(jnp.int32, (rows, 2 * blk), 1)
    visible = (j >= i) & (j <= i + WINDOW) & ((j >= blk) | (n > 0))
    bias = jnp.where(visible, 0.0, NEG_INF)
    pair_id = lax.broadcasted_iota(jnp.int32, (rows, 1), 0) // blk
    zeros = jnp.zeros((2 * blk, HEAD_DIM), BF16)
    for h in range(N_KV):
        kh = k[:, h * HEAD_DIM:(h + 1) * HEAD_DIM] * (HEAD_DIM ** -0.5)
        vh = v[:, h * HEAD_DIM:(h + 1) * HEAD_DIM]
        q2 = jnp.concatenate([q_ref[:, h * hw + jj * pw:h * hw + (jj + 1) * pw] for jj in range(pairs)], axis=0)
        acc = None
        for half in range(2):
            kv_parts = (lambda t: [t, zeros]) if half == 0 else (lambda t: [zeros, t])
            k_pad = jnp.concatenate(kv_parts(kh), axis=1)
            v_pad = jnp.concatenate(kv_parts(vh), axis=1)
            sink = jnp.zeros((rows, 1), F32)
            for jj in range(pairs):
                sink = jnp.where(pair_id == jj, sink_ref[h * GROUP + 2 * jj + half], sink)
            s = lax.dot_general(q2, k_pad, NT_DIMS, preferred_element_type=F32) + bias
            m = jnp.maximum(jnp.max(s, axis=-1, keepdims=True), sink)
            p = jnp.exp(s - m)
            denom = jnp.sum(p, axis=-1, keepdims=True) + jnp.exp(sink - m)
            o = jnp.dot(p.astype(BF16), v_pad, preferred_element_type=F32) / denom
            acc = o if acc is None else acc + o
        o_h = jnp.concatenate([acc[jj * blk:(jj + 1) * blk] for jj in range(pairs)], axis=1)
        za = za_ref[:, h * hw:(h + 1) * hw].astype(F32)
        o_ref[:, h * hw:(h + 1) * hw] = (o_h * _silu(za)).astype(BF16)


def _swa_prompt(sink, p, b, l):
    nb = l // ATTN_BLOCK
    kcol = OFF_KA // KV_W
    vcol = OFF_VA // KV_W
    blk_q = (ATTN_BLOCK, WIDTH_A)
    blk_kv = (ATTN_BLOCK, KV_W)
    cur = lambda bi, n: bi * nb + n
    prev = lambda bi, n: bi * nb + jnp.maximum(n - 1, 0)
    return pl.pallas_call(
        _swa_prompt_kernel,
        out_shape=jax.ShapeDtypeStruct((b * l, WIDTH_A), BF16),
        grid=(b, nb),
        in_specs=[
            pl.BlockSpec(memory_space=pltpu.SMEM),
            pl.BlockSpec(blk_q, lambda bi, n: (cur(bi, n), OFF_QA // WIDTH_A)),
            pl.BlockSpec(blk_q, lambda bi, n: (cur(bi, n), OFF_ZA // WIDTH_A)),
            pl.BlockSpec(blk_kv, lambda bi, n: (prev(bi, n), kcol)),
            pl.BlockSpec(blk_kv, lambda bi, n: (cur(bi, n), kcol)),
            pl.BlockSpec(blk_kv, lambda bi, n: (prev(bi, n), vcol)),
            pl.BlockSpec(blk_kv, lambda bi, n: (cur(bi, n), vcol)),
        ],
        out_specs=pl.BlockSpec(blk_q, lambda bi, n: (cur(bi, n), 0)),
        compiler_params=_cparams(("arbitrary", "arbitrary")),
        name="swa_prompt",
    )(sink, p, p, p, p, p, p)


def _swa_sample_kernel(g_blk, t_len, w_len, sink_ref, q_ref, za_ref, kv_ref, kc_ref, vc_ref,
                       o_ref, kw_ref, vw_ref):
    kpad = 2 * WINDOW - w_len - t_len
    nkeys = 2 * WINDOW
    rows = GROUP * t_len
    r = lax.broadcasted_iota(jnp.int32, (rows, nkeys), 0)
    j = lax.broadcasted_iota(jnp.int32, (rows, nkeys), 1)
    t = r % t_len
    visible = (j <= w_len + t) & (j >= w_len + t - WINDOW)
    bias = jnp.where(visible, 0.0, NEG_INF)[None]
    rg = lax.broadcasted_iota(jnp.int32, (rows, 1), 0) // t_len
    sink_cols = []
    for h in range(N_KV):
        col = jnp.zeros((rows, 1), F32)
        for g in range(GROUP):
            col = jnp.where(rg == g, sink_ref[h * GROUP + g], col)
        sink_cols.append(col)
    sink = jnp.concatenate(sink_cols * g_blk, axis=0).reshape(g_blk * N_KV, rows, 1)
    zpad = jnp.zeros((kpad, KV_W), F32)
    q_all = q_ref[...].astype(F32)
    za_all = za_ref[...].astype(F32)

    scores, values = [], []
    for s in range(g_blk):
        kc = kc_ref[s]
        vc = vc_ref[s]
        kn = kv_ref[s, :, :KV_W]
        vn = kv_ref[s, :, KV_W:]
        kw_ref[s] = jnp.concatenate([kc[t_len:], kn], axis=0)
        vw_ref[s] = jnp.concatenate([vc[t_len:], vn], axis=0)
        k_all = jnp.concatenate([kc, kn, zpad], axis=0).astype(BF16) * (HEAD_DIM ** -0.5)
        v_all = jnp.concatenate([vc, vn, zpad], axis=0).astype(BF16)
        q = q_all[s * t_len:(s + 1) * t_len]
        for h in range(N_KV):
            qh = jnp.concatenate(
                [q[:, (h * GROUP + g) * HEAD_DIM:(h * GROUP + g + 1) * HEAD_DIM] for g in range(GROUP)],
                axis=0).astype(BF16)
            scores.append(lax.dot_general(qh, k_all[:, h * HEAD_DIM:(h + 1) * HEAD_DIM], NT_DIMS,
                                          preferred_element_type=F32))
            values.append(v_all[:, h * HEAD_DIM:(h + 1) * HEAD_DIM])

    sc = jnp.concatenate(scores, axis=0).reshape(g_blk * N_KV, rows, nkeys) + bias
    m = jnp.maximum(jnp.max(sc, axis=-1, keepdims=True), sink)
    p = jnp.exp(sc - m)
    inv = 1.0 / (jnp.sum(p, axis=-1, keepdims=True) + jnp.exp(sink - m))
    pb = p.astype(BF16)

    seq_outs = []
    for s in range(g_blk):
        pieces = []
        for h in range(N_KV):
            c = s * N_KV + h
            oh = jnp.dot(pb[c], values[c], preferred_element_type=F32) * inv[c]
            for g in range(GROUP):
                pieces.append(oh[g * t_len:(g + 1) * t_len])
        o = jnp.concatenate(pieces, axis=1)
        seq_outs.append(o * _silu(za_all[s * t_len:(s + 1) * t_len]))
    o_ref[...] = jnp.concatenate(seq_outs, axis=0).astype(BF16)


def _swa_sample(sink, p, row0, n, t_len, kv3, kv_seq0, cache_k, cache_v, g_blk=8):
    rows = g_blk * t_len
    assert rows % BF16_ROWS == 0
    w_len = cache_k.shape[1]
    blk_q = (rows, WIDTH_A)
    blk_c = (g_blk, w_len, KV_W)
    r0 = row0 // rows
    s0 = kv_seq0 // g_blk
    return pl.pallas_call(
        functools.partial(_swa_sample_kernel, g_blk, t_len, w_len),
        out_shape=(jax.ShapeDtypeStruct((n * t_len, WIDTH_A), BF16),
                   jax.ShapeDtypeStruct((n, w_len, KV_W), F32),
                   jax.ShapeDtypeStruct((n, w_len, KV_W), F32)),
        grid=(n // g_blk,),
        in_specs=[
            pl.BlockSpec(memory_space=pltpu.SMEM),
            pl.BlockSpec(blk_q, lambda i: (r0 + i, OFF_QA // WIDTH_A)),
            pl.BlockSpec(blk_q, lambda i: (r0 + i, OFF_ZA // WIDTH_A)),
            pl.BlockSpec((g_blk, t_len, 2 * KV_W), lambda i: (s0 + i, 0, 0)),
            pl.BlockSpec(blk_c, lambda i: (i, 0, 0)),
            pl.BlockSpec(blk_c, lambda i: (i, 0, 0)),
        ],
        out_specs=(pl.BlockSpec(blk_q, lambda i: (i, 0)),
                   pl.BlockSpec(blk_c, lambda i: (i, 0, 0)),
                   pl.BlockSpec(blk_c, lambda i: (i, 0, 0))),
        compiler_params=_cparams(("arbitrary",)),
        name="swa_sample",
    )(sink, p, p, kv3, cache_k, cache_v)


def _cumsum_rows(x):
    c = x.shape[0]
    row = lax.broadcasted_iota(jnp.int32, (c, 1), 0)
    sh = 1
    while sh < c:
        x = x + jnp.where(row >= sh, pltpu.roll(x, sh, 0), 0.0)
        sh *= 2
    return x


def _gla_chunk(q, k, v, la, s_old, causal):
    c = q.shape[0]
    b = _cumsum_rows(la)
    b_last = b[c - 1:c, :]
    decay_col = jnp.transpose(jnp.broadcast_to(jnp.exp(b_last), (LANE, b_last.shape[1])))
    q_t = (q * jnp.exp(b) * (DK_B ** -0.5)).astype(BF16)
    k_t = (k * jnp.exp(-b)).astype(BF16)
    k_d = (k * jnp.exp(b_last - b)).astype(BF16)
    outs, states = [], []
    for h in range(N_HEADS_B):
        dk = slice(h * DK_B, (h + 1) * DK_B)
        dv = slice(h * DV_B, (h + 1) * DV_B)
        att = lax.dot_general(q_t[:, dk], k_t[:, dk], NT_DIMS, preferred_element_type=F32)
        att = jnp.where(causal, att, 0.0).astype(BF16)
        outs.append(jnp.dot(att, v[:, dv], preferred_element_type=F32)
                    + jnp.dot(q_t[:, dk], s_old[h].astype(BF16), preferred_element_type=F32))
        decay = jnp.concatenate([decay_col[dk]] * (DV_B // LANE), axis=1)
        states.append(decay * s_old[h] + lax.dot_general(k_d[:, dk], v[:, dv], TN_DIMS,
                                                         preferred_element_type=F32))
    return outs, states


def _log_decay(r, w2, b2):
    z = jnp.dot(r.astype(BF16), w2, preferred_element_type=F32) + b2
    ls = jnp.minimum(z, 0.0) - jnp.log(1.0 + jnp.exp(-jnp.abs(z)))
    return ls / GLA_TAU


def _gla_finish(o, zb, gain):
    on = o * lax.rsqrt(jnp.mean(o * o, axis=-1, keepdims=True) + EPS) * gain
    return on * _silu(zb)


def _causal_mask(c):
    ri = lax.broadcasted_iota(jnp.int32, (c, c), 0)
    ci = lax.broadcasted_iota(jnp.int32, (c, c), 1)
    return ri >= ci


def _gla_prompt_kernel(nb, *refs):
    w2_ref, b2_ref, gain_ref, o_ref, s_ref = refs[5 * nb:]

    @pl.when(pl.program_id(0) == 0)
    def _():
        s_ref[...] = jnp.zeros_like(s_ref)

    c = GLA_CHUNK
    causal = _causal_mask(c)
    for bi in range(nb):
        q_ref, k_ref, v_ref, zb_ref, r_ref = refs[5 * bi:5 * bi + 5]
        states = [s_ref[bi, h] for h in range(N_HEADS_B)]
        for sub in range(o_ref.shape[1] // c):
            rs = slice(sub * c, (sub + 1) * c)
            la = _log_decay(r_ref[rs, :], w2_ref[...], b2_ref[...])
            outs, states = _gla_chunk(q_ref[rs, :].astype(F32), k_ref[rs, :].astype(F32), v_ref[rs, :], la,
                                      states, causal)
            for h in range(N_HEADS_B):
                dv = slice(h * DV_B, (h + 1) * DV_B)
                o_ref[bi, rs, dv] = _gla_finish(outs[h], zb_ref[rs, dv].astype(F32), gain_ref[:, dv]).astype(BF16)
        for h in range(N_HEADS_B):
            s_ref[bi, h] = states[h]


def _gla_prompt(p, r, w2, b2, gain, b, l, chunks_per_step=2):
    c = GLA_CHUNK * chunks_per_step
    nc = l // c
    in_specs, args = [], []
    for bi in range(b):
        for arr, width, off in ((p, KEY_B, OFF_QB), (p, KEY_B, OFF_KB), (p, WIDTH_B, OFF_VB),
                                (p, WIDTH_B, OFF_ZB), (r, LANE, 0)):
            in_specs.append(pl.BlockSpec((c, width), functools.partial(
                lambda ci, row0, col: (row0 + ci, col), row0=bi * nc, col=off // width)))
            args.append(arr)
    in_specs += [pl.BlockSpec((LANE, KEY_B), lambda ci: (0, 0)),
                 pl.BlockSpec((1, KEY_B), lambda ci: (0, 0)),
                 pl.BlockSpec((1, WIDTH_B), lambda ci: (0, 0))]
    return pl.pallas_call(
        functools.partial(_gla_prompt_kernel, b),
        out_shape=(jax.ShapeDtypeStruct((b, l, WIDTH_B), BF16),
                   jax.ShapeDtypeStruct((b, N_HEADS_B, DK_B, DV_B), F32)),
        grid=(nc,),
        in_specs=in_specs,
        out_specs=(pl.BlockSpec((b, c, WIDTH_B), lambda ci: (0, ci, 0)),
                   pl.BlockSpec((b, N_HEADS_B, DK_B, DV_B), lambda ci: (0, 0, 0, 0))),
        compiler_params=_cparams(("arbitrary",)),
        name="gla_prompt",
    )(*args, w2, b2, gain)


def _gla_sample_kernel(g_blk, t_len, q_ref, k_ref, v_ref, zb_ref, r_ref, w2_ref, b2_ref, gain_ref, s0_ref,
                       o_ref, s_ref):
    c = 2 * t_len
    causal = _causal_mask(c)

    def pad(x):
        return jnp.concatenate([x, jnp.zeros_like(x)], axis=0)

    live = lax.broadcasted_iota(jnp.int32, (c, 1), 0) < t_len
    q_all = q_ref[...].astype(F32)
    k_all = k_ref[...].astype(F32)
    v_all = v_ref[...].astype(F32)
    zb_all = zb_ref[...].astype(F32)
    seq_outs = []
    for s in range(g_blk):
        rs = slice(s * t_len, (s + 1) * t_len)
        la = jnp.where(live, _log_decay(pad(r_ref[s]), w2_ref[...], b2_ref[...]), 0.0)
        outs, states = _gla_chunk(pad(q_all[rs]), pad(k_all[rs]), pad(v_all[rs]).astype(BF16), la,
                                  [s0_ref[s, h] for h in range(N_HEADS_B)], causal)
        fin = []
        for h in range(N_HEADS_B):
            dv = slice(h * DV_B, (h + 1) * DV_B)
            s_ref[s, h] = states[h]
            fin.append(_gla_finish(outs[h][:t_len], zb_all[rs, dv], gain_ref[:, dv]))
        seq_outs.append(jnp.concatenate(fin, axis=1))
    o_ref[...] = jnp.concatenate(seq_outs, axis=0).astype(BF16)


def _gla_sample(p, r3, row0, t_len, w2, b2, gain, state, g_blk=4):
    n = state.shape[0]
    rows = g_blk * t_len
    assert rows % BF16_ROWS == 0
    st_blk = (g_blk, N_HEADS_B, DK_B, DV_B)
    r0 = row0 // rows
    return pl.pallas_call(
        functools.partial(_gla_sample_kernel, g_blk, t_len),
        out_shape=(jax.ShapeDtypeStruct((n * t_len, WIDTH_B), BF16),
                   jax.ShapeDtypeStruct(state.shape, F32)),
        grid=(n // g_blk,),
        in_specs=[
            pl.BlockSpec((rows, KEY_B), lambda i: (r0 + i, OFF_QB // KEY_B)),
            pl.BlockSpec((rows, KEY_B), lambda i: (r0 + i, OFF_KB // KEY_B)),
            pl.BlockSpec((rows, WIDTH_B), lambda i: (r0 + i, OFF_VB // WIDTH_B)),
            pl.BlockSpec((rows, WIDTH_B), lambda i: (r0 + i, OFF_ZB // WIDTH_B)),
            pl.BlockSpec((g_blk, t_len, LANE), lambda i: (r0 + i, 0, 0)),
            pl.BlockSpec((LANE, KEY_B), lambda i: (0, 0)),
            pl.BlockSpec((1, KEY_B), lambda i: (0, 0)),
            pl.BlockSpec((1, WIDTH_B), lambda i: (0, 0)),
            pl.BlockSpec(st_blk, lambda i: (i, 0, 0, 0)),
        ],
        out_specs=(pl.BlockSpec((rows, WIDTH_B), lambda i: (i, 0)),
                   pl.BlockSpec(st_blk, lambda i: (i, 0, 0, 0))),
        compiler_params=_cparams(("arbitrary",)),
        name="gla_sample",
    )(p, p, p, p, r3, w2, b2, gain, state)


def _merge_kernel(nchunk, n_ptiles, ap_ref, bp_ref, as_ref, bs_ref, wa_ref, wb_ref, ga_ref, gb_ref, o_ref):
    cw = o_ref.shape[1] // nchunk
    is_prompt = pl.program_id(0) < n_ptiles
    a = jnp.where(is_prompt, ap_ref[...], as_ref[...])
    b = jnp.where(is_prompt, bp_ref[...], bs_ref[...])
    for c in range(nchunk):
        cs = slice(c * cw, (c + 1) * cw)
        ua = jnp.dot(a, wa_ref[:, cs], preferred_element_type=F32)
        ub = jnp.dot(b, wb_ref[:, cs], preferred_element_type=F32)
        merged = (jax.nn.sigmoid(ga_ref[:, cs].astype(F32)) * ua
                  + jax.nn.sigmoid(gb_ref[:, cs].astype(F32)) * ub)
        o_ref[:, cs] = merged.astype(BF16)


def _merge(ap, bp, a_s, b_s, w_pa, w_pb, p, tm):
    n_pt = ap.shape[0] // tm
    n_st = a_s.shape[0] // tm
    d = D_MODEL
    resident = dict(pipeline_mode=pl.Buffered(1))
    pt = lambda i: (jnp.minimum(i, n_pt - 1), 0)
    st = lambda i: (jnp.maximum(i - n_pt, 0), 0)
    return pl.pallas_call(
        functools.partial(_merge_kernel, 4, n_pt),
        out_shape=jax.ShapeDtypeStruct(((n_pt + n_st) * tm, d), BF16),
        grid=(n_pt + n_st,),
        in_specs=[
            pl.BlockSpec((tm, WIDTH_A), pt),
            pl.BlockSpec((tm, WIDTH_B), pt),
            pl.BlockSpec((tm, WIDTH_A), st),
            pl.BlockSpec((tm, WIDTH_B), st),
            pl.BlockSpec((WIDTH_A, d), lambda i: (0, 0), **resident),
            pl.BlockSpec((WIDTH_B, d), lambda i: (0, 0), **resident),
            pl.BlockSpec((tm, d), lambda i: (i, OFF_GA // d)),
            pl.BlockSpec((tm, d), lambda i: (i, OFF_GB // d)),
        ],
        out_specs=pl.BlockSpec((tm, d), lambda i: (i, 0)),
        compiler_params=_cparams(("arbitrary",), VMEM_LIMIT_RESIDENT),
        name="merge",
    )(ap, bp, a_s, b_s, w_pa, w_pb, p, p)


def _out_kernel(nchunk, m_ref, w_ref, x_ref, gate_ref, fg_ref, o_ref):
    g_blk, r_blk, d = x_ref.shape
    cw = d // nchunk
    ssq = jnp.zeros((g_blk, r_blk, 1), F32)
    for c in range(nchunk):
        cs = slice(c * cw, (c + 1) * cw)
        y = jnp.dot(m_ref[...], w_ref[:, cs], preferred_element_type=F32)
        xn = x_ref[:, :, cs] + gate_ref[:, :, cs] * y.reshape(g_blk, r_blk, cw)
        o_ref[:, :, cs] = xn
        ssq = ssq + jnp.sum(xn * xn, axis=-1, keepdims=True)
    inv = lax.rsqrt(ssq * (1.0 / d) + EPS)
    for c in range(nchunk):
        cs = slice(c * cw, (c + 1) * cw)
        o_ref[:, :, cs] = o_ref[:, :, cs] * inv * fg_ref[:, :, cs]


def _out_proj(merged, row0, w_o, x3, gate3, fin_gain, g_blk, r_blk):
    n, l, d = x3.shape
    rows = g_blk * r_blk
    lb = l // r_blk
    t0 = row0 // rows
    return pl.pallas_call(
        functools.partial(_out_kernel, 4),
        out_shape=jax.ShapeDtypeStruct((n, l, d), F32),
        grid=(n // g_blk, lb),
        in_specs=[
            pl.BlockSpec((rows, d), lambda i, r: (t0 + i * lb + r, 0)),
            pl.BlockSpec((d, d), lambda i, r: (0, 0), pipeline_mode=pl.Buffered(1)),
            pl.BlockSpec((g_blk, r_blk, d), lambda i, r: (i, r, 0)),
            pl.BlockSpec((g_blk, 1, d), lambda i, r: (i, 0, 0)),
            pl.BlockSpec((1, 1, d), lambda i, r: (0, 0, 0)),
        ],
        out_specs=pl.BlockSpec((g_blk, r_blk, d), lambda i, r: (i, r, 0)),
        compiler_params=_cparams(("arbitrary", "arbitrary"), VMEM_LIMIT_RESIDENT),
        name="out_proj",
    )(merged, w_o, x3, gate3, fin_gain)


def kernel(x_prompt, x_sample, cache_k_win, cache_v_win, state_gla, c_prompt, c_sample,
           w_ada, b_ada, norm_gain, w_in, attn_sink, w_alpha2, b_alpha, gla_norm_gain,
           w_proj_a, w_proj_b, w_out, final_norm_gain):
    assert w_ada.shape[0] == 1, "single-layer step"
    bp, lp, d = x_prompt.shape
    ns, ts, _ = x_sample.shape
    w_len = cache_k_win.shape[2]
    rows_p = bp * lp
    rows_s = ns * ts
    rows = rows_p + rows_s

    wt = jnp.transpose(w_in[0])
    w_rt = jnp.pad(wt[SRC_MAIN:SRC_GATES], ((0, LANE - GATE_RANK), (0, 0))).astype(BF16)
    w2 = jnp.pad(w_alpha2[0], ((0, LANE - GATE_RANK), (0, 0))).astype(BF16)
    b2 = b_alpha[0].reshape(1, KEY_B)
    gla_gain = gla_norm_gain[0].reshape(1, WIDTH_B)
    cache_t = lambda c: jnp.transpose(c[0], (0, 2, 3, 1)).reshape(ns, KV_W, w_len)
    gain3 = norm_gain[0].reshape(1, 1, d)
    fin3 = final_norm_gain.reshape(1, 1, d)
    sink = attn_sink[0]

    m_all = ns + bp
    m_pad = -(-m_all // 8) * 8
    c_all = jnp.concatenate([c_sample, c_prompt, jnp.zeros((m_pad - m_all, d), F32)], axis=0)
    mod = _adaln(c_all, w_ada[0], b_ada[0].reshape(1, 3 * d))
    shift, scale, gate = mod[:, :d], mod[:, d:2 * d], mod[:, 2 * d:]
    sl_s, sl_p = slice(0, ns), slice(ns, ns + bp)
    as3 = lambda a, n: a.reshape(n, 1, d)

    h, r = _prologue(x_prompt, x_sample, gain3, as3(scale[sl_p], bp), as3(shift[sl_p], bp),
                     as3(scale[sl_s], ns), as3(shift[sl_s], ns), w_rt, 512)
    p, w_pa, w_pb, w_o, cache_k, cache_v = _in_proj(
        h, wt, PROJ_DEST, PROJ_SRC, rows // 4, PROJ_TN, BF16,
        casts=(w_proj_a[0], w_proj_b[0], w_out[0]),
        xposes=(cache_t(cache_k_win), cache_t(cache_v_win)), n_side=ns)
    kv = _kv_rows(h, wt, bp, lp, rows_s)
    kv_p = kv[:bp * WINDOW].reshape(bp, WINDOW, 2, N_KV, HEAD_DIM)
    k_win_p = kv_p[:, :, 0][None]
    v_win_p = kv_p[:, :, 1][None]
    kv3 = kv.reshape(kv.shape[0] // ts, ts, 2 * KV_W)
    r3 = r.reshape(rows // ts, ts, LANE)

    ua_p = _swa_prompt(sink, p, bp, lp)
    ub_p, s_p = _gla_prompt(p, r, w2, b2, gla_gain, bp, lp)

    ua_s, k_win_s, v_win_s = _swa_sample(sink, p, rows_p, ns, ts, kv3, bp * WINDOW // ts, cache_k, cache_v)
    ub_s, s_s = _gla_sample(p, r3, rows_p, ts, w2, b2, gla_gain, state_gla[0])

    merged = _merge(ua_p, ub_p.reshape(rows_p, WIDTH_B), ua_s, ub_s, w_pa, w_pb, p, 256)
    y_p = _out_proj(merged, 0, w_o, x_prompt, as3(gate[sl_p], bp), fin3, 1, 256)
    y_s = _out_proj(merged, rows_p, w_o, x_sample, as3(gate[sl_s], ns), fin3, 256 // ts, ts)

    return (y_p, y_s, k_win_p, v_win_p, s_p[None],
            k_win_s.reshape(1, ns, w_len, N_KV, HEAD_DIM), v_win_s.reshape(1, ns, w_len, N_KV, HEAD_DIM),
            s_s[None])
```

```python
import functools

import jax
import jax.numpy as jnp
from jax import lax
from jax.experimental import pallas as pl
from jax.experimental.pallas import tpu as pltpu

F32 = jnp.float32
BF16 = jnp.bfloat16

D_MODEL = 4096
WIDTH_A = 2048
HEAD_DIM = 64
N_HEADS_A = 32
N_KV = 4
GROUP = 8
WINDOW = 128
ATTN_BLOCK = 128
WIDTH_B = 2048
N_HEADS_B = 4
DV_B = 512
KEY_B = 1024
DK_B = 256
GATE_RANK = 16
GLA_TAU = 16.0
GLA_CHUNK = 64
EPS = 1e-6
NEG_INF = -1e30
KV_W = N_KV * HEAD_DIM

V7X_VMEM_BYTES = 64 * 1024 * 1024
VMEM_LIMIT = V7X_VMEM_BYTES - 8 * 1024 * 1024
VMEM_LIMIT_RESIDENT = V7X_VMEM_BYTES - 4 * 1024 * 1024
LANE = 128
BF16_ROWS = 16

SRC_KA = WIDTH_A
SRC_MAIN = WIDTH_A + 2 * KV_W + WIDTH_A + 2 * KEY_B + 2 * WIDTH_B
SRC_GATES = SRC_MAIN + GATE_RANK
PROJ_TN = 512
OFF_GA = 0
OFF_GB = 4096
OFF_QA = 8192
OFF_ZA = 10240
OFF_VB = 12288
OFF_ZB = 14336
OFF_QB = 16384
OFF_KB = 17408
OFF_KA = 18432
OFF_VA = 18688
PROJ_COLS = 18944
_PROJ_TILES = ([(j * PROJ_TN, 16 + t) for j, t in zip((0, 1, 2, 3, 5, 6, 7, 8, 9, 10, 11, 12, 13, 14, 15, 16,
                                                        17, 18, 19, 20),
                                                       (0, 1, 2, 3, 4, 5, 6, 7, 16, 17, 18, 19, 8, 9, 10, 11,
                                                        12, 13, 14, 15))]
               + [(SRC_GATES + j * PROJ_TN, j) for j in range(16)]
               + [(SRC_KA, 16 + 20)])
PROJ_SRC = tuple(s for s, _ in _PROJ_TILES)
PROJ_DEST = tuple(d for _, d in _PROJ_TILES)

NT_DIMS = (((1,), (1,)), ((), ()))
TN_DIMS = (((0,), (0,)), ((), ()))


def _cparams(sem, vmem_limit=VMEM_LIMIT):
    return pltpu.CompilerParams(dimension_semantics=sem, vmem_limit_bytes=vmem_limit)


def _silu(x):
    return x * jax.nn.sigmoid(x)


def _adaln_kernel(c_ref, w_ref, b_ref, o_ref):
    a = _silu(c_ref[...]).astype(BF16)
    w = w_ref[...].astype(BF16)
    o_ref[...] = jnp.dot(a, w, preferred_element_type=F32) + b_ref[...]


def _adaln(c_all, w_ada, b_ada, tn=512):
    m, k = c_all.shape
    n = w_ada.shape[1]
    return pl.pallas_call(
        _adaln_kernel,
        out_shape=jax.ShapeDtypeStruct((m, n), F32),
        grid=(n // tn,),
        in_specs=[
            pl.BlockSpec((m, k), lambda j: (0, 0)),
            pl.BlockSpec((k, tn), lambda j: (0, j)),
            pl.BlockSpec((1, tn), lambda j: (0, j)),
        ],
        out_specs=pl.BlockSpec((m, tn), lambda j: (0, j)),
        compiler_params=_cparams(("arbitrary",)),
        name="adaln",
    )(c_all, w_ada, b_ada)


def _prologue_kernel(n_ptiles, xp_ref, xs_ref, g_ref, scp_ref, shp_ref, scs_ref, shs_ref, wr_ref,
                     h_ref, r_ref):
    i = pl.program_id(0)

    def emit(x, scale, shift):
        ms = jnp.mean(x * x, axis=-1, keepdims=True)
        y = x * lax.rsqrt(ms + EPS) * g_ref[...]
        h = y * (1.0 + scale) + shift
        hb = h.reshape(h.shape[0] * h.shape[1], h.shape[2]).astype(BF16)
        h_ref[...] = hb
        r_ref[...] = lax.dot_general(hb, wr_ref[...], NT_DIMS, preferred_element_type=F32)

    @pl.when(i < n_ptiles)
    def _():
        emit(xp_ref[...], scp_ref[...], shp_ref[...])

    @pl.when(i >= n_ptiles)
    def _():
        emit(xs_ref[...], scs_ref[...], shs_ref[...])


def _prologue(xp, xs, gain, scp, shp, scs, shs, w_rt, rows):
    b, l, d = xp.shape
    n, t, _ = xs.shape
    lb = l // rows
    n_pt = b * lb
    sg = rows // t
    n_st = n // sg
    pt = lambda i: jnp.minimum(i, n_pt - 1)
    st = lambda i: jnp.maximum(i - n_pt, 0)
    return pl.pallas_call(
        functools.partial(_prologue_kernel, n_pt),
        out_shape=(jax.ShapeDtypeStruct((b * l + n * t, d), BF16),
                   jax.ShapeDtypeStruct((b * l + n * t, LANE), F32)),
        grid=(n_pt + n_st,),
        in_specs=[
            pl.BlockSpec((1, rows, d), lambda i: (pt(i) // lb, pt(i) % lb, 0)),
            pl.BlockSpec((sg, t, d), lambda i: (st(i), 0, 0)),
            pl.BlockSpec((1, 1, d), lambda i: (0, 0, 0)),
            pl.BlockSpec((1, 1, d), lambda i: (pt(i) // lb, 0, 0)),
            pl.BlockSpec((1, 1, d), lambda i: (pt(i) // lb, 0, 0)),
            pl.BlockSpec((sg, 1, d), lambda i: (st(i), 0, 0)),
            pl.BlockSpec((sg, 1, d), lambda i: (st(i), 0, 0)),
            pl.BlockSpec((LANE, d), lambda i: (0, 0)),
        ],
        out_specs=(pl.BlockSpec((rows, d), lambda i: (i, 0)),
                   pl.BlockSpec((rows, LANE), lambda i: (i, 0))),
        compiler_params=_cparams(("arbitrary",)),
        name="prologue",
    )(xp, xs, gain, scp, shp, scs, shs, w_rt)


def _inproj_kernel(n_cast, n_xpose, dest_ref, src_ref, a_ref, w_ref, *refs):
    del dest_ref, src_ref
    n_in = n_cast + n_xpose
    side_in, o_ref, last_ref, side_out = refs[:n_in], refs[n_in], refs[n_in + 1], refs[n_in + 2:]
    w = w_ref[...].astype(BF16)
    acc = lax.dot_general(a_ref[...], w, NT_DIMS, preferred_element_type=F32)
    o_ref[...] = acc.astype(o_ref.dtype)
    last_ref[...] = acc

    for src, dst in zip(side_in[:n_cast], side_out[:n_cast]):
        dst[...] = src[...].astype(dst.dtype)
    for src, dst in zip(side_in[n_cast:], side_out[n_cast:]):
        dst[0] = jnp.transpose(src[0])


def _in_proj(a, wt, dest, src, tm, tn, out_dtype, casts, xposes, n_side):
    m, k = a.shape
    nt = len(dest)
    assert (m // tm) * nt >= n_side
    chunk = lambda i, j, d, s: jnp.minimum(i * nt + j, n_side - 1)
    side_in, side_out, side_shapes = [], [], []
    for arr in casts:
        blk = (arr.shape[0] // n_side, arr.shape[1])
        assert blk[0] % BF16_ROWS == 0
        spec = pl.BlockSpec(blk, lambda i, j, d, s: (chunk(i, j, d, s), 0))
        side_in.append(spec)
        side_out.append(spec)
        side_shapes.append(jax.ShapeDtypeStruct(arr.shape, BF16))
    for arr in xposes:
        _, r, c = arr.shape
        side_in.append(pl.BlockSpec((1, r, c), lambda i, j, d, s: (chunk(i, j, d, s), 0, 0)))
        side_out.append(pl.BlockSpec((1, c, r), lambda i, j, d, s: (chunk(i, j, d, s), 0, 0)))
        side_shapes.append(jax.ShapeDtypeStruct((n_side, c, r), arr.dtype))
    return pl.pallas_call(
        functools.partial(_inproj_kernel, len(casts), len(xposes)),
        out_shape=[jax.ShapeDtypeStruct((m, nt * tn), out_dtype), jax.ShapeDtypeStruct((m, tn), F32)] + side_shapes,
        grid_spec=pltpu.PrefetchScalarGridSpec(
            num_scalar_prefetch=2,
            grid=(m // tm, nt),
            in_specs=[
                pl.BlockSpec((tm, k), lambda i, j, d, s: (i, 0), pipeline_mode=pl.Buffered(1)),
                pl.BlockSpec((pl.Element(tn), pl.Element(k)),
                             lambda i, j, d, s: (pl.multiple_of(s[j], GATE_RANK), 0)),
            ] + side_in,
            out_specs=[pl.BlockSpec((tm, tn), lambda i, j, d, s: (i, d[j])),
                       pl.BlockSpec((tm, tn), lambda i, j, d, s: (i, 0))] + side_out,
        ),
        compiler_params=_cparams(("arbitrary", "arbitrary"), VMEM_LIMIT_RESIDENT),
        name="in_proj",
    )(jnp.asarray(dest, jnp.int32), jnp.asarray(src, jnp.int32), a, wt, *casts, *xposes)


def _swa_prompt_kernel(sink_ref, q_ref, za_ref, kp_ref, kc_ref, vp_ref, vc_ref, o_ref):
    n = pl.program_id(1)
    blk = ATTN_BLOCK
    pairs = GROUP // 2
    pw = 2 * HEAD_DIM
    hw = GROUP * HEAD_DIM
    k = jnp.concatenate([kp_ref[...], kc_ref[...]], axis=0)
    v = jnp.concatenate([vp_ref[...], vc_ref[...]], axis=0)
    rows = pairs * blk
    i = lax.broadcasted_iota(jnp.int32, (rows, 2 * blk), 0) % blk
    j = lax.broadcasted_iota(jnp.int32, (rows, 2 * blk), 1)
    visible = (j >= i) & (j <= i + WINDOW) & ((j >= blk) | (n > 0))
    bias = jnp.where(visible, 0.0, NEG_INF)
    pair_id = lax.broadcasted_iota(jnp.int32, (rows, 1), 0) // blk
    zeros = jnp.zeros((2 * blk, HEAD_DIM), BF16)
    for h in range(N_KV):
        kh = k[:, h * HEAD_DIM:(h + 1) * HEAD_DIM] * (HEAD_DIM ** -0.5)
        vh = v[:, h * HEAD_DIM:(h + 1) * HEAD_DIM]
        q2 = jnp.concatenate([q_ref[:, h * hw + jj * pw:h * hw + (jj + 1) * pw] for jj in range(pairs)], axis=0)
        acc = None
        for half in range(2):
            kv_parts = (lambda t: [t, zeros]) if half == 0 else (lambda t: [zeros, t])
            k_pad = jnp.concatenate(kv_parts(kh), axis=1)
            v_pad = jnp.concatenate(kv_parts(vh), axis=1)
            sink = jnp.zeros((rows, 1), F32)
            for jj in range(pairs):
                sink = jnp.where(pair_id == jj, sink_ref[h * GROUP + 2 * jj + half], sink)
            s = lax.dot_general(q2, k_pad, NT_DIMS, preferred_element_type=F32) + bias
            m = jnp.maximum(jnp.max(s, axis=-1, keepdims=True), sink)
            p = jnp.exp(s - m)
            denom = jnp.sum(p, axis=-1, keepdims=True) + jnp.exp(sink - m)
            o = jnp.dot(p.astype(BF16), v_pad, preferred_element_type=F32) / denom
            acc = o if acc is None else acc + o
        o_h = jnp.concatenate([acc[jj * blk:(jj + 1) * blk] for jj in range(pairs)], axis=1)
        za = za_ref[:, h * hw:(h + 1) * hw].astype(F32)
        o_ref[:, h * hw:(h + 1) * hw] = (o_h * _silu(za)).astype(BF16)


def _swa_prompt(sink, p, b, l):
    nb = l // ATTN_BLOCK
    kcol = OFF_KA // KV_W
    vcol = OFF_VA // KV_W
    blk_q = (ATTN_BLOCK, WIDTH_A)
    blk_kv = (ATTN_BLOCK, KV_W)
    cur = lambda bi, n: bi * nb + n
    prev = lambda bi, n: bi * nb + jnp.maximum(n - 1, 0)
    return pl.pallas_call(
        _swa_prompt_kernel,
        out_shape=jax.ShapeDtypeStruct((b * l, WIDTH_A), BF16),
        grid=(b, nb),
        in_specs=[
            pl.BlockSpec(memory_space=pltpu.SMEM),
            pl.BlockSpec(blk_q, lambda bi, n: (cur(bi, n), OFF_QA // WIDTH_A)),
            pl.BlockSpec(blk_q, lambda bi, n: (cur(bi, n), OFF_ZA // WIDTH_A)),
            pl.BlockSpec(blk_kv, lambda bi, n: (prev(bi, n), kcol)),
            pl.BlockSpec(blk_kv, lambda bi, n: (cur(bi, n), kcol)),
            pl.BlockSpec(blk_kv, lambda bi, n: (prev(bi, n), vcol)),
            pl.BlockSpec(blk_kv, lambda bi, n: (cur(bi, n), vcol)),
        ],
        out_specs=pl.BlockSpec(blk_q, lambda bi, n: (cur(bi, n), 0)),
        compiler_params=_cparams(("arbitrary", "arbitrary")),
        name="swa_prompt",
    )(sink, p, p, p, p, p, p)


def _swa_sample_kernel(g_blk, t_len, w_len, sink_ref, q_ref, za_ref, kv_ref, kc_ref, vc_ref,
                       o_ref, kw_ref, vw_ref):
    kpad = 2 * WINDOW - w_len - t_len
    nkeys = 2 * WINDOW
    rows = GROUP * t_len
    r = lax.broadcasted_iota(jnp.int32, (rows, nkeys), 0)
    j = lax.broadcasted_iota(jnp.int32, (rows, nkeys), 1)
    t = r % t_len
    visible = (j <= w_len + t) & (j >= w_len + t - WINDOW)
    bias = jnp.where(visible, 0.0, NEG_INF)[None]
    rg = lax.broadcasted_iota(jnp.int32, (rows, 1), 0) // t_len
    sink_cols = []
    for h in range(N_KV):
        col = jnp.zeros((rows, 1), F32)
        for g in range(GROUP):
            col = jnp.where(rg == g, sink_ref[h * GROUP + g], col)
        sink_cols.append(col)
    sink = jnp.concatenate(sink_cols * g_blk, axis=0).reshape(g_blk * N_KV, rows, 1)
    zpad = jnp.zeros((kpad, KV_W), F32)
    q_all = q_ref[...].astype(F32)
    za_all = za_ref[...].astype(F32)

    scores, values = [], []
    for s in range(g_blk):
        kc = kc_ref[s]
        vc = vc_ref[s]
        kn = kv_ref[s, :, :KV_W]
        vn = kv_ref[s, :, KV_W:]
        kw_ref[s] = jnp.concatenate([kc[t_len:], kn], axis=0)
        vw_ref[s] = jnp.concatenate([vc[t_len:], vn], axis=0)
        k_all = jnp.concatenate([kc, kn, zpad], axis=0).astype(BF16) * (HEAD_DIM ** -0.5)
        v_all = jnp.concatenate([vc, vn, zpad], axis=0).astype(BF16)
        q = q_all[s * t_len:(s + 1) * t_len]
        for h in range(N_KV):
            qh = jnp.concatenate(
                [q[:, (h * GROUP + g) * HEAD_DIM:(h * GROUP + g + 1) * HEAD_DIM] for g in range(GROUP)],
                axis=0).astype(BF16)
            scores.append(lax.dot_general(qh, k_all[:, h * HEAD_DIM:(h + 1) * HEAD_DIM], NT_DIMS,
                                          preferred_element_type=F32))
            values.append(v_all[:, h * HEAD_DIM:(h + 1) * HEAD_DIM])

    sc = jnp.concatenate(scores, axis=0).reshape(g_blk * N_KV, rows, nkeys) + bias
    m = jnp.maximum(jnp.max(sc, axis=-1, keepdims=True), sink)
    p = jnp.exp(sc - m)
    inv = 1.0 / (jnp.sum(p, axis=-1, keepdims=True) + jnp.exp(sink - m))
    pb = p.astype(BF16)

    seq_outs = []
    for s in range(g_blk):
        pieces = []
        for h in range(N_KV):
            c = s * N_KV + h
            oh = jnp.dot(pb[c], values[c], preferred_element_type=F32) * inv[c]
            for g in range(GROUP):
                pieces.append(oh[g * t_len:(g + 1) * t_len])
        o = jnp.concatenate(pieces, axis=1)
        seq_outs.append(o * _silu(za_all[s * t_len:(s + 1) * t_len]))
    o_ref[...] = jnp.concatenate(seq_outs, axis=0).astype(BF16)


def _swa_sample(sink, p, row0, n, t_len, kv3, kv_seq0, cache_k, cache_v, g_blk=8):
    rows = g_blk * t_len
    assert rows % BF16_ROWS == 0
    w_len = cache_k.shape[1]
    blk_q = (rows, WIDTH_A)
    blk_c = (g_blk, w_len, KV_W)
    r0 = row0 // rows
    s0 = kv_seq0 // g_blk
    return pl.pallas_call(
        functools.partial(_swa_sample_kernel, g_blk, t_len, w_len),
        out_shape=(jax.ShapeDtypeStruct((n * t_len, WIDTH_A), BF16),
                   jax.ShapeDtypeStruct((n, w_len, KV_W), F32),
                   jax.ShapeDtypeStruct((n, w_len, KV_W), F32)),
        grid=(n // g_blk,),
        in_specs=[
            pl.BlockSpec(memory_space=pltpu.SMEM),
            pl.BlockSpec(blk_q, lambda i: (r0 + i, OFF_QA // WIDTH_A)),
            pl.BlockSpec(blk_q, lambda i: (r0 + i, OFF_ZA // WIDTH_A)),
            pl.BlockSpec((g_blk, t_len, 2 * KV_W), lambda i: (s0 + i, 0, 0)),
            pl.BlockSpec(blk_c, lambda i: (i, 0, 0)),
            pl.BlockSpec(blk_c, lambda i: (i, 0, 0)),
        ],
        out_specs=(pl.BlockSpec(blk_q, lambda i: (i, 0)),
                   pl.BlockSpec(blk_c, lambda i: (i, 0, 0)),
                   pl.BlockSpec(blk_c, lambda i: (i, 0, 0))),
        compiler_params=_cparams(("arbitrary",)),
        name="swa_sample",
    )(sink, p, p, kv3, cache_k, cache_v)


def _cumsum_rows(x):
    c = x.shape[0]
    row = lax.broadcasted_iota(jnp.int32, (c, 1), 0)
    sh = 1
    while sh < c:
        x = x + jnp.where(row >= sh, pltpu.roll(x, sh, 0), 0.0)
        sh *= 2
    return x


def _gla_chunk(q, k, v, la, s_old, causal):
    c = q.shape[0]
    b = _cumsum_rows(la)
    b_last = b[c - 1:c, :]
    decay_col = jnp.transpose(jnp.broadcast_to(jnp.exp(b_last), (LANE, b_last.shape[1])))
    q_t = (q * jnp.exp(b) * (DK_B ** -0.5)).astype(BF16)
    k_t = (k * jnp.exp(-b)).astype(BF16)
    k_d = (k * jnp.exp(b_last - b)).astype(BF16)
    outs, states = [], []
    for h in range(N_HEADS_B):
        dk = slice(h * DK_B, (h + 1) * DK_B)
        dv = slice(h * DV_B, (h + 1) * DV_B)
        att = lax.dot_general(q_t[:, dk], k_t[:, dk], NT_DIMS, preferred_element_type=F32)
        att = jnp.where(causal, att, 0.0).astype(BF16)
        outs.append(jnp.dot(att, v[:, dv], preferred_element_type=F32)
                    + jnp.dot(q_t[:, dk], s_old[h].astype(BF16), preferred_element_type=F32))
        decay = jnp.concatenate([decay_col[dk]] * (DV_B // LANE), axis=1)
        states.append(decay * s_old[h] + lax.dot_general(k_d[:, dk], v[:, dv], TN_DIMS,
                                                         preferred_element_type=F32))
    return outs, states


def _log_decay(r, w2, b2):
    z = jnp.dot(r.astype(BF16), w2, preferred_element_type=F32) + b2
    ls = jnp.minimum(z, 0.0) - jnp.log(1.0 + jnp.exp(-jnp.abs(z)))
    return ls / GLA_TAU


def _gla_finish(o, zb, gain):
    on = o * lax.rsqrt(jnp.mean(o * o, axis=-1, keepdims=True) + EPS) * gain
    return on * _silu(zb)


def _causal_mask(c):
    ri = lax.broadcasted_iota(jnp.int32, (c, c), 0)
    ci = lax.broadcasted_iota(jnp.int32, (c, c), 1)
    return ri >= ci


def _gla_prompt_kernel(nb, *refs):
    w2_ref, b2_ref, gain_ref, o_ref, s_ref = refs[5 * nb:]

    @pl.when(pl.program_id(0) == 0)
    def _():
        s_ref[...] = jnp.zeros_like(s_ref)

    c = GLA_CHUNK
    causal = _causal_mask(c)
    for bi in range(nb):
        q_ref, k_ref, v_ref, zb_ref, r_ref = refs[5 * bi:5 * bi + 5]
        states = [s_ref[bi, h] for h in range(N_HEADS_B)]
        for sub in range(o_ref.shape[1] // c):
            rs = slice(sub * c, (sub + 1) * c)
            la = _log_decay(r_ref[rs, :], w2_ref[...], b2_ref[...])
            outs, states = _gla_chunk(q_ref[rs, :].astype(F32), k_ref[rs, :].astype(F32), v_ref[rs, :], la,
                                      states, causal)
            for h in range(N_HEADS_B):
                dv = slice(h * DV_B, (h + 1) * DV_B)
                o_ref[bi, rs, dv] = _gla_finish(outs[h], zb_ref[rs, dv].astype(F32), gain_ref[:, dv]).astype(BF16)
        for h in range(N_HEADS_B):
            s_ref[bi, h] = states[h]


def _gla_prompt(p, r, w2, b2, gain, b, l, chunks_per_step=2):
    c = GLA_CHUNK * chunks_per_step
    nc = l // c
    in_specs, args = [], []
    for bi in range(b):
        for arr, width, off in ((p, KEY_B, OFF_QB), (p, KEY_B, OFF_KB), (p, WIDTH_B, OFF_VB),
                                (p, WIDTH_B, OFF_ZB), (r, LANE, 0)):
            in_specs.append(pl.BlockSpec((c, width), functools.partial(
                lambda ci, row0, col: (row0 + ci, col), row0=bi * nc, col=off // width)))
            args.append(arr)
    in_specs += [pl.BlockSpec((LANE, KEY_B), lambda ci: (0, 0)),
                 pl.BlockSpec((1, KEY_B), lambda ci: (0, 0)),
                 pl.BlockSpec((1, WIDTH_B), lambda ci: (0, 0))]
    return pl.pallas_call(
        functools.partial(_gla_prompt_kernel, b),
        out_shape=(jax.ShapeDtypeStruct((b, l, WIDTH_B), BF16),
                   jax.ShapeDtypeStruct((b, N_HEADS_B, DK_B, DV_B), F32)),
        grid=(nc,),
        in_specs=in_specs,
        out_specs=(pl.BlockSpec((b, c, WIDTH_B), lambda ci: (0, ci, 0)),
                   pl.BlockSpec((b, N_HEADS_B, DK_B, DV_B), lambda ci: (0, 0, 0, 0))),
        compiler_params=_cparams(("arbitrary",)),
        name="gla_prompt",
    )(*args, w2, b2, gain)


def _gla_sample_kernel(g_blk, t_len, q_ref, k_ref, v_ref, zb_ref, r_ref, w2_ref, b2_ref, gain_ref, s0_ref,
                       o_ref, s_ref):
    c = 2 * t_len
    causal = _causal_mask(c)

    def pad(x):
        return jnp.concatenate([x, jnp.zeros_like(x)], axis=0)

    live = lax.broadcasted_iota(jnp.int32, (c, 1), 0) < t_len
    q_all = q_ref[...].astype(F32)
    k_all = k_ref[...].astype(F32)
    v_all = v_ref[...].astype(F32)
    zb_all = zb_ref[...].astype(F32)
    seq_outs = []
    for s in range(g_blk):
        rs = slice(s * t_len, (s + 1) * t_len)
        la = jnp.where(live, _log_decay(pad(r_ref[s]), w2_ref[...], b2_ref[...]), 0.0)
        outs, states = _gla_chunk(pad(q_all[rs]), pad(k_all[rs]), pad(v_all[rs]).astype(BF16), la,
                                  [s0_ref[s, h] for h in range(N_HEADS_B)], causal)
        fin = []
        for h in range(N_HEADS_B):
            dv = slice(h * DV_B, (h + 1) * DV_B)
            s_ref[s, h] = states[h]
            fin.append(_gla_finish(outs[h][:t_len], zb_all[rs, dv], gain_ref[:, dv]))
        seq_outs.append(jnp.concatenate(fin, axis=1))
    o_ref[...] = jnp.concatenate(seq_outs, axis=0).astype(BF16)


def _gla_sample(p, r3, row0, t_len, w2, b2, gain, state, g_blk=4):
    n = state.shape[0]
    rows = g_blk * t_len
    assert rows % BF16_ROWS == 0
    st_blk = (g_blk, N_HEADS_B, DK_B, DV_B)
    r0 = row0 // rows
    return pl.pallas_call(
        functools.partial(_gla_sample_kernel, g_blk, t_len),
        out_shape=(jax.ShapeDtypeStruct((n * t_len, WIDTH_B), BF16),
                   jax.ShapeDtypeStruct(state.shape, F32)),
        grid=(n // g_blk,),
        in_specs=[
            pl.BlockSpec((rows, KEY_B), lambda i: (r0 + i, OFF_QB // KEY_B)),
            pl.BlockSpec((rows, KEY_B), lambda i: (r0 + i, OFF_KB // KEY_B)),
            pl.BlockSpec((rows, WIDTH_B), lambda i: (r0 + i, OFF_VB // WIDTH_B)),
            pl.BlockSpec((rows, WIDTH_B), lambda i: (r0 + i, OFF_ZB // WIDTH_B)),
            pl.BlockSpec((g_blk, t_len, LANE), lambda i: (r0 + i, 0, 0)),
            pl.BlockSpec((LANE, KEY_B), lambda i: (0, 0)),
            pl.BlockSpec((1, KEY_B), lambda i: (0, 0)),
            pl.BlockSpec((1, WIDTH_B), lambda i: (0, 0)),
            pl.BlockSpec(st_blk, lambda i: (i, 0, 0, 0)),
        ],
        out_specs=(pl.BlockSpec((rows, WIDTH_B), lambda i: (i, 0)),
                   pl.BlockSpec(st_blk, lambda i: (i, 0, 0, 0))),
        compiler_params=_cparams(("arbitrary",)),
        name="gla_sample",
    )(p, p, p, p, r3, w2, b2, gain, state)


def _merge_kernel(nchunk, n_ptiles, ap_ref, bp_ref, as_ref, bs_ref, wa_ref, wb_ref, ga_ref, gb_ref, o_ref):
    cw = o_ref.shape[1] // nchunk
    is_prompt = pl.program_id(0) < n_ptiles
    a = jnp.where(is_prompt, ap_ref[...], as_ref[...])
    b = jnp.where(is_prompt, bp_ref[...], bs_ref[...])
    for c in range(nchunk):
        cs = slice(c * cw, (c + 1) * cw)
        ua = jnp.dot(a, wa_ref[:, cs], preferred_element_type=F32)
        ub = jnp.dot(b, wb_ref[:, cs], preferred_element_type=F32)
        merged = (jax.nn.sigmoid(ga_ref[:, cs].astype(F32)) * ua
                  + jax.nn.sigmoid(gb_ref[:, cs].astype(F32)) * ub)
        o_ref[:, cs] = merged.astype(BF16)


def _merge(ap, bp, a_s, b_s, w_pa, w_pb, p, tm):
    n_pt = ap.shape[0] // tm
    n_st = a_s.shape[0] // tm
    d = D_MODEL
    resident = dict(pipeline_mode=pl.Buffered(1))
    pt = lambda i: (jnp.minimum(i, n_pt - 1), 0)
    st = lambda i: (jnp.maximum(i - n_pt, 0), 0)
    return pl.pallas_call(
        functools.partial(_merge_kernel, 4, n_pt),
        out_shape=jax.ShapeDtypeStruct(((n_pt + n_st) * tm, d), BF16),
        grid=(n_pt + n_st,),
        in_specs=[
            pl.BlockSpec((tm, WIDTH_A), pt),
            pl.BlockSpec((tm, WIDTH_B), pt),
            pl.BlockSpec((tm, WIDTH_A), st),
            pl.BlockSpec((tm, WIDTH_B), st),
            pl.BlockSpec((WIDTH_A, d), lambda i: (0, 0), **resident),
            pl.BlockSpec((WIDTH_B, d), lambda i: (0, 0), **resident),
            pl.BlockSpec((tm, d), lambda i: (i, OFF_GA // d)),
            pl.BlockSpec((tm, d), lambda i: (i, OFF_GB // d)),
        ],
        out_specs=pl.BlockSpec((tm, d), lambda i: (i, 0)),
        compiler_params=_cparams(("arbitrary",), VMEM_LIMIT_RESIDENT),
        name="merge",
    )(ap, bp, a_s, b_s, w_pa, w_pb, p, p)


def _out_kernel(nchunk, m_ref, w_ref, x_ref, gate_ref, fg_ref, o_ref):
    g_blk, r_blk, d = x_ref.shape
    cw = d // nchunk
    ssq = jnp.zeros((g_blk, r_blk, 1), F32)
    for c in range(nchunk):
        cs = slice(c * cw, (c + 1) * cw)
        y = jnp.dot(m_ref[...], w_ref[:, cs], preferred_element_type=F32)
        xn = x_ref[:, :, cs] + gate_ref[:, :, cs] * y.reshape(g_blk, r_blk, cw)
        o_ref[:, :, cs] = xn
        ssq = ssq + jnp.sum(xn * xn, axis=-1, keepdims=True)
    inv = lax.rsqrt(ssq * (1.0 / d) + EPS)
    for c in range(nchunk):
        cs = slice(c * cw, (c + 1) * cw)
        o_ref[:, :, cs] = o_ref[:, :, cs] * inv * fg_ref[:, :, cs]


def _out_proj(merged, row0, w_o, x3, gate3, fin_gain, g_blk, r_blk):
    n, l, d = x3.shape
    rows = g_blk * r_blk
    lb = l // r_blk
    t0 = row0 // rows
    return pl.pallas_call(
        functools.partial(_out_kernel, 4),
        out_shape=jax.ShapeDtypeStruct((n, l, d), F32),
        grid=(n // g_blk, lb),
        in_specs=[
            pl.BlockSpec((rows, d), lambda i, r: (t0 + i * lb + r, 0)),
            pl.BlockSpec((d, d), lambda i, r: (0, 0), pipeline_mode=pl.Buffered(1)),
            pl.BlockSpec((g_blk, r_blk, d), lambda i, r: (i, r, 0)),
            pl.BlockSpec((g_blk, 1, d), lambda i, r: (i, 0, 0)),
            pl.BlockSpec((1, 1, d), lambda i, r: (0, 0, 0)),
        ],
        out_specs=pl.BlockSpec((g_blk, r_blk, d), lambda i, r: (i, r, 0)),
        compiler_params=_cparams(("arbitrary", "arbitrary"), VMEM_LIMIT_RESIDENT),
        name="out_proj",
    )(merged, w_o, x3, gate3, fin_gain)


def kernel(x_prompt, x_sample, cache_k_win, cache_v_win, state_gla, c_prompt, c_sample,
           w_ada, b_ada, norm_gain, w_in, attn_sink, w_alpha2, b_alpha, gla_norm_gain,
           w_proj_a, w_proj_b, w_out, final_norm_gain):
    assert w_ada.shape[0] == 1, "single-layer step"
    bp, lp, d = x_prompt.shape
    ns, ts, _ = x_sample.shape
    w_len = cache_k_win.shape[2]
    rows_p = bp * lp
    rows_s = ns * ts
    rows = rows_p + rows_s

    wt = jnp.transpose(w_in[0])
    w_rt = jnp.pad(wt[SRC_MAIN:SRC_GATES], ((0, LANE - GATE_RANK), (0, 0))).astype(BF16)
    w2 = jnp.pad(w_alpha2[0], ((0, LANE - GATE_RANK), (0, 0))).astype(BF16)
    b2 = b_alpha[0].reshape(1, KEY_B)
    gla_gain = gla_norm_gain[0].reshape(1, WIDTH_B)
    cache_t = lambda c: jnp.transpose(c[0], (0, 2, 3, 1)).reshape(ns, KV_W, w_len)
    gain3 = norm_gain[0].reshape(1, 1, d)
    fin3 = final_norm_gain.reshape(1, 1, d)
    sink = attn_sink[0]

    m_all = ns + bp
    m_pad = -(-m_all // 8) * 8
    c_all = jnp.concatenate([c_sample, c_prompt, jnp.zeros((m_pad - m_all, d), F32)], axis=0)
    mod = _adaln(c_all, w_ada[0], b_ada[0].reshape(1, 3 * d))
    shift, scale, gate = mod[:, :d], mod[:, d:2 * d], mod[:, 2 * d:]
    sl_s, sl_p = slice(0, ns), slice(ns, ns + bp)
    as3 = lambda a, n: a.reshape(n, 1, d)

    h, r = _prologue(x_prompt, x_sample, gain3, as3(scale[sl_p], bp), as3(shift[sl_p], bp),
                     as3(scale[sl_s], ns), as3(shift[sl_s], ns), w_rt, 512)
    p, kv, w_pa, w_pb, w_o, cache_k, cache_v = _in_proj(
        h, wt, PROJ_DEST, PROJ_SRC, rows // 4, PROJ_TN, BF16,
        casts=(w_proj_a[0], w_proj_b[0], w_out[0]),
        xposes=(cache_t(cache_k_win), cache_t(cache_v_win)), n_side=ns)
    kv_p = jnp.stack([kv[(bi + 1) * lp - WINDOW:(bi + 1) * lp] for bi in range(bp)])
    kv_p = kv_p.reshape(bp, WINDOW, 2, N_KV, HEAD_DIM)
    k_win_p = kv_p[:, :, 0][None]
    v_win_p = kv_p[:, :, 1][None]
    kv3 = kv.reshape(rows // ts, ts, 2 * KV_W)
    r3 = r.reshape(rows // ts, ts, LANE)

    ua_p = _swa_prompt(sink, p, bp, lp)
    ub_p, s_p = _gla_prompt(p, r, w2, b2, gla_gain, bp, lp)

    ua_s, k_win_s, v_win_s = _swa_sample(sink, p, rows_p, ns, ts, kv3, rows_p // ts, cache_k, cache_v)
    ub_s, s_s = _gla_sample(p, r3, rows_p, ts, w2, b2, gla_gain, state_gla[0])

    merged = _merge(ua_p, ub_p.reshape(rows_p, WIDTH_B), ua_s, ub_s, w_pa, w_pb, p, 256)
    y_p = _out_proj(merged, 0, w_o, x_prompt, as3(gate[sl_p], bp), fin3, 1, 256)
    y_s = _out_proj(merged, rows_p, w_o, x_sample, as3(gate[sl_s], ns), fin3, 256 // ts, ts)

    return (y_p, y_s, k_win_p, v_win_p, s_p[None],
            k_win_s.reshape(1, ns, w_len, N_KV, HEAD_DIM), v_win_s.reshape(1, ns, w_len, N_KV, HEAD_DIM),
            s_s[None])
```

```python
import functools

import jax
import jax.numpy as jnp
from jax import lax
from jax.experimental import pallas as pl
from jax.experimental.pallas import tpu as pltpu

F32 = jnp.float32
BF16 = jnp.bfloat16

D_MODEL = 4096
WIDTH_A = 2048
HEAD_DIM = 64
N_HEADS_A = 32
N_KV = 4
GROUP = 8
WINDOW = 128
ATTN_BLOCK = 128
WIDTH_B = 2048
N_HEADS_B = 4
DV_B = 512
KEY_B = 1024
DK_B = 256
GATE_RANK = 16
GLA_TAU = 16.0
GLA_CHUNK = 64
EPS = 1e-6
NEG_INF = -1e30
KV_W = N_KV * HEAD_DIM

V7X_VMEM_BYTES = 64 * 1024 * 1024
VMEM_LIMIT = V7X_VMEM_BYTES - 8 * 1024 * 1024
VMEM_LIMIT_RESIDENT = V7X_VMEM_BYTES - 4 * 1024 * 1024
LANE = 128
BF16_ROWS = 16

SRC_KA = WIDTH_A
SRC_MAIN = WIDTH_A + 2 * KV_W + WIDTH_A + 2 * KEY_B + 2 * WIDTH_B
SRC_GATES = SRC_MAIN + GATE_RANK
PROJ_TN = 512
OFF_GA = 0
OFF_GB = 4096
OFF_QA = 8192
OFF_ZA = 10240
OFF_VB = 12288
OFF_ZB = 14336
OFF_QB = 16384
OFF_KB = 17408
OFF_KA = 18432
OFF_VA = 18688
PROJ_COLS = 18944
_PROJ_TILES = ([(j * PROJ_TN, 16 + t) for j, t in zip((0, 1, 2, 3, 5, 6, 7, 8, 9, 10, 11, 12, 13, 14, 15, 16,
                                                        17, 18, 19, 20),
                                                       (0, 1, 2, 3, 4, 5, 6, 7, 16, 17, 18, 19, 8, 9, 10, 11,
                                                        12, 13, 14, 15))]
               + [(SRC_GATES + j * PROJ_TN, j) for j in range(16)]
               + [(SRC_KA, 16 + 20)])
PROJ_SRC = tuple(s for s, _ in _PROJ_TILES)
PROJ_DEST = tuple(d for _, d in _PROJ_TILES)

NT_DIMS = (((1,), (1,)), ((), ()))
TN_DIMS = (((0,), (0,)), ((), ()))


def _cparams(sem, vmem_limit=VMEM_LIMIT):
    return pltpu.CompilerParams(dimension_semantics=sem, vmem_limit_bytes=vmem_limit)


def _silu(x):
    return x * jax.nn.sigmoid(x)


def _adaln_kernel(c_ref, w_ref, b_ref, o_ref):
    a = _silu(c_ref[...]).astype(BF16)
    w = w_ref[...].astype(BF16)
    o_ref[...] = jnp.dot(a, w, preferred_element_type=F32) + b_ref[...]


def _adaln(c_all, w_ada, b_ada, tn=512):
    m, k = c_all.shape
    n = w_ada.shape[1]
    return pl.pallas_call(
        _adaln_kernel,
        out_shape=jax.ShapeDtypeStruct((m, n), F32),
        grid=(n // tn,),
        in_specs=[
            pl.BlockSpec((m, k), lambda j: (0, 0)),
            pl.BlockSpec((k, tn), lambda j: (0, j)),
            pl.BlockSpec((1, tn), lambda j: (0, j)),
        ],
        out_specs=pl.BlockSpec((m, tn), lambda j: (0, j)),
        compiler_params=_cparams(("arbitrary",)),
        name="adaln",
    )(c_all, w_ada, b_ada)


def _prologue_kernel(n_ptiles, xp_ref, xs_ref, g_ref, scp_ref, shp_ref, scs_ref, shs_ref, wr_ref,
                     h_ref, r_ref):
    i = pl.program_id(0)

    def emit(x, scale, shift):
        ms = jnp.mean(x * x, axis=-1, keepdims=True)
        y = x * lax.rsqrt(ms + EPS) * g_ref[...]
        h = y * (1.0 + scale) + shift
        hb = h.reshape(h.shape[0] * h.shape[1], h.shape[2]).astype(BF16)
        h_ref[...] = hb
        r_ref[...] = lax.dot_general(hb, wr_ref[...], NT_DIMS, preferred_element_type=F32)

    @pl.when(i < n_ptiles)
    def _():
        emit(xp_ref[...], scp_ref[...], shp_ref[...])

    @pl.when(i >= n_ptiles)
    def _():
        emit(xs_ref[...], scs_ref[...], shs_ref[...])


def _prologue(xp, xs, gain, scp, shp, scs, shs, w_rt, rows):
    b, l, d = xp.shape
    n, t, _ = xs.shape
    lb = l // rows
    n_pt = b * lb
    sg = rows // t
    n_st = n // sg
    pt = lambda i: jnp.minimum(i, n_pt - 1)
    st = lambda i: jnp.maximum(i - n_pt, 0)
    return pl.pallas_call(
        functools.partial(_prologue_kernel, n_pt),
        out_shape=(jax.ShapeDtypeStruct((b * l + n * t, d), BF16),
                   jax.ShapeDtypeStruct((b * l + n * t, LANE), F32)),
        grid=(n_pt + n_st,),
        in_specs=[
            pl.BlockSpec((1, rows, d), lambda i: (pt(i) // lb, pt(i) % lb, 0)),
            pl.BlockSpec((sg, t, d), lambda i: (st(i), 0, 0)),
            pl.BlockSpec((1, 1, d), lambda i: (0, 0, 0)),
            pl.BlockSpec((1, 1, d), lambda i: (pt(i) // lb, 0, 0)),
            pl.BlockSpec((1, 1, d), lambda i: (pt(i) // lb, 0, 0)),
            pl.BlockSpec((sg, 1, d), lambda i: (st(i), 0, 0)),
            pl.BlockSpec((sg, 1, d), lambda i: (st(i), 0, 0)),
            pl.BlockSpec((LANE, d), lambda i: (0, 0)),
        ],
        out_specs=(pl.BlockSpec((rows, d), lambda i: (i, 0)),
                   pl.BlockSpec((rows, LANE), lambda i: (i, 0))),
        compiler_params=_cparams(("arbitrary",)),
        name="prologue",
    )(xp, xs, gain, scp, shp, scs, shs, w_rt)


def _inproj_kernel(n_cast, n_xpose, dest_ref, src_ref, a_ref, w_ref, *refs):
    del dest_ref, src_ref
    n_in = n_cast + n_xpose
    side_in, o_ref, last_ref, side_out = refs[:n_in], refs[n_in], refs[n_in + 1], refs[n_in + 2:]
    w = w_ref[...].astype(BF16)
    acc = lax.dot_general(a_ref[...], w, NT_DIMS, preferred_element_type=F32)
    o_ref[...] = acc.astype(o_ref.dtype)
    last_ref[...] = acc

    for src, dst in zip(side_in[:n_cast], side_out[:n_cast]):
        dst[...] = src[...].astype(dst.dtype)
    for src, dst in zip(side_in[n_cast:], side_out[n_cast:]):
        dst[0] = jnp.transpose(src[0])


def _in_proj(a, wt, dest, src, tm, tn, out_dtype, casts, xposes, n_side):
    m, k = a.shape
    nt = len(dest)
    assert (m // tm) * nt >= n_side
    chunk = lambda i, j, d, s: jnp.minimum(i * nt + j, n_side - 1)
    side_in, side_out, side_shapes = [], [], []
    for arr in casts:
        blk = (arr.shape[0] // n_side, arr.shape[1])
        assert blk[0] % BF16_ROWS == 0
        spec = pl.BlockSpec(blk, lambda i, j, d, s: (chunk(i, j, d, s), 0))
        side_in.append(spec)
        side_out.append(spec)
        side_shapes.append(jax.ShapeDtypeStruct(arr.shape, BF16))
    for arr in xposes:
        _, r, c = arr.shape
        side_in.append(pl.BlockSpec((1, r, c), lambda i, j, d, s: (chunk(i, j, d, s), 0, 0)))
        side_out.append(pl.BlockSpec((1, c, r), lambda i, j, d, s: (chunk(i, j, d, s), 0, 0)))
        side_shapes.append(jax.ShapeDtypeStruct((n_side, c, r), arr.dtype))
    return pl.pallas_call(
        functools.partial(_inproj_kernel, len(casts), len(xposes)),
        out_shape=[jax.ShapeDtypeStruct((m, nt * tn), out_dtype), jax.ShapeDtypeStruct((m, tn), F32)] + side_shapes,
        grid_spec=pltpu.PrefetchScalarGridSpec(
            num_scalar_prefetch=2,
            grid=(m // tm, nt),
            in_specs=[
                pl.BlockSpec((tm, k), lambda i, j, d, s: (i, 0), pipeline_mode=pl.Buffered(1)),
                pl.BlockSpec((pl.Element(tn), pl.Element(k)),
                             lambda i, j, d, s: (pl.multiple_of(s[j], GATE_RANK), 0)),
            ] + side_in,
            out_specs=[pl.BlockSpec((tm, tn), lambda i, j, d, s: (i, d[j])),
                       pl.BlockSpec((tm, tn), lambda i, j, d, s: (i, 0))] + side_out,
        ),
        compiler_params=_cparams(("arbitrary", "arbitrary"), VMEM_LIMIT_RESIDENT),
        name="in_proj",
    )(jnp.asarray(dest, jnp.int32), jnp.asarray(src, jnp.int32), a, wt, *casts, *xposes)


def _swa_prompt_kernel(sink_ref, q_ref, za_ref, kp_ref, kc_ref, vp_ref, vc_ref, o_ref):
    n = pl.program_id(1)
    blk = ATTN_BLOCK
    pairs = GROUP // 2
    pw = 2 * HEAD_DIM
    hw = GROUP * HEAD_DIM
    k = jnp.concatenate([kp_ref[...], kc_ref[...]], axis=0)
    v = jnp.concatenate([vp_ref[...], vc_ref[...]], axis=0)
    rows = pairs * blk
    i = lax.broadcasted_iota(jnp.int32, (rows, 2 * blk), 0) % blk
    j = lax.broadcasted_iota(jnp.int32, (rows, 2 * blk), 1)
    visible = (j >= i) & (j <= i + WINDOW) & ((j >= blk) | (n > 0))
    bias = jnp.where(visible, 0.0, NEG_INF)
    pair_id = lax.broadcasted_iota(jnp.int32, (rows, 1), 0) // blk
    zeros = jnp.zeros((2 * blk, HEAD_DIM), BF16)
    for h in range(N_KV):
        kh = k[:, h * HEAD_DIM:(h + 1) * HEAD_DIM] * (HEAD_DIM ** -0.5)
        vh = v[:, h * HEAD_DIM:(h + 1) * HEAD_DIM]
        q2 = jnp.concatenate([q_ref[:, h * hw + jj * pw:h * hw + (jj + 1) * pw] for jj in range(pairs)], axis=0)
        acc = None
        for half in range(2):
            kv_parts = (lambda t: [t, zeros]) if half == 0 else (lambda t: [zeros, t])
            k_pad = jnp.concatenate(kv_parts(kh), axis=1)
            v_pad = jnp.concatenate(kv_parts(vh), axis=1)
            sink = jnp.zeros((rows, 1), F32)
            for jj in range(pairs):
                sink = jnp.where(pair_id == jj, sink_ref[h * GROUP + 2 * jj + half], sink)
            s = lax.dot_general(q2, k_pad, NT_DIMS, preferred_element_type=F32) + bias
            m = jnp.maximum(jnp.max(s, axis=-1, keepdims=True), sink)
            p = jnp.exp(s - m)
            denom = jnp.sum(p, axis=-1, keepdims=True) + jnp.exp(sink - m)
            o = jnp.dot(p.astype(BF16), v_pad, preferred_element_type=F32) / denom
            acc = o if acc is None else acc + o
        o_h = jnp.concatenate([acc[jj * blk:(jj + 1) * blk] for jj in range(pairs)], axis=1)
        za = za_ref[:, h * hw:(h + 1) * hw].astype(F32)
        o_ref[:, h * hw:(h + 1) * hw] = (o_h * _silu(za)).astype(BF16)


def _swa_prompt(sink, p, b, l):
    nb = l // ATTN_BLOCK
    kcol = OFF_KA // KV_W
    vcol = OFF_VA // KV_W
    blk_q = (ATTN_BLOCK, WIDTH_A)
    blk_kv = (ATTN_BLOCK, KV_W)
    cur = lambda bi, n: bi * nb + n
    prev = lambda bi, n: bi * nb + jnp.maximum(n - 1, 0)
    return pl.pallas_call(
        _swa_prompt_kernel,
        out_shape=jax.ShapeDtypeStruct((b * l, WIDTH_A), BF16),
        grid=(b, nb),
        in_specs=[
            pl.BlockSpec(memory_space=pltpu.SMEM),
            pl.BlockSpec(blk_q, lambda bi, n: (cur(bi, n), OFF_QA // WIDTH_A)),
            pl.BlockSpec(blk_q, lambda bi, n: (cur(bi, n), OFF_ZA // WIDTH_A)),
            pl.BlockSpec(blk_kv, lambda bi, n: (prev(bi, n), kcol)),
            pl.BlockSpec(blk_kv, lambda bi, n: (cur(bi, n), kcol)),
            pl.BlockSpec(blk_kv, lambda bi, n: (prev(bi, n), vcol)),
            pl.BlockSpec(blk_kv, lambda bi, n: (cur(bi, n), vcol)),
        ],
        out_specs=pl.BlockSpec(blk_q, lambda bi, n: (cur(bi, n), 0)),
        compiler_params=_cparams(("arbitrary", "arbitrary")),
        name="swa_prompt",
    )(sink, p, p, p, p, p, p)


def _swa_sample_kernel(g_blk, t_len, w_len, sink_ref, q_ref, za_ref, kv_ref, kc_ref, vc_ref,
                       o_ref, kw_ref, vw_ref):
    kpad = 2 * WINDOW - w_len - t_len
    nkeys = 2 * WINDOW
    rows = GROUP * t_len
    r = lax.broadcasted_iota(jnp.int32, (rows, nkeys), 0)
    j = lax.broadcasted_iota(jnp.int32, (rows, nkeys), 1)
    t = r % t_len
    visible = (j <= w_len + t) & (j >= w_len + t - WINDOW)
    bias = jnp.where(visible, 0.0, NEG_INF)[None]
    rg = lax.broadcasted_iota(jnp.int32, (rows, 1), 0) // t_len
    sink_cols = []
    for h in range(N_KV):
        col = jnp.zeros((rows, 1), F32)
        for g in range(GROUP):
            col = jnp.where(rg == g, sink_ref[h * GROUP + g], col)
        sink_cols.append(col)
    sink = jnp.concatenate(sink_cols * g_blk, axis=0).reshape(g_blk * N_KV, rows, 1)
    zpad = jnp.zeros((kpad, KV_W), F32)
    q_all = q_ref[...].astype(F32)
    za_all = za_ref[...].astype(F32)

    scores, values = [], []
    for s in range(g_blk):
        kc = kc_ref[s]
        vc = vc_ref[s]
        kn = kv_ref[s, :, :KV_W]
        vn = kv_ref[s, :, KV_W:]
        kw_ref[s] = jnp.concatenate([kc[t_len:], kn], axis=0)
        vw_ref[s] = jnp.concatenate([vc[t_len:], vn], axis=0)
        k_all = jnp.concatenate([kc, kn, zpad], axis=0).astype(BF16) * (HEAD_DIM ** -0.5)
        v_all = jnp.concatenate([vc, vn, zpad], axis=0).astype(BF16)
        q = q_all[s * t_len:(s + 1) * t_len]
        for h in range(N_KV):
            qh = jnp.concatenate(
                [q[:, (h * GROUP + g) * HEAD_DIM:(h * GROUP + g + 1) * HEAD_DIM] for g in range(GROUP)],
                axis=0).astype(BF16)
            scores.append(lax.dot_general(qh, k_all[:, h * HEAD_DIM:(h + 1) * HEAD_DIM], NT_DIMS,
                                          preferred_element_type=F32))
            values.append(v_all[:, h * HEAD_DIM:(h + 1) * HEAD_DIM])

    sc = jnp.concatenate(scores, axis=0).reshape(g_blk * N_KV, rows, nkeys) + bias
    m = jnp.maximum(jnp.max(sc, axis=-1, keepdims=True), sink)
    p = jnp.exp(sc - m)
    inv = 1.0 / (jnp.sum(p, axis=-1, keepdims=True) + jnp.exp(sink - m))
    pb = p.astype(BF16)

    seq_outs = []
    for s in range(g_blk):
        pieces = []
        for h in range(N_KV):
            c = s * N_KV + h
            oh = jnp.dot(pb[c], values[c], preferred_element_type=F32) * inv[c]
            for g in range(GROUP):
                pieces.append(oh[g * t_len:(g + 1) * t_len])
        o = jnp.concatenate(pieces, axis=1)
        seq_outs.append(o * _silu(za_all[s * t_len:(s + 1) * t_len]))
    o_ref[...] = jnp.concatenate(seq_outs, axis=0).astype(BF16)


def _swa_sample(sink, p, row0, n, t_len, kv3, kv_seq0, cache_k, cache_v, g_blk=8):
    rows = g_blk * t_len
    assert rows % BF16_ROWS == 0
    w_len = cache_k.shape[1]
    blk_q = (rows, WIDTH_A)
    blk_c = (g_blk, w_len, KV_W)
    r0 = row0 // rows
    s0 = kv_seq0 // g_blk
    return pl.pallas_call(
        functools.partial(_swa_sample_kernel, g_blk, t_len, w_len),
        out_shape=(jax.ShapeDtypeStruct((n * t_len, WIDTH_A), BF16),
                   jax.ShapeDtypeStruct((n, w_len, KV_W), F32),
                   jax.ShapeDtypeStruct((n, w_len, KV_W), F32)),
        grid=(n // g_blk,),
        in_specs=[
            pl.BlockSpec(memory_space=pltpu.SMEM),
            pl.BlockSpec(blk_q, lambda i: (r0 + i, OFF_QA // WIDTH_A)),
            pl.BlockSpec(blk_q, lambda i: (r0 + i, OFF_ZA // WIDTH_A)),
            pl.BlockSpec((g_blk, t_len, 2 * KV_W), lambda i: (s0 + i, 0, 0)),
            pl.BlockSpec(blk_c, lambda i: (i, 0, 0)),
            pl.BlockSpec(blk_c, lambda i: (i, 0, 0)),
        ],
        out_specs=(pl.BlockSpec(blk_q, lambda i: (i, 0)),
                   pl.BlockSpec(blk_c, lambda i: (i, 0, 0)),
                   pl.BlockSpec(blk_c, lambda i: (i, 0, 0))),
        compiler_params=_cparams(("arbitrary",)),
        name="swa_sample",
    )(sink, p, p, kv3, cache_k, cache_v)


def _cumsum_rows(x):
    c = x.shape[0]
    row = lax.broadcasted_iota(jnp.int32, (c, 1), 0)
    sh = 1
    while sh < c:
        x = x + jnp.where(row >= sh, pltpu.roll(x, sh, 0), 0.0)
        sh *= 2
    return x


def _gla_chunk(q, k, v, la, s_old, causal):
    c = q.shape[0]
    b = _cumsum_rows(la)
    b_last = b[c - 1:c, :]
    decay_col = jnp.transpose(jnp.broadcast_to(jnp.exp(b_last), (LANE, b_last.shape[1])))
    q_t = (q * jnp.exp(b) * (DK_B ** -0.5)).astype(BF16)
    k_t = (k * jnp.exp(-b)).astype(BF16)
    k_d = (k * jnp.exp(b_last - b)).astype(BF16)
    outs, states = [], []
    for h in range(N_HEADS_B):
        dk = slice(h * DK_B, (h + 1) * DK_B)
        dv = slice(h * DV_B, (h + 1) * DV_B)
        att = lax.dot_general(q_t[:, dk], k_t[:, dk], NT_DIMS, preferred_element_type=F32)
        att = jnp.where(causal, att, 0.0).astype(BF16)
        outs.append(jnp.dot(att, v[:, dv], preferred_element_type=F32)
                    + jnp.dot(q_t[:, dk], s_old[h].astype(BF16), preferred_element_type=F32))
        decay = jnp.concatenate([decay_col[dk]] * (DV_B // LANE), axis=1)
        states.append(decay * s_old[h] + lax.dot_general(k_d[:, dk], v[:, dv], TN_DIMS,
                                                         preferred_element_type=F32))
    return outs, states


def _gla_chunk_pair(q, k, v, la, s_old, causal):
    c = q.shape[0] // 2
    ra, rb = slice(0, c), slice(c, 2 * c)
    b_a, b_b = _cumsum_rows(la[ra]), _cumsum_rows(la[rb])
    t_a, t_b = b_a[c - 1:c, :], b_b[c - 1:c, :]
    scale = DK_B ** -0.5
    q_ta = (q[ra] * jnp.exp(b_a) * scale).astype(BF16)
    q_tb = q[rb] * jnp.exp(b_b) * scale
    k_ta = (k[ra] * jnp.exp(-b_a)).astype(BF16)
    k_tb = (k[rb] * jnp.exp(-b_b)).astype(BF16)
    k_da = k[ra] * jnp.exp(t_a - b_a)
    k_db = (k[rb] * jnp.exp(t_b - b_b)).astype(BF16)
    q_read = jnp.concatenate([q_ta, (q_tb * jnp.exp(t_a)).astype(BF16)], axis=0)
    k_upd = jnp.concatenate([(k_da * jnp.exp(t_b)).astype(BF16), k_db], axis=0)
    q_tb, k_da = q_tb.astype(BF16), k_da.astype(BF16)
    decay_col = jnp.transpose(jnp.broadcast_to(jnp.exp(t_a + t_b), (LANE, t_a.shape[1])))
    outs, states = [], []
    for h in range(N_HEADS_B):
        dk = slice(h * DK_B, (h + 1) * DK_B)
        dv = slice(h * DV_B, (h + 1) * DV_B)
        att_a = lax.dot_general(q_ta[:, dk], k_ta[:, dk], NT_DIMS, preferred_element_type=F32)
        att_b = lax.dot_general(q_tb[:, dk], k_tb[:, dk], NT_DIMS, preferred_element_type=F32)
        cross = lax.dot_general(q_tb[:, dk], k_da[:, dk], NT_DIMS, preferred_element_type=F32)
        on_a = jnp.concatenate([jnp.where(causal, att_a, 0.0), cross], axis=0).astype(BF16)
        o = (jnp.dot(on_a, v[ra, dv], preferred_element_type=F32)
             + jnp.dot(q_read[:, dk], s_old[h].astype(BF16), preferred_element_type=F32))
        o_b = o[rb] + jnp.dot(jnp.where(causal, att_b, 0.0).astype(BF16), v[rb, dv], preferred_element_type=F32)
        outs.append(jnp.concatenate([o[ra], o_b], axis=0))
        decay = jnp.concatenate([decay_col[dk]] * (DV_B // LANE), axis=1)
        states.append(decay * s_old[h] + lax.dot_general(k_upd[:, dk], v[:, dv], TN_DIMS,
                                                         preferred_element_type=F32))
    return outs, states


def _log_decay(r, w2, b2):
    z = jnp.dot(r.astype(BF16), w2, preferred_element_type=F32) + b2
    ls = jnp.minimum(z, 0.0) - jnp.log(1.0 + jnp.exp(-jnp.abs(z)))
    return ls / GLA_TAU


def _gla_finish(o, zb, gain):
    on = o * lax.rsqrt(jnp.mean(o * o, axis=-1, keepdims=True) + EPS) * gain
    return on * _silu(zb)


def _causal_mask(c):
    ri = lax.broadcasted_iota(jnp.int32, (c, c), 0)
    ci = lax.broadcasted_iota(jnp.int32, (c, c), 1)
    return ri >= ci


def _gla_prompt_kernel(nb, *refs):
    w2_ref, b2_ref, gain_ref, o_ref, s_ref = refs[5 * nb:]

    @pl.when(pl.program_id(0) == 0)
    def _():
        s_ref[...] = jnp.zeros_like(s_ref)

    c = 2 * GLA_CHUNK
    causal = _causal_mask(GLA_CHUNK)
    for bi in range(nb):
        q_ref, k_ref, v_ref, zb_ref, r_ref = refs[5 * bi:5 * bi + 5]
        states = [s_ref[bi, h] for h in range(N_HEADS_B)]
        for sub in range(o_ref.shape[1] // c):
            rs = slice(sub * c, (sub + 1) * c)
            la = _log_decay(r_ref[rs, :], w2_ref[...], b2_ref[...])
            outs, states = _gla_chunk_pair(q_ref[rs, :].astype(F32), k_ref[rs, :].astype(F32), v_ref[rs, :], la,
                                           states, causal)
            for h in range(N_HEADS_B):
                dv = slice(h * DV_B, (h + 1) * DV_B)
                o_ref[bi, rs, dv] = _gla_finish(outs[h], zb_ref[rs, dv].astype(F32), gain_ref[:, dv]).astype(BF16)
        for h in range(N_HEADS_B):
            s_ref[bi, h] = states[h]


def _gla_prompt(p, r, w2, b2, gain, b, l, chunks_per_step=2):
    c = GLA_CHUNK * chunks_per_step
    nc = l // c
    in_specs, args = [], []
    for bi in range(b):
        for arr, width, off in ((p, KEY_B, OFF_QB), (p, KEY_B, OFF_KB), (p, WIDTH_B, OFF_VB),
                                (p, WIDTH_B, OFF_ZB), (r, LANE, 0)):
            in_specs.append(pl.BlockSpec((c, width), functools.partial(
                lambda ci, row0, col: (row0 + ci, col), row0=bi * nc, col=off // width)))
            args.append(arr)
    in_specs += [pl.BlockSpec((LANE, KEY_B), lambda ci: (0, 0)),
                 pl.BlockSpec((1, KEY_B), lambda ci: (0, 0)),
                 pl.BlockSpec((1, WIDTH_B), lambda ci: (0, 0))]
    return pl.pallas_call(
        functools.partial(_gla_prompt_kernel, b),
        out_shape=(jax.ShapeDtypeStruct((b, l, WIDTH_B), BF16),
                   jax.ShapeDtypeStruct((b, N_HEADS_B, DK_B, DV_B), F32)),
        grid=(nc,),
        in_specs=in_specs,
        out_specs=(pl.BlockSpec((b, c, WIDTH_B), lambda ci: (0, ci, 0)),
                   pl.BlockSpec((b, N_HEADS_B, DK_B, DV_B), lambda ci: (0, 0, 0, 0))),
        compiler_params=_cparams(("arbitrary",)),
        name="gla_prompt",
    )(*args, w2, b2, gain)


def _gla_sample_kernel(g_blk, t_len, q_ref, k_ref, v_ref, zb_ref, r_ref, w2_ref, b2_ref, gain_ref, s0_ref,
                       o_ref, s_ref):
    c = 2 * t_len
    causal = _causal_mask(c)

    def pad(x):
        return jnp.concatenate([x, jnp.zeros_like(x)], axis=0)

    live = lax.broadcasted_iota(jnp.int32, (c, 1), 0) < t_len
    q_all = q_ref[...].astype(F32)
    k_all = k_ref[...].astype(F32)
    v_all = v_ref[...].astype(F32)
    zb_all = zb_ref[...].astype(F32)
    seq_outs = []
    for s in range(g_blk):
        rs = slice(s * t_len, (s + 1) * t_len)
        la = jnp.where(live, _log_decay(pad(r_ref[s]), w2_ref[...], b2_ref[...]), 0.0)
        outs, states = _gla_chunk(pad(q_all[rs]), pad(k_all[rs]), pad(v_all[rs]).astype(BF16), la,
                                  [s0_ref[s, h] for h in range(N_HEADS_B)], causal)
        fin = []
        for h in range(N_HEADS_B):
            dv = slice(h * DV_B, (h + 1) * DV_B)
            s_ref[s, h] = states[h]
            fin.append(_gla_finish(outs[h][:t_len], zb_all[rs, dv], gain_ref[:, dv]))
        seq_outs.append(jnp.concatenate(fin, axis=1))
    o_ref[...] = jnp.concatenate(seq_outs, axis=0).astype(BF16)


def _gla_sample(p, r3, row0, t_len, w2, b2, gain, state, g_blk=4):
    n = state.shape[0]
    rows = g_blk * t_len
    assert rows % BF16_ROWS == 0
    st_blk = (g_blk, N_HEADS_B, DK_B, DV_B)
    r0 = row0 // rows
    return pl.pallas_call(
        functools.partial(_gla_sample_kernel, g_blk, t_len),
        out_shape=(jax.ShapeDtypeStruct((n * t_len, WIDTH_B), BF16),
                   jax.ShapeDtypeStruct(state.shape, F32)),
        grid=(n // g_blk,),
        in_specs=[
            pl.BlockSpec((rows, KEY_B), lambda i: (r0 + i, OFF_QB // KEY_B)),
            pl.BlockSpec((rows, KEY_B), lambda i: (r0 + i, OFF_KB // KEY_B)),
            pl.BlockSpec((rows, WIDTH_B), lambda i: (r0 + i, OFF_VB // WIDTH_B)),
            pl.BlockSpec((rows, WIDTH_B), lambda i: (r0 + i, OFF_ZB // WIDTH_B)),
            pl.BlockSpec((g_blk, t_len, LANE), lambda i: (r0 + i, 0, 0)),
            pl.BlockSpec((LANE, KEY_B), lambda i: (0, 0)),
            pl.BlockSpec((1, KEY_B), lambda i: (0, 0)),
            pl.BlockSpec((1, WIDTH_B), lambda i: (0, 0)),
            pl.BlockSpec(st_blk, lambda i: (i, 0, 0, 0)),
        ],
        out_specs=(pl.BlockSpec((rows, WIDTH_B), lambda i: (i, 0)),
                   pl.BlockSpec(st_blk, lambda i: (i, 0, 0, 0))),
        compiler_params=_cparams(("arbitrary",)),
        name="gla_sample",
    )(p, p, p, p, r3, w2, b2, gain, state)


def _merge_kernel(nchunk, n_ptiles, ap_ref, bp_ref, as_ref, bs_ref, wa_ref, wb_ref, ga_ref, gb_ref, o_ref):
    cw = o_ref.shape[1] // nchunk
    is_prompt = pl.program_id(0) < n_ptiles
    a = jnp.where(is_prompt, ap_ref[...], as_ref[...])
    b = jnp.where(is_prompt, bp_ref[...], bs_ref[...])
    for c in range(nchunk):
        cs = slice(c * cw, (c + 1) * cw)
        ua = jnp.dot(a, wa_ref[:, cs], preferred_element_type=F32)
        ub = jnp.dot(b, wb_ref[:, cs], preferred_element_type=F32)
        merged = (jax.nn.sigmoid(ga_ref[:, cs].astype(F32)) * ua
                  + jax.nn.sigmoid(gb_ref[:, cs].astype(F32)) * ub)
        o_ref[:, cs] = merged.astype(BF16)


def _merge(ap, bp, a_s, b_s, w_pa, w_pb, p, tm):
    n_pt = ap.shape[0] // tm
    n_st = a_s.shape[0] // tm
    d = D_MODEL
    resident = dict(pipeline_mode=pl.Buffered(1))
    pt = lambda i: (jnp.minimum(i, n_pt - 1), 0)
    st = lambda i: (jnp.maximum(i - n_pt, 0), 0)
    return pl.pallas_call(
        functools.partial(_merge_kernel, 4, n_pt),
        out_shape=jax.ShapeDtypeStruct(((n_pt + n_st) * tm, d), BF16),
        grid=(n_pt + n_st,),
        in_specs=[
            pl.BlockSpec((tm, WIDTH_A), pt),
            pl.BlockSpec((tm, WIDTH_B), pt),
            pl.BlockSpec((tm, WIDTH_A), st),
            pl.BlockSpec((tm, WIDTH_B), st),
            pl.BlockSpec((WIDTH_A, d), lambda i: (0, 0), **resident),
            pl.BlockSpec((WIDTH_B, d), lambda i: (0, 0), **resident),
            pl.BlockSpec((tm, d), lambda i: (i, OFF_GA // d)),
            pl.BlockSpec((tm, d), lambda i: (i, OFF_GB // d)),
        ],
        out_specs=pl.BlockSpec((tm, d), lambda i: (i, 0)),
        compiler_params=_cparams(("arbitrary",), VMEM_LIMIT_RESIDENT),
        name="merge",
    )(ap, bp, a_s, b_s, w_pa, w_pb, p, p)


def _out_kernel(nchunk, m_ref, w_ref, x_ref, gate_ref, fg_ref, o_ref):
    g_blk, r_blk, d = x_ref.shape
    cw = d // nchunk
    ssq = jnp.zeros((g_blk, r_blk, 1), F32)
    for c in range(nchunk):
        cs = slice(c * cw, (c + 1) * cw)
        y = jnp.dot(m_ref[...], w_ref[:, cs], preferred_element_type=F32)
        xn = x_ref[:, :, cs] + gate_ref[:, :, cs] * y.reshape(g_blk, r_blk, cw)
        o_ref[:, :, cs] = xn
        ssq = ssq + jnp.sum(xn * xn, axis=-1, keepdims=True)
    inv = lax.rsqrt(ssq * (1.0 / d) + EPS)
    for c in range(nchunk):
        cs = slice(c * cw, (c + 1) * cw)
        o_ref[:, :, cs] = o_ref[:, :, cs] * inv * fg_ref[:, :, cs]


def _out_proj(merged, row0, w_o, x3, gate3, fin_gain, g_blk, r_blk):
    n, l, d = x3.shape
    rows = g_blk * r_blk
    lb = l // r_blk
    t0 = row0 // rows
    return pl.pallas_call(
        functools.partial(_out_kernel, 4),
        out_shape=jax.ShapeDtypeStruct((n, l, d), F32),
        grid=(n // g_blk, lb),
        in_specs=[
            pl.BlockSpec((rows, d), lambda i, r: (t0 + i * lb + r, 0)),
            pl.BlockSpec((d, d), lambda i, r: (0, 0), pipeline_mode=pl.Buffered(1)),
            pl.BlockSpec((g_blk, r_blk, d), lambda i, r: (i, r, 0)),
            pl.BlockSpec((g_blk, 1, d), lambda i, r: (i, 0, 0)),
            pl.BlockSpec((1, 1, d), lambda i, r: (0, 0, 0)),
        ],
        out_specs=pl.BlockSpec((g_blk, r_blk, d), lambda i, r: (i, r, 0)),
        compiler_params=_cparams(("arbitrary", "arbitrary"), VMEM_LIMIT_RESIDENT),
        name="out_proj",
    )(merged, w_o, x3, gate3, fin_gain)


def kernel(x_prompt, x_sample, cache_k_win, cache_v_win, state_gla, c_prompt, c_sample,
           w_ada, b_ada, norm_gain, w_in, attn_sink, w_alpha2, b_alpha, gla_norm_gain,
           w_proj_a, w_proj_b, w_out, final_norm_gain):
    assert w_ada.shape[0] == 1, "single-layer step"
    bp, lp, d = x_prompt.shape
    ns, ts, _ = x_sample.shape
    w_len = cache_k_win.shape[2]
    rows_p = bp * lp
    rows_s = ns * ts
    rows = rows_p + rows_s

    wt = jnp.transpose(w_in[0])
    w_rt = jnp.pad(wt[SRC_MAIN:SRC_GATES], ((0, LANE - GATE_RANK), (0, 0))).astype(BF16)
    w2 = jnp.pad(w_alpha2[0], ((0, LANE - GATE_RANK), (0, 0))).astype(BF16)
    b2 = b_alpha[0].reshape(1, KEY_B)
    gla_gain = gla_norm_gain[0].reshape(1, WIDTH_B)
    cache_t = lambda c: jnp.transpose(c[0], (0, 2, 3, 1)).reshape(ns, KV_W, w_len)
    gain3 = norm_gain[0].reshape(1, 1, d)
    fin3 = final_norm_gain.reshape(1, 1, d)
    sink = attn_sink[0]

    m_all = ns + bp
    m_pad = -(-m_all // 8) * 8
    c_all = jnp.concatenate([c_sample, c_prompt, jnp.zeros((m_pad - m_all, d), F32)], axis=0)
    mod = _adaln(c_all, w_ada[0], b_ada[0].reshape(1, 3 * d))
    shift, scale, gate = mod[:, :d], mod[:, d:2 * d], mod[:, 2 * d:]
    sl_s, sl_p = slice(0, ns), slice(ns, ns + bp)
    as3 = lambda a, n: a.reshape(n, 1, d)

    h, r = _prologue(x_prompt, x_sample, gain3, as3(scale[sl_p], bp), as3(shift[sl_p], bp),
                     as3(scale[sl_s], ns), as3(shift[sl_s], ns), w_rt, 512)
    p, kv, w_pa, w_pb, w_o, cache_k, cache_v = _in_proj(
        h, wt, PROJ_DEST, PROJ_SRC, rows // 4, PROJ_TN, BF16,
        casts=(w_proj_a[0], w_proj_b[0], w_out[0]),
        xposes=(cache_t(cache_k_win), cache_t(cache_v_win)), n_side=ns)
    kv_p = jnp.stack([kv[(bi + 1) * lp - WINDOW:(bi + 1) * lp] for bi in range(bp)])
    kv_p = kv_p.reshape(bp, WINDOW, 2, N_KV, HEAD_DIM)
    k_win_p = kv_p[:, :, 0][None]
    v_win_p = kv_p[:, :, 1][None]
    kv3 = kv.reshape(rows // ts, ts, 2 * KV_W)
    r3 = r.reshape(rows // ts, ts, LANE)

    ua_p = _swa_prompt(sink, p, bp, lp)
    ub_p, s_p = _gla_prompt(p, r, w2, b2, gla_gain, bp, lp)

    ua_s, k_win_s, v_win_s = _swa_sample(sink, p, rows_p, ns, ts, kv3, rows_p // ts, cache_k, cache_v)
    ub_s, s_s = _gla_sample(p, r3, rows_p, ts, w2, b2, gla_gain, state_gla[0])

    merged = _merge(ua_p, ub_p.reshape(rows_p, WIDTH_B), ua_s, ub_s, w_pa, w_pb, p, 256)
    y_p = _out_proj(merged, 0, w_o, x_prompt, as3(gate[sl_p], bp), fin3, 1, 256)
    y_s = _out_proj(merged, rows_p, w_o, x_sample, as3(gate[sl_s], ns), fin3, 256 // ts, ts)

    return (y_p, y_s, k_win_p, v_win_p, s_p[None],
            k_win_s.reshape(1, ns, w_len, N_KV, HEAD_DIM), v_win_s.reshape(1, ns, w_len, N_KV, HEAD_DIM),
            s_s[None])
```

```python
import functools

import jax
import jax.numpy as jnp
from jax import lax
from jax.experimental import pallas as pl
from jax.experimental.pallas import tpu as pltpu

F32 = jnp.float32
BF16 = jnp.bfloat16

D_MODEL = 4096
WIDTH_A = 2048
HEAD_DIM = 64
N_HEADS_A = 32
N_KV = 4
GROUP = 8
WINDOW = 128
ATTN_BLOCK = 128
WIDTH_B = 2048
N_HEADS_B = 4
DV_B = 512
KEY_B = 1024
DK_B = 256
GATE_RANK = 16
GLA_TAU = 16.0
GLA_CHUNK = 64
EPS = 1e-6
NEG_INF = -1e30
KV_W = N_KV * HEAD_DIM

V7X_VMEM_BYTES = 64 * 1024 * 1024
VMEM_LIMIT = V7X_VMEM_BYTES - 8 * 1024 * 1024
VMEM_LIMIT_RESIDENT = V7X_VMEM_BYTES - 4 * 1024 * 1024
LANE = 128
BF16_ROWS = 16

SRC_KA = WIDTH_A
SRC_MAIN = WIDTH_A + 2 * KV_W + WIDTH_A + 2 * KEY_B + 2 * WIDTH_B
SRC_GATES = SRC_MAIN + GATE_RANK
PROJ_TN = 512
OFF_GA = 0
OFF_GB = 4096
OFF_QA = 8192
OFF_ZA = 10240
OFF_VB = 12288
OFF_ZB = 14336
OFF_QB = 16384
OFF_KB = 17408
OFF_KA = 18432
OFF_VA = 18688
PROJ_COLS = 18944
_PROJ_TILES = ([(j * PROJ_TN, 16 + t) for j, t in zip((0, 1, 2, 3, 5, 6, 7, 8, 9, 10, 11, 12, 13, 14, 15, 16,
                                                        17, 18, 19, 20),
                                                       (0, 1, 2, 3, 4, 5, 6, 7, 16, 17, 18, 19, 8, 9, 10, 11,
                                                        12, 13, 14, 15))]
               + [(SRC_GATES + j * PROJ_TN, j) for j in range(16)]
               + [(SRC_KA, 16 + 20)])
PROJ_SRC = tuple(s for s, _ in _PROJ_TILES)
PROJ_DEST = tuple(d for _, d in _PROJ_TILES)

NT_DIMS = (((1,), (1,)), ((), ()))
TN_DIMS = (((0,), (0,)), ((), ()))


def _cparams(sem, vmem_limit=VMEM_LIMIT):
    return pltpu.CompilerParams(dimension_semantics=sem, vmem_limit_bytes=vmem_limit)


def _silu(x):
    return x * jax.nn.sigmoid(x)


def _adaln_kernel(c_ref, w_ref, b_ref, o_ref):
    a = _silu(c_ref[...]).astype(BF16)
    w = w_ref[...].astype(BF16)
    o_ref[...] = jnp.dot(a, w, preferred_element_type=F32) + b_ref[...]


def _adaln(c_all, w_ada, b_ada, tn=512):
    m, k = c_all.shape
    n = w_ada.shape[1]
    return pl.pallas_call(
        _adaln_kernel,
        out_shape=jax.ShapeDtypeStruct((m, n), F32),
        grid=(n // tn,),
        in_specs=[
            pl.BlockSpec((m, k), lambda j: (0, 0)),
            pl.BlockSpec((k, tn), lambda j: (0, j)),
            pl.BlockSpec((1, tn), lambda j: (0, j)),
        ],
        out_specs=pl.BlockSpec((m, tn), lambda j: (0, j)),
        compiler_params=_cparams(("arbitrary",)),
        name="adaln",
    )(c_all, w_ada, b_ada)


def _prologue_kernel(n_ptiles, xp_ref, xs_ref, g_ref, scp_ref, shp_ref, scs_ref, shs_ref, wr_ref,
                     h_ref, r_ref):
    i = pl.program_id(0)

    def emit(x, scale, shift):
        ms = jnp.mean(x * x, axis=-1, keepdims=True)
        y = x * lax.rsqrt(ms + EPS) * g_ref[...]
        h = y * (1.0 + scale) + shift
        hb = h.reshape(h.shape[0] * h.shape[1], h.shape[2]).astype(BF16)
        h_ref[...] = hb
        r_ref[...] = lax.dot_general(hb, wr_ref[...], NT_DIMS, preferred_element_type=F32)

    @pl.when(i < n_ptiles)
    def _():
        emit(xp_ref[...], scp_ref[...], shp_ref[...])

    @pl.when(i >= n_ptiles)
    def _():
        emit(xs_ref[...], scs_ref[...], shs_ref[...])


def _prologue(xp, xs, gain, scp, shp, scs, shs, w_rt, rows):
    b, l, d = xp.shape
    n, t, _ = xs.shape
    lb = l // rows
    n_pt = b * lb
    sg = rows // t
    n_st = n // sg
    pt = lambda i: jnp.minimum(i, n_pt - 1)
    st = lambda i: jnp.maximum(i - n_pt, 0)
    return pl.pallas_call(
        functools.partial(_prologue_kernel, n_pt),
        out_shape=(jax.ShapeDtypeStruct((b * l + n * t, d), BF16),
                   jax.ShapeDtypeStruct((b * l + n * t, LANE), F32)),
        grid=(n_pt + n_st,),
        in_specs=[
            pl.BlockSpec((1, rows, d), lambda i: (pt(i) // lb, pt(i) % lb, 0)),
            pl.BlockSpec((sg, t, d), lambda i: (st(i), 0, 0)),
            pl.BlockSpec((1, 1, d), lambda i: (0, 0, 0)),
            pl.BlockSpec((1, 1, d), lambda i: (pt(i) // lb, 0, 0)),
            pl.BlockSpec((1, 1, d), lambda i: (pt(i) // lb, 0, 0)),
            pl.BlockSpec((sg, 1, d), lambda i: (st(i), 0, 0)),
            pl.BlockSpec((sg, 1, d), lambda i: (st(i), 0, 0)),
            pl.BlockSpec((LANE, d), lambda i: (0, 0)),
        ],
        out_specs=(pl.BlockSpec((rows, d), lambda i: (i, 0)),
                   pl.BlockSpec((rows, LANE), lambda i: (i, 0))),
        compiler_params=_cparams(("arbitrary",)),
        name="prologue",
    )(xp, xs, gain, scp, shp, scs, shs, w_rt)


def _inproj_kernel(n_cast, n_xpose, dest_ref, src_ref, a_ref, w_ref, *refs):
    del dest_ref, src_ref
    n_in = n_cast + n_xpose
    side_in, o_ref, last_ref, side_out = refs[:n_in], refs[n_in], refs[n_in + 1], refs[n_in + 2:]
    w = w_ref[...].astype(BF16)
    acc = lax.dot_general(a_ref[...], w, NT_DIMS, preferred_element_type=F32)
    o_ref[...] = acc.astype(o_ref.dtype)
    last_ref[...] = acc

    for src, dst in zip(side_in[:n_cast], side_out[:n_cast]):
        dst[...] = src[...].astype(dst.dtype)
    for src, dst in zip(side_in[n_cast:], side_out[n_cast:]):
        dst[0] = jnp.transpose(src[0])


def _in_proj(a, wt, dest, src, tm, tn, out_dtype, casts, xposes, n_side):
    m, k = a.shape
    nt = len(dest)
    assert (m // tm) * nt >= n_side
    chunk = lambda i, j, d, s: jnp.minimum(i * nt + j, n_side - 1)
    side_in, side_out, side_shapes = [], [], []
    for arr in casts:
        blk = (arr.shape[0] // n_side, arr.shape[1])
        assert blk[0] % BF16_ROWS == 0
        spec = pl.BlockSpec(blk, lambda i, j, d, s: (chunk(i, j, d, s), 0))
        side_in.append(spec)
        side_out.append(spec)
        side_shapes.append(jax.ShapeDtypeStruct(arr.shape, BF16))
    for arr in xposes:
        _, r, c = arr.shape
        side_in.append(pl.BlockSpec((1, r, c), lambda i, j, d, s: (chunk(i, j, d, s), 0, 0)))
        side_out.append(pl.BlockSpec((1, c, r), lambda i, j, d, s: (chunk(i, j, d, s), 0, 0)))
        side_shapes.append(jax.ShapeDtypeStruct((n_side, c, r), arr.dtype))
    return pl.pallas_call(
        functools.partial(_inproj_kernel, len(casts), len(xposes)),
        out_shape=[jax.ShapeDtypeStruct((m, nt * tn), out_dtype), jax.ShapeDtypeStruct((m, tn), F32)] + side_shapes,
        grid_spec=pltpu.PrefetchScalarGridSpec(
            num_scalar_prefetch=2,
            grid=(m // tm, nt),
            in_specs=[
                pl.BlockSpec((tm, k), lambda i, j, d, s: (i, 0), pipeline_mode=pl.Buffered(1)),
                pl.BlockSpec((pl.Element(tn), pl.Element(k)),
                             lambda i, j, d, s: (pl.multiple_of(s[j], GATE_RANK), 0)),
            ] + side_in,
            out_specs=[pl.BlockSpec((tm, tn), lambda i, j, d, s: (i, d[j])),
                       pl.BlockSpec((tm, tn), lambda i, j, d, s: (i, 0))] + side_out,
        ),
        compiler_params=_cparams(("arbitrary", "arbitrary"), VMEM_LIMIT_RESIDENT),
        name="in_proj",
    )(jnp.asarray(dest, jnp.int32), jnp.asarray(src, jnp.int32), a, wt, *casts, *xposes)


def _swa_prompt_kernel(sink_ref, q_ref, za_ref, kp_ref, kc_ref, vp_ref, vc_ref, o_ref):
    n = pl.program_id(1)
    blk = ATTN_BLOCK
    pairs = GROUP // 2
    pw = 2 * HEAD_DIM
    hw = GROUP * HEAD_DIM
    k = jnp.concatenate([kp_ref[...], kc_ref[...]], axis=0)
    v = jnp.concatenate([vp_ref[...], vc_ref[...]], axis=0)
    rows = pairs * blk
    i = lax.broadcasted_iota(jnp.int32, (rows, 2 * blk), 0) % blk
    j = lax.broadcasted_iota(jnp.int32, (rows, 2 * blk), 1)
    visible = (j >= i) & (j <= i + WINDOW) & ((j >= blk) | (n > 0))
    bias = jnp.where(visible, 0.0, NEG_INF)
    pair_id = lax.broadcasted_iota(jnp.int32, (rows, 1), 0) // blk
    zeros = jnp.zeros((2 * blk, HEAD_DIM), BF16)
    for h in range(N_KV):
        kh = k[:, h * HEAD_DIM:(h + 1) * HEAD_DIM] * (HEAD_DIM ** -0.5)
        vh = v[:, h * HEAD_DIM:(h + 1) * HEAD_DIM]
        q2 = jnp.concatenate([q_ref[:, h * hw + jj * pw:h * hw + (jj + 1) * pw] for jj in range(pairs)], axis=0)
        acc = None
        for half in range(2):
            kv_parts = (lambda t: [t, zeros]) if half == 0 else (lambda t: [zeros, t])
            k_pad = jnp.concatenate(kv_parts(kh), axis=1)
            v_pad = jnp.concatenate(kv_parts(vh), axis=1)
            sink = jnp.zeros((rows, 1), F32)
            for jj in range(pairs):
                sink = jnp.where(pair_id == jj, sink_ref[h * GROUP + 2 * jj + half], sink)
            s = lax.dot_general(q2, k_pad, NT_DIMS, preferred_element_type=F32) + bias
            m = jnp.maximum(jnp.max(s, axis=-1, keepdims=True), sink)
            p = jnp.exp(s - m)
            denom = jnp.sum(p, axis=-1, keepdims=True) + jnp.exp(sink - m)
            o = jnp.dot(p.astype(BF16), v_pad, preferred_element_type=F32) / denom
            acc = o if acc is None else acc + o
        o_h = jnp.concatenate([acc[jj * blk:(jj + 1) * blk] for jj in range(pairs)], axis=1)
        za = za_ref[:, h * hw:(h + 1) * hw].astype(F32)
        o_ref[:, h * hw:(h + 1) * hw] = (o_h * _silu(za)).astype(BF16)


def _swa_prompt(sink, p, b, l):
    nb = l // ATTN_BLOCK
    kcol = OFF_KA // KV_W
    vcol = OFF_VA // KV_W
    blk_q = (ATTN_BLOCK, WIDTH_A)
    blk_kv = (ATTN_BLOCK, KV_W)
    cur = lambda bi, n: bi * nb + n
    prev = lambda bi, n: bi * nb + jnp.maximum(n - 1, 0)
    return pl.pallas_call(
        _swa_prompt_kernel,
        out_shape=jax.ShapeDtypeStruct((b * l, WIDTH_A), BF16),
        grid=(b, nb),
        in_specs=[
            pl.BlockSpec(memory_space=pltpu.SMEM),
            pl.BlockSpec(blk_q, lambda bi, n: (cur(bi, n), OFF_QA // WIDTH_A)),
            pl.BlockSpec(blk_q, lambda bi, n: (cur(bi, n), OFF_ZA // WIDTH_A)),
            pl.BlockSpec(blk_kv, lambda bi, n: (prev(bi, n), kcol)),
            pl.BlockSpec(blk_kv, lambda bi, n: (cur(bi, n), kcol)),
            pl.BlockSpec(blk_kv, lambda bi, n: (prev(bi, n), vcol)),
            pl.BlockSpec(blk_kv, lambda bi, n: (cur(bi, n), vcol)),
        ],
        out_specs=pl.BlockSpec(blk_q, lambda bi, n: (cur(bi, n), 0)),
        compiler_params=_cparams(("arbitrary", "arbitrary")),
        name="swa_prompt",
    )(sink, p, p, p, p, p, p)


def _swa_sample_kernel(g_blk, t_len, w_len, sink_ref, q_ref, za_ref, kv_ref, kc_ref, vc_ref,
                       o_ref, kw_ref, vw_ref):
    kpad = 2 * WINDOW - w_len - t_len
    nkeys = 2 * WINDOW
    rows = GROUP * t_len
    r = lax.broadcasted_iota(jnp.int32, (rows, nkeys), 0)
    j = lax.broadcasted_iota(jnp.int32, (rows, nkeys), 1)
    t = r % t_len
    visible = (j <= w_len + t) & (j >= w_len + t - WINDOW)
    bias = jnp.where(visible, 0.0, NEG_INF)[None]
    rg = lax.broadcasted_iota(jnp.int32, (rows, 1), 0) // t_len
    sink_cols = []
    for h in range(N_KV):
        col = jnp.zeros((rows, 1), F32)
        for g in range(GROUP):
            col = jnp.where(rg == g, sink_ref[h * GROUP + g], col)
        sink_cols.append(col)
    sink = jnp.concatenate(sink_cols * g_blk, axis=0).reshape(g_blk * N_KV, rows, 1)
    zpad = jnp.zeros((kpad, KV_W), F32)
    q_all = q_ref[...].astype(F32)
    za_all = za_ref[...].astype(F32)

    scores, values = [], []
    for s in range(g_blk):
        kc = kc_ref[s]
        vc = vc_ref[s]
        kn = kv_ref[s, :, :KV_W]
        vn = kv_ref[s, :, KV_W:]
        kw_ref[s] = jnp.concatenate([kc[t_len:], kn], axis=0)
        vw_ref[s] = jnp.concatenate([vc[t_len:], vn], axis=0)
        k_all = jnp.concatenate([kc, kn, zpad], axis=0).astype(BF16) * (HEAD_DIM ** -0.5)
        v_all = jnp.concatenate([vc, vn, zpad], axis=0).astype(BF16)
        q = q_all[s * t_len:(s + 1) * t_len]
        for h in range(N_KV):
            qh = jnp.concatenate(
                [q[:, (h * GROUP + g) * HEAD_DIM:(h * GROUP + g + 1) * HEAD_DIM] for g in range(GROUP)],
                axis=0).astype(BF16)
            scores.append(lax.dot_general(qh, k_all[:, h * HEAD_DIM:(h + 1) * HEAD_DIM], NT_DIMS,
                                          preferred_element_type=F32))
            values.append(v_all[:, h * HEAD_DIM:(h + 1) * HEAD_DIM])

    sc = jnp.concatenate(scores, axis=0).reshape(g_blk * N_KV, rows, nkeys) + bias
    m = jnp.maximum(jnp.max(sc, axis=-1, keepdims=True), sink)
    p = jnp.exp(sc - m)
    inv = 1.0 / (jnp.sum(p, axis=-1, keepdims=True) + jnp.exp(sink - m))
    pb = p.astype(BF16)

    seq_outs = []
    for s in range(g_blk):
        pieces = []
        for h in range(N_KV):
            c = s * N_KV + h
            oh = jnp.dot(pb[c], values[c], preferred_element_type=F32) * inv[c]
            for g in range(GROUP):
                pieces.append(oh[g * t_len:(g + 1) * t_len])
        o = jnp.concatenate(pieces, axis=1)
        seq_outs.append(o * _silu(za_all[s * t_len:(s + 1) * t_len]))
    o_ref[...] = jnp.concatenate(seq_outs, axis=0).astype(BF16)


def _swa_sample(sink, p, row0, n, t_len, kv3, kv_seq0, cache_k, cache_v, g_blk=8):
    rows = g_blk * t_len
    assert rows % BF16_ROWS == 0
    w_len = cache_k.shape[1]
    blk_q = (rows, WIDTH_A)
    blk_c = (g_blk, w_len, KV_W)
    r0 = row0 // rows
    s0 = kv_seq0 // g_blk
    return pl.pallas_call(
        functools.partial(_swa_sample_kernel, g_blk, t_len, w_len),
        out_shape=(jax.ShapeDtypeStruct((n * t_len, WIDTH_A), BF16),
                   jax.ShapeDtypeStruct((n, w_len, KV_W), F32),
                   jax.ShapeDtypeStruct((n, w_len, KV_W), F32)),
        grid=(n // g_blk,),
        in_specs=[
            pl.BlockSpec(memory_space=pltpu.SMEM),
            pl.BlockSpec(blk_q, lambda i: (r0 + i, OFF_QA // WIDTH_A)),
            pl.BlockSpec(blk_q, lambda i: (r0 + i, OFF_ZA // WIDTH_A)),
            pl.BlockSpec((g_blk, t_len, 2 * KV_W), lambda i: (s0 + i, 0, 0)),
            pl.BlockSpec(blk_c, lambda i: (i, 0, 0)),
            pl.BlockSpec(blk_c, lambda i: (i, 0, 0)),
        ],
        out_specs=(pl.BlockSpec(blk_q, lambda i: (i, 0)),
                   pl.BlockSpec(blk_c, lambda i: (i, 0, 0)),
                   pl.BlockSpec(blk_c, lambda i: (i, 0, 0))),
        compiler_params=_cparams(("arbitrary",)),
        name="swa_sample",
    )(sink, p, p, kv3, cache_k, cache_v)


def _cumsum_rows(x):
    c = x.shape[0]
    row = lax.broadcasted_iota(jnp.int32, (c, 1), 0)
    sh = 1
    while sh < c:
        x = x + jnp.where(row >= sh, pltpu.roll(x, sh, 0), 0.0)
        sh *= 2
    return x


def _gla_chunk(q, k, v, la, s_old, causal):
    c = q.shape[0]
    b = _cumsum_rows(la)
    b_last = b[c - 1:c, :]
    decay_col = jnp.transpose(jnp.broadcast_to(jnp.exp(b_last), (LANE, b_last.shape[1])))
    q_t = (q * jnp.exp(b) * (DK_B ** -0.5)).astype(BF16)
    k_t = (k * jnp.exp(-b)).astype(BF16)
    k_d = (k * jnp.exp(b_last - b)).astype(BF16)
    outs, states = [], []
    for h in range(N_HEADS_B):
        dk = slice(h * DK_B, (h + 1) * DK_B)
        dv = slice(h * DV_B, (h + 1) * DV_B)
        att = lax.dot_general(q_t[:, dk], k_t[:, dk], NT_DIMS, preferred_element_type=F32)
        att = jnp.where(causal, att, 0.0).astype(BF16)
        outs.append(jnp.dot(att, v[:, dv], preferred_element_type=F32)
                    + jnp.dot(q_t[:, dk], s_old[h].astype(BF16), preferred_element_type=F32))
        decay = jnp.concatenate([decay_col[dk]] * (DV_B // LANE), axis=1)
        states.append(decay * s_old[h] + lax.dot_general(k_d[:, dk], v[:, dv], TN_DIMS,
                                                         preferred_element_type=F32))
    return outs, states


def _gla_block(q, k, v, la, s_old, causal, n):
    c = q.shape[0] // n
    rows = [slice(i * c, (i + 1) * c) for i in range(n)]
    b = [_cumsum_rows(la[r]) for r in rows]
    t = [bi[c - 1:c, :] for bi in b]
    pre = [None]
    for i in range(n):
        pre.append(t[i] if pre[-1] is None else pre[-1] + t[i])
    decayed = lambda x, e: x if e is None else x * jnp.exp(e)
    scale = DK_B ** -0.5
    q_t = [q[r] * jnp.exp(bi) * scale for r, bi in zip(rows, b)]
    k_t = [(k[r] * jnp.exp(-bi)).astype(BF16) for r, bi in zip(rows, b)]
    k_d = [k[r] * jnp.exp(ti - bi) for r, bi, ti in zip(rows, b, t)]
    q_read = jnp.concatenate([decayed(q_t[i], pre[i]).astype(BF16) for i in range(n)], axis=0)
    k_upd = jnp.concatenate([decayed(k_d[j], None if j == n - 1 else pre[n] - pre[j + 1]).astype(BF16)
                             for j in range(n)], axis=0)
    q_b = [x.astype(BF16) for x in q_t]
    q_x = {(i, j): q_b[i] if i == j + 1 else (q_t[i] * jnp.exp(pre[i] - pre[j + 1])).astype(BF16)
           for j in range(n) for i in range(j + 1, n)}
    k_db = [x.astype(BF16) for x in k_d]
    decay_col = jnp.transpose(jnp.broadcast_to(jnp.exp(pre[n]), (LANE, q.shape[1])))
    nt = lambda x, y: lax.dot_general(x, y, NT_DIMS, preferred_element_type=F32)
    outs, states = [], []
    for h in range(N_HEADS_B):
        dk = slice(h * DK_B, (h + 1) * DK_B)
        dv = slice(h * DV_B, (h + 1) * DV_B)
        att = [nt(q_b[j][:, dk], k_t[j][:, dk]) for j in range(n)]
        cross = {ij: nt(qx[:, dk], k_db[ij[1]][:, dk]) for ij, qx in q_x.items()}
        lhs = [jnp.concatenate([jnp.where(causal, att[j], 0.0)] + [cross[i, j] for i in range(j + 1, n)],
                               axis=0).astype(BF16) for j in range(n)]
        o = (jnp.dot(lhs[0], v[rows[0], dv], preferred_element_type=F32)
             + jnp.dot(q_read[:, dk], s_old[h].astype(BF16), preferred_element_type=F32))
        done = [o[rows[0]]]
        for j in range(1, n):
            o = o[c:] + jnp.dot(lhs[j], v[rows[j], dv], preferred_element_type=F32)
            done.append(o[:c])
        outs.append(jnp.concatenate(done, axis=0))
        decay = jnp.concatenate([decay_col[dk]] * (DV_B // LANE), axis=1)
        states.append(decay * s_old[h] + lax.dot_general(k_upd[:, dk], v[:, dv], TN_DIMS,
                                                         preferred_element_type=F32))
    return outs, states


def _log_decay(r, w2, b2):
    z = jnp.dot(r.astype(BF16), w2, preferred_element_type=F32) + b2
    ls = jnp.minimum(z, 0.0) - jnp.log(1.0 + jnp.exp(-jnp.abs(z)))
    return ls / GLA_TAU


def _gla_finish(o, zb, gain):
    on = o * lax.rsqrt(jnp.mean(o * o, axis=-1, keepdims=True) + EPS) * gain
    return on * _silu(zb)


def _causal_mask(c):
    ri = lax.broadcasted_iota(jnp.int32, (c, c), 0)
    ci = lax.broadcasted_iota(jnp.int32, (c, c), 1)
    return ri >= ci


def _gla_prompt_kernel(nb, n_blk, *refs):
    w2_ref, b2_ref, gain_ref, o_ref, s_ref = refs[5 * nb:]

    @pl.when(pl.program_id(0) == 0)
    def _():
        s_ref[...] = jnp.zeros_like(s_ref)

    c = n_blk * GLA_CHUNK
    causal = _causal_mask(GLA_CHUNK)
    for bi in range(nb):
        q_ref, k_ref, v_ref, zb_ref, r_ref = refs[5 * bi:5 * bi + 5]
        states = [s_ref[bi, h] for h in range(N_HEADS_B)]
        for sub in range(o_ref.shape[1] // c):
            rs = slice(sub * c, (sub + 1) * c)
            la = _log_decay(r_ref[rs, :], w2_ref[...], b2_ref[...])
            outs, states = _gla_block(q_ref[rs, :].astype(F32), k_ref[rs, :].astype(F32), v_ref[rs, :], la,
                                      states, causal, n_blk)
            for h in range(N_HEADS_B):
                dv = slice(h * DV_B, (h + 1) * DV_B)
                o_ref[bi, rs, dv] = _gla_finish(outs[h], zb_ref[rs, dv].astype(F32), gain_ref[:, dv]).astype(BF16)
        for h in range(N_HEADS_B):
            s_ref[bi, h] = states[h]


def _gla_prompt(p, r, w2, b2, gain, b, l, chunks_per_block=4):
    c = GLA_CHUNK * chunks_per_block
    nc = l // c
    in_specs, args = [], []
    for bi in range(b):
        for arr, width, off in ((p, KEY_B, OFF_QB), (p, KEY_B, OFF_KB), (p, WIDTH_B, OFF_VB),
                                (p, WIDTH_B, OFF_ZB), (r, LANE, 0)):
            in_specs.append(pl.BlockSpec((c, width), functools.partial(
                lambda ci, row0, col: (row0 + ci, col), row0=bi * nc, col=off // width)))
            args.append(arr)
    in_specs += [pl.BlockSpec((LANE, KEY_B), lambda ci: (0, 0)),
                 pl.BlockSpec((1, KEY_B), lambda ci: (0, 0)),
                 pl.BlockSpec((1, WIDTH_B), lambda ci: (0, 0))]
    return pl.pallas_call(
        functools.partial(_gla_prompt_kernel, b, chunks_per_block),
        out_shape=(jax.ShapeDtypeStruct((b, l, WIDTH_B), BF16),
                   jax.ShapeDtypeStruct((b, N_HEADS_B, DK_B, DV_B), F32)),
        grid=(nc,),
        in_specs=in_specs,
        out_specs=(pl.BlockSpec((b, c, WIDTH_B), lambda ci: (0, ci, 0)),
                   pl.BlockSpec((b, N_HEADS_B, DK_B, DV_B), lambda ci: (0, 0, 0, 0))),
        compiler_params=_cparams(("arbitrary",)),
        name="gla_prompt",
    )(*args, w2, b2, gain)


def _gla_sample_kernel(g_blk, t_len, q_ref, k_ref, v_ref, zb_ref, r_ref, w2_ref, b2_ref, gain_ref, s0_ref,
                       o_ref, s_ref):
    c = 2 * t_len
    causal = _causal_mask(c)

    def pad(x):
        return jnp.concatenate([x, jnp.zeros_like(x)], axis=0)

    live = lax.broadcasted_iota(jnp.int32, (c, 1), 0) < t_len
    q_all = q_ref[...].astype(F32)
    k_all = k_ref[...].astype(F32)
    v_all = v_ref[...].astype(F32)
    zb_all = zb_ref[...].astype(F32)
    seq_outs = []
    for s in range(g_blk):
        rs = slice(s * t_len, (s + 1) * t_len)
        la = jnp.where(live, _log_decay(pad(r_ref[s]), w2_ref[...], b2_ref[...]), 0.0)
        outs, states = _gla_chunk(pad(q_all[rs]), pad(k_all[rs]), pad(v_all[rs]).astype(BF16), la,
                                  [s0_ref[s, h] for h in range(N_HEADS_B)], causal)
        fin = []
        for h in range(N_HEADS_B):
            dv = slice(h * DV_B, (h + 1) * DV_B)
            s_ref[s, h] = states[h]
            fin.append(_gla_finish(outs[h][:t_len], zb_all[rs, dv], gain_ref[:, dv]))
        seq_outs.append(jnp.concatenate(fin, axis=1))
    o_ref[...] = jnp.concatenate(seq_outs, axis=0).astype(BF16)


def _gla_sample(p, r3, row0, t_len, w2, b2, gain, state, g_blk=4):
    n = state.shape[0]
    rows = g_blk * t_len
    assert rows % BF16_ROWS == 0
    st_blk = (g_blk, N_HEADS_B, DK_B, DV_B)
    r0 = row0 // rows
    return pl.pallas_call(
        functools.partial(_gla_sample_kernel, g_blk, t_len),
        out_shape=(jax.ShapeDtypeStruct((n * t_len, WIDTH_B), BF16),
                   jax.ShapeDtypeStruct(state.shape, F32)),
        grid=(n // g_blk,),
        in_specs=[
            pl.BlockSpec((rows, KEY_B), lambda i: (r0 + i, OFF_QB // KEY_B)),
            pl.BlockSpec((rows, KEY_B), lambda i: (r0 + i, OFF_KB // KEY_B)),
            pl.BlockSpec((rows, WIDTH_B), lambda i: (r0 + i, OFF_VB // WIDTH_B)),
            pl.BlockSpec((rows, WIDTH_B), lambda i: (r0 + i, OFF_ZB // WIDTH_B)),
            pl.BlockSpec((g_blk, t_len, LANE), lambda i: (r0 + i, 0, 0)),
            pl.BlockSpec((LANE, KEY_B), lambda i: (0, 0)),
            pl.BlockSpec((1, KEY_B), lambda i: (0, 0)),
            pl.BlockSpec((1, WIDTH_B), lambda i: (0, 0)),
            pl.BlockSpec(st_blk, lambda i: (i, 0, 0, 0)),
        ],
        out_specs=(pl.BlockSpec((rows, WIDTH_B), lambda i: (i, 0)),
                   pl.BlockSpec(st_blk, lambda i: (i, 0, 0, 0))),
        compiler_params=_cparams(("arbitrary",)),
        name="gla_sample",
    )(p, p, p, p, r3, w2, b2, gain, state)


def _merge_kernel(nchunk, n_ptiles, ap_ref, bp_ref, as_ref, bs_ref, wa_ref, wb_ref, ga_ref, gb_ref, o_ref):
    cw = o_ref.shape[1] // nchunk
    is_prompt = pl.program_id(0) < n_ptiles
    a = jnp.where(is_prompt, ap_ref[...], as_ref[...])
    b = jnp.where(is_prompt, bp_ref[...], bs_ref[...])
    for c in range(nchunk):
        cs = slice(c * cw, (c + 1) * cw)
        ua = jnp.dot(a, wa_ref[:, cs], preferred_element_type=F32)
        ub = jnp.dot(b, wb_ref[:, cs], preferred_element_type=F32)
        merged = (jax.nn.sigmoid(ga_ref[:, cs].astype(F32)) * ua
                  + jax.nn.sigmoid(gb_ref[:, cs].astype(F32)) * ub)
        o_ref[:, cs] = merged.astype(BF16)


def _merge(ap, bp, a_s, b_s, w_pa, w_pb, p, tm):
    n_pt = ap.shape[0] // tm
    n_st = a_s.shape[0] // tm
    d = D_MODEL
    resident = dict(pipeline_mode=pl.Buffered(1))
    pt = lambda i: (jnp.minimum(i, n_pt - 1), 0)
    st = lambda i: (jnp.maximum(i - n_pt, 0), 0)
    return pl.pallas_call(
        functools.partial(_merge_kernel, 4, n_pt),
        out_shape=jax.ShapeDtypeStruct(((n_pt + n_st) * tm, d), BF16),
        grid=(n_pt + n_st,),
        in_specs=[
            pl.BlockSpec((tm, WIDTH_A), pt),
            pl.BlockSpec((tm, WIDTH_B), pt),
            pl.BlockSpec((tm, WIDTH_A), st),
            pl.BlockSpec((tm, WIDTH_B), st),
            pl.BlockSpec((WIDTH_A, d), lambda i: (0, 0), **resident),
            pl.BlockSpec((WIDTH_B, d), lambda i: (0, 0), **resident),
            pl.BlockSpec((tm, d), lambda i: (i, OFF_GA // d)),
            pl.BlockSpec((tm, d), lambda i: (i, OFF_GB // d)),
        ],
        out_specs=pl.BlockSpec((tm, d), lambda i: (i, 0)),
        compiler_params=_cparams(("arbitrary",), VMEM_LIMIT_RESIDENT),
        name="merge",
    )(ap, bp, a_s, b_s, w_pa, w_pb, p, p)


def _out_kernel(nchunk, m_ref, w_ref, x_ref, gate_ref, fg_ref, o_ref):
    g_blk, r_blk, d = x_ref.shape
    cw = d // nchunk
    ssq = jnp.zeros((g_blk, r_blk, 1), F32)
    for c in range(nchunk):
        cs = slice(c * cw, (c + 1) * cw)
        y = jnp.dot(m_ref[...], w_ref[:, cs], preferred_element_type=F32)
        xn = x_ref[:, :, cs] + gate_ref[:, :, cs] * y.reshape(g_blk, r_blk, cw)
        o_ref[:, :, cs] = xn
        ssq = ssq + jnp.sum(xn * xn, axis=-1, keepdims=True)
    inv = lax.rsqrt(ssq * (1.0 / d) + EPS)
    for c in range(nchunk):
        cs = slice(c * cw, (c + 1) * cw)
        o_ref[:, :, cs] = o_ref[:, :, cs] * inv * fg_ref[:, :, cs]


def _out_proj(merged, row0, w_o, x3, gate3, fin_gain, g_blk, r_blk):
    n, l, d = x3.shape
    rows = g_blk * r_blk
    lb = l // r_blk
    t0 = row0 // rows
    return pl.pallas_call(
        functools.partial(_out_kernel, 4),
        out_shape=jax.ShapeDtypeStruct((n, l, d), F32),
        grid=(n // g_blk, lb),
        in_specs=[
            pl.BlockSpec((rows, d), lambda i, r: (t0 + i * lb + r, 0)),
            pl.BlockSpec((d, d), lambda i, r: (0, 0), pipeline_mode=pl.Buffered(1)),
            pl.BlockSpec((g_blk, r_blk, d), lambda i, r: (i, r, 0)),
            pl.BlockSpec((g_blk, 1, d), lambda i, r: (i, 0, 0)),
            pl.BlockSpec((1, 1, d), lambda i, r: (0, 0, 0)),
        ],
        out_specs=pl.BlockSpec((g_blk, r_blk, d), lambda i, r: (i, r, 0)),
        compiler_params=_cparams(("arbitrary", "arbitrary"), VMEM_LIMIT_RESIDENT),
        name="out_proj",
    )(merged, w_o, x3, gate3, fin_gain)


def kernel(x_prompt, x_sample, cache_k_win, cache_v_win, state_gla, c_prompt, c_sample,
           w_ada, b_ada, norm_gain, w_in, attn_sink, w_alpha2, b_alpha, gla_norm_gain,
           w_proj_a, w_proj_b, w_out, final_norm_gain):
    assert w_ada.shape[0] == 1, "single-layer step"
    bp, lp, d = x_prompt.shape
    ns, ts, _ = x_sample.shape
    w_len = cache_k_win.shape[2]
    rows_p = bp * lp
    rows_s = ns * ts
    rows = rows_p + rows_s

    wt = jnp.transpose(w_in[0])
    w_rt = jnp.pad(wt[SRC_MAIN:SRC_GATES], ((0, LANE - GATE_RANK), (0, 0))).astype(BF16)
    w2 = jnp.pad(w_alpha2[0], ((0, LANE - GATE_RANK), (0, 0))).astype(BF16)
    b2 = b_alpha[0].reshape(1, KEY_B)
    gla_gain = gla_norm_gain[0].reshape(1, WIDTH_B)
    cache_t = lambda c: jnp.transpose(c[0], (0, 2, 3, 1)).reshape(ns, KV_W, w_len)
    gain3 = norm_gain[0].reshape(1, 1, d)
    fin3 = final_norm_gain.reshape(1, 1, d)
    sink = attn_sink[0]

    m_all = ns + bp
    m_pad = -(-m_all // 8) * 8
    c_all = jnp.concatenate([c_sample, c_prompt, jnp.zeros((m_pad - m_all, d), F32)], axis=0)
    mod = _adaln(c_all, w_ada[0], b_ada[0].reshape(1, 3 * d))
    shift, scale, gate = mod[:, :d], mod[:, d:2 * d], mod[:, 2 * d:]
    sl_s, sl_p = slice(0, ns), slice(ns, ns + bp)
    as3 = lambda a, n: a.reshape(n, 1, d)

    h, r = _prologue(x_prompt, x_sample, gain3, as3(scale[sl_p], bp), as3(shift[sl_p], bp),
                     as3(scale[sl_s], ns), as3(shift[sl_s], ns), w_rt, 512)
    p, kv, w_pa, w_pb, w_o, cache_k, cache_v = _in_proj(
        h, wt, PROJ_DEST, PROJ_SRC, rows // 4, PROJ_TN, BF16,
        casts=(w_proj_a[0], w_proj_b[0], w_out[0]),
        xposes=(cache_t(cache_k_win), cache_t(cache_v_win)), n_side=ns)
    kv_p = jnp.stack([kv[(bi + 1) * lp - WINDOW:(bi + 1) * lp] for bi in range(bp)])
    kv_p = kv_p.reshape(bp, WINDOW, 2, N_KV, HEAD_DIM)
    k_win_p = kv_p[:, :, 0][None]
    v_win_p = kv_p[:, :, 1][None]
    kv3 = kv.reshape(rows // ts, ts, 2 * KV_W)
    r3 = r.reshape(rows // ts, ts, LANE)

    ua_p = _swa_prompt(sink, p, bp, lp)
    ub_p, s_p = _gla_prompt(p, r, w2, b2, gla_gain, bp, lp)

    ua_s, k_win_s, v_win_s = _swa_sample(sink, p, rows_p, ns, ts, kv3, rows_p // ts, cache_k, cache_v)
    ub_s, s_s = _gla_sample(p, r3, rows_p, ts, w2, b2, gla_gain, state_gla[0])

    merged = _merge(ua_p, ub_p.reshape(rows_p, WIDTH_B), ua_s, ub_s, w_pa, w_pb, p, 256)
    y_p = _out_proj(merged, 0, w_o, x_prompt, as3(gate[sl_p], bp), fin3, 1, 256)
    y_s = _out_proj(merged, rows_p, w_o, x_sample, as3(gate[sl_s], ns), fin3, 256 // ts, ts)

    return (y_p, y_s, k_win_p, v_win_p, s_p[None],
            k_win_s.reshape(1, ns, w_len, N_KV, HEAD_DIM), v_win_s.reshape(1, ns, w_len, N_KV, HEAD_DIM),
            s_s[None])
```

```python
import functools

import jax
import jax.numpy as jnp
from jax import lax
from jax.experimental import pallas as pl
from jax.experimental.pallas import tpu as pltpu

F32 = jnp.float32
BF16 = jnp.bfloat16

D_MODEL = 4096
WIDTH_A = 2048
HEAD_DIM = 64
N_HEADS_A = 32
N_KV = 4
GROUP = 8
WINDOW = 128
ATTN_BLOCK = 128
WIDTH_B = 2048
N_HEADS_B = 4
DV_B = 512
KEY_B = 1024
DK_B = 256
GATE_RANK = 16
GLA_TAU = 16.0
GLA_CHUNK = 64
EPS = 1e-6
NEG_INF = -1e30
KV_W = N_KV * HEAD_DIM

V7X_VMEM_BYTES = 64 * 1024 * 1024
VMEM_LIMIT = V7X_VMEM_BYTES - 8 * 1024 * 1024
VMEM_LIMIT_RESIDENT = V7X_VMEM_BYTES - 4 * 1024 * 1024
LANE = 128
BF16_ROWS = 16

SRC_KA = WIDTH_A
SRC_MAIN = WIDTH_A + 2 * KV_W + WIDTH_A + 2 * KEY_B + 2 * WIDTH_B
SRC_GATES = SRC_MAIN + GATE_RANK
PROJ_TN = 512
OFF_GA = 0
OFF_GB = 4096
OFF_QA = 8192
OFF_ZA = 10240
OFF_VB = 12288
OFF_ZB = 14336
OFF_QB = 16384
OFF_KB = 17408
OFF_KA = 18432
OFF_VA = 18688
PROJ_COLS = 18944
_PROJ_TILES = ([(j * PROJ_TN, 16 + t) for j, t in zip((0, 1, 2, 3, 5, 6, 7, 8, 9, 10, 11, 12, 13, 14, 15, 16,
                                                        17, 18, 19, 20),
                                                       (0, 1, 2, 3, 4, 5, 6, 7, 16, 17, 18, 19, 8, 9, 10, 11,
                                                        12, 13, 14, 15))]
               + [(SRC_GATES + j * PROJ_TN, j) for j in range(16)]
               + [(SRC_KA, 16 + 20)])
PROJ_SRC = tuple(s for s, _ in _PROJ_TILES)
PROJ_DEST = tuple(d for _, d in _PROJ_TILES)

NT_DIMS = (((1,), (1,)), ((), ()))
TN_DIMS = (((0,), (0,)), ((), ()))


def _cparams(sem, vmem_limit=VMEM_LIMIT):
    return pltpu.CompilerParams(dimension_semantics=sem, vmem_limit_bytes=vmem_limit)


def _silu(x):
    return x * jax.nn.sigmoid(x)


def _adaln_kernel(c_ref, w_ref, b_ref, o_ref):
    a = _silu(c_ref[...]).astype(BF16)
    w = w_ref[...].astype(BF16)
    o_ref[...] = jnp.dot(a, w, preferred_element_type=F32) + b_ref[...]


def _adaln(c_all, w_ada, b_ada, tn=512):
    m, k = c_all.shape
    n = w_ada.shape[1]
    return pl.pallas_call(
        _adaln_kernel,
        out_shape=jax.ShapeDtypeStruct((m, n), F32),
        grid=(n // tn,),
        in_specs=[
            pl.BlockSpec((m, k), lambda j: (0, 0)),
            pl.BlockSpec((k, tn), lambda j: (0, j)),
            pl.BlockSpec((1, tn), lambda j: (0, j)),
        ],
        out_specs=pl.BlockSpec((m, tn), lambda j: (0, j)),
        compiler_params=_cparams(("arbitrary",)),
        name="adaln",
    )(c_all, w_ada, b_ada)


def _prologue_kernel(n_ptiles, xp_ref, xs_ref, g_ref, scp_ref, shp_ref, scs_ref, shs_ref, wr_ref,
                     h_ref, r_ref):
    i = pl.program_id(0)

    def emit(x, scale, shift):
        ms = jnp.mean(x * x, axis=-1, keepdims=True)
        y = x * lax.rsqrt(ms + EPS) * g_ref[...]
        h = y * (1.0 + scale) + shift
        hb = h.reshape(h.shape[0] * h.shape[1], h.shape[2]).astype(BF16)
        h_ref[...] = hb
        r_ref[...] = lax.dot_general(hb, wr_ref[...], NT_DIMS, preferred_element_type=F32)

    @pl.when(i < n_ptiles)
    def _():
        emit(xp_ref[...], scp_ref[...], shp_ref[...])

    @pl.when(i >= n_ptiles)
    def _():
        emit(xs_ref[...], scs_ref[...], shs_ref[...])


def _prologue(xp, xs, gain, scp, shp, scs, shs, w_rt, rows):
    b, l, d = xp.shape
    n, t, _ = xs.shape
    lb = l // rows
    n_pt = b * lb
    sg = rows // t
    n_st = n // sg
    pt = lambda i: jnp.minimum(i, n_pt - 1)
    st = lambda i: jnp.maximum(i - n_pt, 0)
    return pl.pallas_call(
        functools.partial(_prologue_kernel, n_pt),
        out_shape=(jax.ShapeDtypeStruct((b * l + n * t, d), BF16),
                   jax.ShapeDtypeStruct((b * l + n * t, LANE), F32)),
        grid=(n_pt + n_st,),
        in_specs=[
            pl.BlockSpec((1, rows, d), lambda i: (pt(i) // lb, pt(i) % lb, 0)),
            pl.BlockSpec((sg, t, d), lambda i: (st(i), 0, 0)),
            pl.BlockSpec((1, 1, d), lambda i: (0, 0, 0)),
            pl.BlockSpec((1, 1, d), lambda i: (pt(i) // lb, 0, 0)),
            pl.BlockSpec((1, 1, d), lambda i: (pt(i) // lb, 0, 0)),
            pl.BlockSpec((sg, 1, d), lambda i: (st(i), 0, 0)),
            pl.BlockSpec((sg, 1, d), lambda i: (st(i), 0, 0)),
            pl.BlockSpec((LANE, d), lambda i: (0, 0)),
        ],
        out_specs=(pl.BlockSpec((rows, d), lambda i: (i, 0)),
                   pl.BlockSpec((rows, LANE), lambda i: (i, 0))),
        compiler_params=_cparams(("arbitrary",)),
        name="prologue",
    )(xp, xs, gain, scp, shp, scs, shs, w_rt)


def _inproj_kernel(n_cast, n_xpose, dest_ref, src_ref, a_ref, w_ref, *refs):
    del dest_ref, src_ref
    n_in = n_cast + n_xpose
    side_in, o_ref, last_ref, side_out = refs[:n_in], refs[n_in], refs[n_in + 1], refs[n_in + 2:]
    w = w_ref[...].astype(BF16)
    acc = lax.dot_general(a_ref[...], w, NT_DIMS, preferred_element_type=F32)
    o_ref[...] = acc.astype(o_ref.dtype)
    last_ref[...] = acc

    for src, dst in zip(side_in[:n_cast], side_out[:n_cast]):
        dst[...] = src[...].astype(dst.dtype)
    for src, dst in zip(side_in[n_cast:], side_out[n_cast:]):
        dst[0] = jnp.transpose(src[0])


def _in_proj(a, wt, dest, src, tm, tn, out_dtype, casts, xposes, n_side):
    m, k = a.shape
    nt = len(dest)
    assert (m // tm) * nt >= n_side
    chunk = lambda i, j, d, s: jnp.minimum(i * nt + j, n_side - 1)
    side_in, side_out, side_shapes = [], [], []
    for arr in casts:
        blk = (arr.shape[0] // n_side, arr.shape[1])
        assert blk[0] % BF16_ROWS == 0
        spec = pl.BlockSpec(blk, lambda i, j, d, s: (chunk(i, j, d, s), 0))
        side_in.append(spec)
        side_out.append(spec)
        side_shapes.append(jax.ShapeDtypeStruct(arr.shape, BF16))
    for arr in xposes:
        _, r, c = arr.shape
        side_in.append(pl.BlockSpec((1, r, c), lambda i, j, d, s: (chunk(i, j, d, s), 0, 0)))
        side_out.append(pl.BlockSpec((1, c, r), lambda i, j, d, s: (chunk(i, j, d, s), 0, 0)))
        side_shapes.append(jax.ShapeDtypeStruct((n_side, c, r), arr.dtype))
    return pl.pallas_call(
        functools.partial(_inproj_kernel, len(casts), len(xposes)),
        out_shape=[jax.ShapeDtypeStruct((m, nt * tn), out_dtype), jax.ShapeDtypeStruct((m, tn), F32)] + side_shapes,
        grid_spec=pltpu.PrefetchScalarGridSpec(
            num_scalar_prefetch=2,
            grid=(m // tm, nt),
            in_specs=[
                pl.BlockSpec((tm, k), lambda i, j, d, s: (i, 0), pipeline_mode=pl.Buffered(1)),
                pl.BlockSpec((pl.Element(tn), pl.Element(k)),
                             lambda i, j, d, s: (pl.multiple_of(s[j], GATE_RANK), 0)),
            ] + side_in,
            out_specs=[pl.BlockSpec((tm, tn), lambda i, j, d, s: (i, d[j])),
                       pl.BlockSpec((tm, tn), lambda i, j, d, s: (i, 0))] + side_out,
        ),
        compiler_params=_cparams(("arbitrary", "arbitrary"), VMEM_LIMIT_RESIDENT),
        name="in_proj",
    )(jnp.asarray(dest, jnp.int32), jnp.asarray(src, jnp.int32), a, wt, *casts, *xposes)


def _swa_prompt_kernel(sink_ref, q_ref, za_ref, kp_ref, kc_ref, vp_ref, vc_ref, o_ref):
    n = pl.program_id(1)
    blk = ATTN_BLOCK
    pairs = GROUP // 2
    pw = 2 * HEAD_DIM
    hw = GROUP * HEAD_DIM
    k = jnp.concatenate([kp_ref[...], kc_ref[...]], axis=0)
    v = jnp.concatenate([vp_ref[...], vc_ref[...]], axis=0)
    rows = pairs * blk
    i = lax.broadcasted_iota(jnp.int32, (rows, 2 * blk), 0) % blk
    j = lax.broadcasted_iota(jnp.int32, (rows, 2 * blk), 1)
    visible = (j >= i) & (j <= i + WINDOW) & ((j >= blk) | (n > 0))
    bias = jnp.where(visible, 0.0, NEG_INF)
    pair_id = lax.broadcasted_iota(jnp.int32, (rows, 1), 0) // blk
    zeros = jnp.zeros((2 * blk, HEAD_DIM), BF16)
    for h in range(N_KV):
        kh = k[:, h * HEAD_DIM:(h + 1) * HEAD_DIM] * (HEAD_DIM ** -0.5)
        vh = v[:, h * HEAD_DIM:(h + 1) * HEAD_DIM]
        q2 = jnp.concatenate([q_ref[:, h * hw + jj * pw:h * hw + (jj + 1) * pw] for jj in range(pairs)], axis=0)
        acc = None
        k_pads = [jnp.concatenate([kh, zeros], axis=1), jnp.concatenate([zeros, kh], axis=1)]
        scores = [lax.dot_general(q2, kp, NT_DIMS, preferred_element_type=F32) + bias for kp in k_pads]
        probs = []
        for half in range(2):
            sink = jnp.zeros((rows, 1), F32)
            for jj in range(pairs):
                sink = jnp.where(pair_id == jj, sink_ref[h * GROUP + 2 * jj + half], sink)
            s = scores[half]
            m = jnp.maximum(jnp.max(s, axis=-1, keepdims=True), sink)
            p = jnp.exp(s - m)
            denom = jnp.sum(p, axis=-1, keepdims=True) + jnp.exp(sink - m)
            probs.append((p.astype(BF16), denom))
        for half in range(2):
            kv_parts = (lambda t: [t, zeros]) if half == 0 else (lambda t: [zeros, t])
            v_pad = jnp.concatenate(kv_parts(vh), axis=1)
            pb, denom = probs[half]
            o = jnp.dot(pb, v_pad, preferred_element_type=F32) / denom
            acc = o if acc is None else acc + o
        o_h = jnp.concatenate([acc[jj * blk:(jj + 1) * blk] for jj in range(pairs)], axis=1)
        za = za_ref[:, h * hw:(h + 1) * hw].astype(F32)
        o_ref[:, h * hw:(h + 1) * hw] = (o_h * _silu(za)).astype(BF16)


def _swa_prompt(sink, p, b, l):
    nb = l // ATTN_BLOCK
    kcol = OFF_KA // KV_W
    vcol = OFF_VA // KV_W
    blk_q = (ATTN_BLOCK, WIDTH_A)
    blk_kv = (ATTN_BLOCK, KV_W)
    cur = lambda bi, n: bi * nb + n
    prev = lambda bi, n: bi * nb + jnp.maximum(n - 1, 0)
    return pl.pallas_call(
        _swa_prompt_kernel,
        out_shape=jax.ShapeDtypeStruct((b * l, WIDTH_A), BF16),
        grid=(b, nb),
        in_specs=[
            pl.BlockSpec(memory_space=pltpu.SMEM),
            pl.BlockSpec(blk_q, lambda bi, n: (cur(bi, n), OFF_QA // WIDTH_A)),
            pl.BlockSpec(blk_q, lambda bi, n: (cur(bi, n), OFF_ZA // WIDTH_A)),
            pl.BlockSpec(blk_kv, lambda bi, n: (prev(bi, n), kcol)),
            pl.BlockSpec(blk_kv, lambda bi, n: (cur(bi, n), kcol)),
            pl.BlockSpec(blk_kv, lambda bi, n: (prev(bi, n), vcol)),
            pl.BlockSpec(blk_kv, lambda bi, n: (cur(bi, n), vcol)),
        ],
        out_specs=pl.BlockSpec(blk_q, lambda bi, n: (cur(bi, n), 0)),
        compiler_params=_cparams(("arbitrary", "arbitrary")),
        name="swa_prompt",
    )(sink, p, p, p, p, p, p)


def _swa_sample_kernel(g_blk, t_len, w_len, sink_ref, q_ref, za_ref, kv_ref, kc_ref, vc_ref,
                       o_ref, kw_ref, vw_ref):
    kpad = 2 * WINDOW - w_len - t_len
    nkeys = 2 * WINDOW
    rows = GROUP * t_len
    r = lax.broadcasted_iota(jnp.int32, (rows, nkeys), 0)
    j = lax.broadcasted_iota(jnp.int32, (rows, nkeys), 1)
    t = r % t_len
    visible = (j <= w_len + t) & (j >= w_len + t - WINDOW)
    bias = jnp.where(visible, 0.0, NEG_INF)[None]
    rg = lax.broadcasted_iota(jnp.int32, (rows, 1), 0) // t_len
    sink_cols = []
    for h in range(N_KV):
        col = jnp.zeros((rows, 1), F32)
        for g in range(GROUP):
            col = jnp.where(rg == g, sink_ref[h * GROUP + g], col)
        sink_cols.append(col)
    sink = jnp.concatenate(sink_cols * g_blk, axis=0).reshape(g_blk * N_KV, rows, 1)
    zpad = jnp.zeros((kpad, KV_W), F32)
    q_all = q_ref[...].astype(F32)
    za_all = za_ref[...].astype(F32)

    scores, values = [], []
    for s in range(g_blk):
        kc = kc_ref[s]
        vc = vc_ref[s]
        kn = kv_ref[s, :, :KV_W]
        vn = kv_ref[s, :, KV_W:]
        kw_ref[s] = jnp.concatenate([kc[t_len:], kn], axis=0)
        vw_ref[s] = jnp.concatenate([vc[t_len:], vn], axis=0)
        k_all = jnp.concatenate([kc, kn, zpad], axis=0).astype(BF16) * (HEAD_DIM ** -0.5)
        v_all = jnp.concatenate([vc, vn, zpad], axis=0).astype(BF16)
        q = q_all[s * t_len:(s + 1) * t_len]
        for h in range(N_KV):
            qh = jnp.concatenate(
                [q[:, (h * GROUP + g) * HEAD_DIM:(h * GROUP + g + 1) * HEAD_DIM] for g in range(GROUP)],
                axis=0).astype(BF16)
            scores.append(lax.dot_general(qh, k_all[:, h * HEAD_DIM:(h + 1) * HEAD_DIM], NT_DIMS,
                                          preferred_element_type=F32))
            values.append(v_all[:, h * HEAD_DIM:(h + 1) * HEAD_DIM])

    sc = jnp.concatenate(scores, axis=0).reshape(g_blk * N_KV, rows, nkeys) + bias
    m = jnp.maximum(jnp.max(sc, axis=-1, keepdims=True), sink)
    p = jnp.exp(sc - m)
    inv = 1.0 / (jnp.sum(p, axis=-1, keepdims=True) + jnp.exp(sink - m))
    pb = p.astype(BF16)

    seq_outs = []
    for s in range(g_blk):
        pieces = []
        for h in range(N_KV):
            c = s * N_KV + h
            oh = jnp.dot(pb[c], values[c], preferred_element_type=F32) * inv[c]
            for g in range(GROUP):
                pieces.append(oh[g * t_len:(g + 1) * t_len])
        o = jnp.concatenate(pieces, axis=1)
        seq_outs.append(o * _silu(za_all[s * t_len:(s + 1) * t_len]))
    o_ref[...] = jnp.concatenate(seq_outs, axis=0).astype(BF16)


def _swa_sample(sink, p, row0, n, t_len, kv3, kv_seq0, cache_k, cache_v, g_blk=8):
    rows = g_blk * t_len
    assert rows % BF16_ROWS == 0
    w_len = cache_k.shape[1]
    blk_q = (rows, WIDTH_A)
    blk_c = (g_blk, w_len, KV_W)
    r0 = row0 // rows
    s0 = kv_seq0 // g_blk
    return pl.pallas_call(
        functools.partial(_swa_sample_kernel, g_blk, t_len, w_len),
        out_shape=(jax.ShapeDtypeStruct((n * t_len, WIDTH_A), BF16),
                   jax.ShapeDtypeStruct((n, w_len, KV_W), F32),
                   jax.ShapeDtypeStruct((n, w_len, KV_W), F32)),
        grid=(n // g_blk,),
        in_specs=[
            pl.BlockSpec(memory_space=pltpu.SMEM),
            pl.BlockSpec(blk_q, lambda i: (r0 + i, OFF_QA // WIDTH_A)),
            pl.BlockSpec(blk_q, lambda i: (r0 + i, OFF_ZA // WIDTH_A)),
            pl.BlockSpec((g_blk, t_len, 2 * KV_W), lambda i: (s0 + i, 0, 0)),
            pl.BlockSpec(blk_c, lambda i: (i, 0, 0)),
            pl.BlockSpec(blk_c, lambda i: (i, 0, 0)),
        ],
        out_specs=(pl.BlockSpec(blk_q, lambda i: (i, 0)),
                   pl.BlockSpec(blk_c, lambda i: (i, 0, 0)),
                   pl.BlockSpec(blk_c, lambda i: (i, 0, 0))),
        compiler_params=_cparams(("arbitrary",)),
        name="swa_sample",
    )(sink, p, p, kv3, cache_k, cache_v)


def _cumsum_rows(x):
    c = x.shape[0]
    row = lax.broadcasted_iota(jnp.int32, (c, 1), 0)
    sh = 1
    while sh < c:
        x = x + jnp.where(row >= sh, pltpu.roll(x, sh, 0), 0.0)
        sh *= 2
    return x


def _gla_chunk(q, k, v, la, s_old, causal):
    c = q.shape[0]
    b = _cumsum_rows(la)
    b_last = b[c - 1:c, :]
    decay_col = jnp.transpose(jnp.broadcast_to(jnp.exp(b_last), (LANE, b_last.shape[1])))
    q_t = (q * jnp.exp(b) * (DK_B ** -0.5)).astype(BF16)
    k_t = (k * jnp.exp(-b)).astype(BF16)
    k_d = (k * jnp.exp(b_last - b)).astype(BF16)
    outs, states = [], []
    for h in range(N_HEADS_B):
        dk = slice(h * DK_B, (h + 1) * DK_B)
        dv = slice(h * DV_B, (h + 1) * DV_B)
        att = lax.dot_general(q_t[:, dk], k_t[:, dk], NT_DIMS, preferred_element_type=F32)
        att = jnp.where(causal, att, 0.0).astype(BF16)
        outs.append(jnp.dot(att, v[:, dv], preferred_element_type=F32)
                    + jnp.dot(q_t[:, dk], s_old[h].astype(BF16), preferred_element_type=F32))
        decay = jnp.concatenate([decay_col[dk]] * (DV_B // LANE), axis=1)
        states.append(decay * s_old[h] + lax.dot_general(k_d[:, dk], v[:, dv], TN_DIMS,
                                                         preferred_element_type=F32))
    return outs, states


def _gla_block(q, k, v, la, s_old, causal, n):
    c = q.shape[0] // n
    rows = [slice(i * c, (i + 1) * c) for i in range(n)]
    b = [_cumsum_rows(la[r]) for r in rows]
    t = [bi[c - 1:c, :] for bi in b]
    pre = [None]
    for i in range(n):
        pre.append(t[i] if pre[-1] is None else pre[-1] + t[i])
    decayed = lambda x, e: x if e is None else x * jnp.exp(e)
    scale = DK_B ** -0.5
    q_t = [q[r] * jnp.exp(bi) * scale for r, bi in zip(rows, b)]
    k_t = [(k[r] * jnp.exp(-bi)).astype(BF16) for r, bi in zip(rows, b)]
    k_d = [k[r] * jnp.exp(ti - bi) for r, bi, ti in zip(rows, b, t)]
    q_read = jnp.concatenate([decayed(q_t[i], pre[i]).astype(BF16) for i in range(n)], axis=0)
    k_upd = jnp.concatenate([decayed(k_d[j], None if j == n - 1 else pre[n] - pre[j + 1]).astype(BF16)
                             for j in range(n)], axis=0)
    q_b = [x.astype(BF16) for x in q_t]
    q_x = {(i, j): q_b[i] if i == j + 1 else (q_t[i] * jnp.exp(pre[i] - pre[j + 1])).astype(BF16)
           for j in range(n) for i in range(j + 1, n)}
    k_db = [x.astype(BF16) for x in k_d]
    decay_col = jnp.transpose(jnp.broadcast_to(jnp.exp(pre[n]), (LANE, q.shape[1])))
    nt = lambda x, y: lax.dot_general(x, y, NT_DIMS, preferred_element_type=F32)
    outs, states = [], []
    for h in range(N_HEADS_B):
        dk = slice(h * DK_B, (h + 1) * DK_B)
        dv = slice(h * DV_B, (h + 1) * DV_B)
        att = [nt(q_b[j][:, dk], k_t[j][:, dk]) for j in range(n)]
        cross = {ij: nt(qx[:, dk], k_db[ij[1]][:, dk]) for ij, qx in q_x.items()}
        lhs = [jnp.concatenate([jnp.where(causal, att[j], 0.0)] + [cross[i, j] for i in range(j + 1, n)],
                               axis=0).astype(BF16) for j in range(n)]
        o = (jnp.dot(lhs[0], v[rows[0], dv], preferred_element_type=F32)
             + jnp.dot(q_read[:, dk], s_old[h].astype(BF16), preferred_element_type=F32))
        done = [o[rows[0]]]
        for j in range(1, n):
            o = o[c:] + jnp.dot(lhs[j], v[rows[j], dv], preferred_element_type=F32)
            done.append(o[:c])
        outs.append(jnp.concatenate(done, axis=0))
        decay = jnp.concatenate([decay_col[dk]] * (DV_B // LANE), axis=1)
        states.append(decay * s_old[h] + lax.dot_general(k_upd[:, dk], v[:, dv], TN_DIMS,
                                                         preferred_element_type=F32))
    return outs, states


def _log_decay(r, w2, b2):
    z = jnp.dot(r.astype(BF16), w2, preferred_element_type=F32) + b2
    ls = jnp.minimum(z, 0.0) - jnp.log(1.0 + jnp.exp(-jnp.abs(z)))
    return ls / GLA_TAU


def _gla_finish(o, zb, gain):
    on = o * lax.rsqrt(jnp.mean(o * o, axis=-1, keepdims=True) + EPS) * gain
    return on * _silu(zb)


def _causal_mask(c):
    ri = lax.broadcasted_iota(jnp.int32, (c, c), 0)
    ci = lax.broadcasted_iota(jnp.int32, (c, c), 1)
    return ri >= ci


def _gla_prompt_kernel(nb, n_blk, *refs):
    w2_ref, b2_ref, gain_ref, o_ref, s_ref = refs[5 * nb:]

    @pl.when(pl.program_id(0) == 0)
    def _():
        s_ref[...] = jnp.zeros_like(s_ref)

    c = n_blk * GLA_CHUNK
    causal = _causal_mask(GLA_CHUNK)
    for bi in range(nb):
        q_ref, k_ref, v_ref, zb_ref, r_ref = refs[5 * bi:5 * bi + 5]
        states = [s_ref[bi, h] for h in range(N_HEADS_B)]
        for sub in range(o_ref.shape[1] // c):
            rs = slice(sub * c, (sub + 1) * c)
            la = _log_decay(r_ref[rs, :], w2_ref[...], b2_ref[...])
            outs, states = _gla_block(q_ref[rs, :].astype(F32), k_ref[rs, :].astype(F32), v_ref[rs, :], la,
                                      states, causal, n_blk)
            for h in range(N_HEADS_B):
                dv = slice(h * DV_B, (h + 1) * DV_B)
                o_ref[bi, rs, dv] = _gla_finish(outs[h], zb_ref[rs, dv].astype(F32), gain_ref[:, dv]).astype(BF16)
        for h in range(N_HEADS_B):
            s_ref[bi, h] = states[h]


def _gla_prompt(p, r, w2, b2, gain, b, l, chunks_per_block=4):
    c = GLA_CHUNK * chunks_per_block
    nc = l // c
    in_specs, args = [], []
    for bi in range(b):
        for arr, width, off in ((p, KEY_B, OFF_QB), (p, KEY_B, OFF_KB), (p, WIDTH_B, OFF_VB),
                                (p, WIDTH_B, OFF_ZB), (r, LANE, 0)):
            in_specs.append(pl.BlockSpec((c, width), functools.partial(
                lambda ci, row0, col: (row0 + ci, col), row0=bi * nc, col=off // width)))
            args.append(arr)
    in_specs += [pl.BlockSpec((LANE, KEY_B), lambda ci: (0, 0)),
                 pl.BlockSpec((1, KEY_B), lambda ci: (0, 0)),
                 pl.BlockSpec((1, WIDTH_B), lambda ci: (0, 0))]
    return pl.pallas_call(
        functools.partial(_gla_prompt_kernel, b, chunks_per_block),
        out_shape=(jax.ShapeDtypeStruct((b, l, WIDTH_B), BF16),
                   jax.ShapeDtypeStruct((b, N_HEADS_B, DK_B, DV_B), F32)),
        grid=(nc,),
        in_specs=in_specs,
        out_specs=(pl.BlockSpec((b, c, WIDTH_B), lambda ci: (0, ci, 0)),
                   pl.BlockSpec((b, N_HEADS_B, DK_B, DV_B), lambda ci: (0, 0, 0, 0))),
        compiler_params=_cparams(("arbitrary",)),
        name="gla_prompt",
    )(*args, w2, b2, gain)


def _gla_sample_kernel(g_blk, t_len, q_ref, k_ref, v_ref, zb_ref, r_ref, w2_ref, b2_ref, gain_ref, s0_ref,
                       o_ref, s_ref):
    c = 2 * t_len
    causal = _causal_mask(c)

    def pad(x):
        return jnp.concatenate([x, jnp.zeros_like(x)], axis=0)

    live = lax.broadcasted_iota(jnp.int32, (c, 1), 0) < t_len
    q_all = q_ref[...].astype(F32)
    k_all = k_ref[...].astype(F32)
    v_all = v_ref[...].astype(F32)
    zb_all = zb_ref[...].astype(F32)
    seq_outs = []
    for s in range(g_blk):
        rs = slice(s * t_len, (s + 1) * t_len)
        la = jnp.where(live, _log_decay(pad(r_ref[s]), w2_ref[...], b2_ref[...]), 0.0)
        outs, states = _gla_chunk(pad(q_all[rs]), pad(k_all[rs]), pad(v_all[rs]).astype(BF16), la,
                                  [s0_ref[s, h] for h in range(N_HEADS_B)], causal)
        fin = []
        for h in range(N_HEADS_B):
            dv = slice(h * DV_B, (h + 1) * DV_B)
            s_ref[s, h] = states[h]
            fin.append(_gla_finish(outs[h][:t_len], zb_all[rs, dv], gain_ref[:, dv]))
        seq_outs.append(jnp.concatenate(fin, axis=1))
    o_ref[...] = jnp.concatenate(seq_outs, axis=0).astype(BF16)


def _gla_sample(p, r3, row0, t_len, w2, b2, gain, state, g_blk=4):
    n = state.shape[0]
    rows = g_blk * t_len
    assert rows % BF16_ROWS == 0
    st_blk = (g_blk, N_HEADS_B, DK_B, DV_B)
    r0 = row0 // rows
    return pl.pallas_call(
        functools.partial(_gla_sample_kernel, g_blk, t_len),
        out_shape=(jax.ShapeDtypeStruct((n * t_len, WIDTH_B), BF16),
                   jax.ShapeDtypeStruct(state.shape, F32)),
        grid=(n // g_blk,),
        in_specs=[
            pl.BlockSpec((rows, KEY_B), lambda i: (r0 + i, OFF_QB // KEY_B)),
            pl.BlockSpec((rows, KEY_B), lambda i: (r0 + i, OFF_KB // KEY_B)),
            pl.BlockSpec((rows, WIDTH_B), lambda i: (r0 + i, OFF_VB // WIDTH_B)),
            pl.BlockSpec((rows, WIDTH_B), lambda i: (r0 + i, OFF_ZB // WIDTH_B)),
            pl.BlockSpec((g_blk, t_len, LANE), lambda i: (r0 + i, 0, 0)),
            pl.BlockSpec((LANE, KEY_B), lambda i: (0, 0)),
            pl.BlockSpec((1, KEY_B), lambda i: (0, 0)),
            pl.BlockSpec((1, WIDTH_B), lambda i: (0, 0)),
            pl.BlockSpec(st_blk, lambda i: (i, 0, 0, 0)),
        ],
        out_specs=(pl.BlockSpec((rows, WIDTH_B), lambda i: (i, 0)),
                   pl.BlockSpec(st_blk, lambda i: (i, 0, 0, 0))),
        compiler_params=_cparams(("arbitrary",)),
        name="gla_sample",
    )(p, p, p, p, r3, w2, b2, gain, state)


def _merge_kernel(nchunk, n_ptiles, ap_ref, bp_ref, as_ref, bs_ref, wa_ref, wb_ref, ga_ref, gb_ref, o_ref):
    cw = o_ref.shape[1] // nchunk
    is_prompt = pl.program_id(0) < n_ptiles
    a = jnp.where(is_prompt, ap_ref[...], as_ref[...])
    b = jnp.where(is_prompt, bp_ref[...], bs_ref[...])
    for c in range(nchunk):
        cs = slice(c * cw, (c + 1) * cw)
        ua = jnp.dot(a, wa_ref[:, cs], preferred_element_type=F32)
        ub = jnp.dot(b, wb_ref[:, cs], preferred_element_type=F32)
        merged = (jax.nn.sigmoid(ga_ref[:, cs].astype(F32)) * ua
                  + jax.nn.sigmoid(gb_ref[:, cs].astype(F32)) * ub)
        o_ref[:, cs] = merged.astype(BF16)


def _merge(ap, bp, a_s, b_s, w_pa, w_pb, p, tm):
    n_pt = ap.shape[0] // tm
    n_st = a_s.shape[0] // tm
    d = D_MODEL
    resident = dict(pipeline_mode=pl.Buffered(1))
    pt = lambda i: (jnp.minimum(i, n_pt - 1), 0)
    st = lambda i: (jnp.maximum(i - n_pt, 0), 0)
    return pl.pallas_call(
        functools.partial(_merge_kernel, 4, n_pt),
        out_shape=jax.ShapeDtypeStruct(((n_pt + n_st) * tm, d), BF16),
        grid=(n_pt + n_st,),
        in_specs=[
            pl.BlockSpec((tm, WIDTH_A), pt),
            pl.BlockSpec((tm, WIDTH_B), pt),
            pl.BlockSpec((tm, WIDTH_A), st),
            pl.BlockSpec((tm, WIDTH_B), st),
            pl.BlockSpec((WIDTH_A, d), lambda i: (0, 0), **resident),
            pl.BlockSpec((WIDTH_B, d), lambda i: (0, 0), **resident),
            pl.BlockSpec((tm, d), lambda i: (i, OFF_GA // d)),
            pl.BlockSpec((tm, d), lambda i: (i, OFF_GB // d)),
        ],
        out_specs=pl.BlockSpec((tm, d), lambda i: (i, 0)),
        compiler_params=_cparams(("arbitrary",), VMEM_LIMIT_RESIDENT),
        name="merge",
    )(ap, bp, a_s, b_s, w_pa, w_pb, p, p)


def _out_kernel(nchunk, m_ref, w_ref, x_ref, gate_ref, fg_ref, o_ref):
    g_blk, r_blk, d = x_ref.shape
    cw = d // nchunk
    ssq = jnp.zeros((g_blk, r_blk, 1), F32)
    for c in range(nchunk):
        cs = slice(c * cw, (c + 1) * cw)
        y = jnp.dot(m_ref[...], w_ref[:, cs], preferred_element_type=F32)
        xn = x_ref[:, :, cs] + gate_ref[:, :, cs] * y.reshape(g_blk, r_blk, cw)
        o_ref[:, :, cs] = xn
        ssq = ssq + jnp.sum(xn * xn, axis=-1, keepdims=True)
    inv = lax.rsqrt(ssq * (1.0 / d) + EPS)
    for c in range(nchunk):
        cs = slice(c * cw, (c + 1) * cw)
        o_ref[:, :, cs] = o_ref[:, :, cs] * inv * fg_ref[:, :, cs]


def _out_proj(merged, row0, w_o, x3, gate3, fin_gain, g_blk, r_blk):
    n, l, d = x3.shape
    rows = g_blk * r_blk
    lb = l // r_blk
    t0 = row0 // rows
    return pl.pallas_call(
        functools.partial(_out_kernel, 4),
        out_shape=jax.ShapeDtypeStruct((n, l, d), F32),
        grid=(n // g_blk, lb),
        in_specs=[
            pl.BlockSpec((rows, d), lambda i, r: (t0 + i * lb + r, 0)),
            pl.BlockSpec((d, d), lambda i, r: (0, 0), pipeline_mode=pl.Buffered(1)),
            pl.BlockSpec((g_blk, r_blk, d), lambda i, r: (i, r, 0)),
            pl.BlockSpec((g_blk, 1, d), lambda i, r: (i, 0, 0)),
            pl.BlockSpec((1, 1, d), lambda i, r: (0, 0, 0)),
        ],
        out_specs=pl.BlockSpec((g_blk, r_blk, d), lambda i, r: (i, r, 0)),
        compiler_params=_cparams(("arbitrary", "arbitrary"), VMEM_LIMIT_RESIDENT),
        name="out_proj",
    )(merged, w_o, x3, gate3, fin_gain)


def kernel(x_prompt, x_sample, cache_k_win, cache_v_win, state_gla, c_prompt, c_sample,
           w_ada, b_ada, norm_gain, w_in, attn_sink, w_alpha2, b_alpha, gla_norm_gain,
           w_proj_a, w_proj_b, w_out, final_norm_gain):
    assert w_ada.shape[0] == 1, "single-layer step"
    bp, lp, d = x_prompt.shape
    ns, ts, _ = x_sample.shape
    w_len = cache_k_win.shape[2]
    rows_p = bp * lp
    rows_s = ns * ts
    rows = rows_p + rows_s

    wt = jnp.transpose(w_in[0])
    w_rt = jnp.pad(wt[SRC_MAIN:SRC_GATES], ((0, LANE - GATE_RANK), (0, 0))).astype(BF16)
    w2 = jnp.pad(w_alpha2[0], ((0, LANE - GATE_RANK), (0, 0))).astype(BF16)
    b2 = b_alpha[0].reshape(1, KEY_B)
    gla_gain = gla_norm_gain[0].reshape(1, WIDTH_B)
    cache_t = lambda c: jnp.transpose(c[0], (0, 2, 3, 1)).reshape(ns, KV_W, w_len)
    gain3 = norm_gain[0].reshape(1, 1, d)
    fin3 = final_norm_gain.reshape(1, 1, d)
    sink = attn_sink[0]

    m_all = ns + bp
    m_pad = -(-m_all // 8) * 8
    c_all = jnp.concatenate([c_sample, c_prompt, jnp.zeros((m_pad - m_all, d), F32)], axis=0)
    mod = _adaln(c_all, w_ada[0], b_ada[0].reshape(1, 3 * d))
    shift, scale, gate = mod[:, :d], mod[:, d:2 * d], mod[:, 2 * d:]
    sl_s, sl_p = slice(0, ns), slice(ns, ns + bp)
    as3 = lambda a, n: a.reshape(n, 1, d)

    h, r = _prologue(x_prompt, x_sample, gain3, as3(scale[sl_p], bp), as3(shift[sl_p], bp),
                     as3(scale[sl_s], ns), as3(shift[sl_s], ns), w_rt, 512)
    p, kv, w_pa, w_pb, w_o, cache_k, cache_v = _in_proj(
        h, wt, PROJ_DEST, PROJ_SRC, rows // 4, PROJ_TN, BF16,
        casts=(w_proj_a[0], w_proj_b[0], w_out[0]),
        xposes=(cache_t(cache_k_win), cache_t(cache_v_win)), n_side=ns)
    kv_p = jnp.stack([kv[(bi + 1) * lp - WINDOW:(bi + 1) * lp] for bi in range(bp)])
    kv_p = kv_p.reshape(bp, WINDOW, 2, N_KV, HEAD_DIM)
    k_win_p = kv_p[:, :, 0][None]
    v_win_p = kv_p[:, :, 1][None]
    kv3 = kv.reshape(rows // ts, ts, 2 * KV_W)
    r3 = r.reshape(rows // ts, ts, LANE)

    ua_p = _swa_prompt(sink, p, bp, lp)
    ub_p, s_p = _gla_prompt(p, r, w2, b2, gla_gain, bp, lp)

    ua_s, k_win_s, v_win_s = _swa_sample(sink, p, rows_p, ns, ts, kv3, rows_p // ts, cache_k, cache_v)
    ub_s, s_s = _gla_sample(p, r3, rows_p, ts, w2, b2, gla_gain, state_gla[0])

    merged = _merge(ua_p, ub_p.reshape(rows_p, WIDTH_B), ua_s, ub_s, w_pa, w_pb, p, 256)
    y_p = _out_proj(merged, 0, w_o, x_prompt, as3(gate[sl_p], bp), fin3, 1, 256)
    y_s = _out_proj(merged, rows_p, w_o, x_sample, as3(gate[sl_s], ns), fin3, 256 // ts, ts)

    return (y_p, y_s, k_win_p, v_win_p, s_p[None],
            k_win_s.reshape(1, ns, w_len, N_KV, HEAD_DIM), v_win_s.reshape(1, ns, w_len, N_KV, HEAD_DIM),
            s_s[None])
```

```python
import functools

import jax
import jax.numpy as jnp
from jax import lax
from jax.experimental import pallas as pl
from jax.experimental.pallas import tpu as pltpu

F32 = jnp.float32
BF16 = jnp.bfloat16

D_MODEL = 4096
WIDTH_A = 2048
HEAD_DIM = 64
N_HEADS_A = 32
N_KV = 4
GROUP = 8
WINDOW = 128
ATTN_BLOCK = 128
WIDTH_B = 2048
N_HEADS_B = 4
DV_B = 512
KEY_B = 1024
DK_B = 256
GATE_RANK = 16
GLA_TAU = 16.0
GLA_CHUNK = 64
EPS = 1e-6
NEG_INF = -1e30
KV_W = N_KV * HEAD_DIM

V7X_VMEM_BYTES = 64 * 1024 * 1024
VMEM_LIMIT = V7X_VMEM_BYTES - 8 * 1024 * 1024
VMEM_LIMIT_RESIDENT = V7X_VMEM_BYTES - 4 * 1024 * 1024
LANE = 128
BF16_ROWS = 16

SRC_KA = WIDTH_A
SRC_MAIN = WIDTH_A + 2 * KV_W + WIDTH_A + 2 * KEY_B + 2 * WIDTH_B
SRC_GATES = SRC_MAIN + GATE_RANK
PROJ_TN = 512
OFF_GA = 0
OFF_GB = 4096
OFF_QA = 8192
OFF_ZA = 10240
OFF_VB = 12288
OFF_ZB = 14336
OFF_QB = 16384
OFF_KB = 17408
OFF_KA = 18432
OFF_VA = 18688
PROJ_COLS = 18944
_PROJ_TILES = ([(j * PROJ_TN, 16 + t) for j, t in zip((0, 1, 2, 3, 5, 6, 7, 8, 9, 10, 11, 12, 13, 14, 15, 16,
                                                        17, 18, 19, 20),
                                                       (0, 1, 2, 3, 4, 5, 6, 7, 16, 17, 18, 19, 8, 9, 10, 11,
                                                        12, 13, 14, 15))]
               + [(SRC_GATES + j * PROJ_TN, j) for j in range(16)]
               + [(SRC_KA, 16 + 20)])
PROJ_SRC = tuple(s for s, _ in _PROJ_TILES)
PROJ_DEST = tuple(d for _, d in _PROJ_TILES)

NT_DIMS = (((1,), (1,)), ((), ()))
TN_DIMS = (((0,), (0,)), ((), ()))


def _cparams(sem, vmem_limit=VMEM_LIMIT):
    return pltpu.CompilerParams(dimension_semantics=sem, vmem_limit_bytes=vmem_limit)


def _silu(x):
    return x * jax.nn.sigmoid(x)


def _adaln_kernel(c_ref, w_ref, b_ref, o_ref):
    a = _silu(c_ref[...]).astype(BF16)
    w = w_ref[...].astype(BF16)
    o_ref[...] = jnp.dot(a, w, preferred_element_type=F32) + b_ref[...]


def _adaln(c_all, w_ada, b_ada, tn=512):
    m, k = c_all.shape
    n = w_ada.shape[1]
    return pl.pallas_call(
        _adaln_kernel,
        out_shape=jax.ShapeDtypeStruct((m, n), F32),
        grid=(n // tn,),
        in_specs=[
            pl.BlockSpec((m, k), lambda j: (0, 0)),
            pl.BlockSpec((k, tn), lambda j: (0, j)),
            pl.BlockSpec((1, tn), lambda j: (0, j)),
        ],
        out_specs=pl.BlockSpec((m, tn), lambda j: (0, j)),
        compiler_params=_cparams(("arbitrary",)),
        name="adaln",
    )(c_all, w_ada, b_ada)


def _prologue_kernel(n_ptiles, xp_ref, xs_ref, g_ref, scp_ref, shp_ref, scs_ref, shs_ref, wr_ref,
                     h_ref, r_ref):
    i = pl.program_id(0)

    def emit(x, scale, shift):
        ms = jnp.mean(x * x, axis=-1, keepdims=True)
        y = x * lax.rsqrt(ms + EPS) * g_ref[...]
        h = y * (1.0 + scale) + shift
        hb = h.reshape(h.shape[0] * h.shape[1], h.shape[2]).astype(BF16)
        h_ref[...] = hb
        r_ref[...] = lax.dot_general(hb, wr_ref[...], NT_DIMS, preferred_element_type=F32)

    @pl.when(i < n_ptiles)
    def _():
        emit(xp_ref[...], scp_ref[...], shp_ref[...])

    @pl.when(i >= n_ptiles)
    def _():
        emit(xs_ref[...], scs_ref[...], shs_ref[...])


def _prologue(xp, xs, gain, scp, shp, scs, shs, w_rt, rows):
    b, l, d = xp.shape
    n, t, _ = xs.shape
    lb = l // rows
    n_pt = b * lb
    sg = rows // t
    n_st = n // sg
    pt = lambda i: jnp.minimum(i, n_pt - 1)
    st = lambda i: jnp.maximum(i - n_pt, 0)
    return pl.pallas_call(
        functools.partial(_prologue_kernel, n_pt),
        out_shape=(jax.ShapeDtypeStruct((b * l + n * t, d), BF16),
                   jax.ShapeDtypeStruct((b * l + n * t, LANE), F32)),
        grid=(n_pt + n_st,),
        in_specs=[
            pl.BlockSpec((1, rows, d), lambda i: (pt(i) // lb, pt(i) % lb, 0)),
            pl.BlockSpec((sg, t, d), lambda i: (st(i), 0, 0)),
            pl.BlockSpec((1, 1, d), lambda i: (0, 0, 0)),
            pl.BlockSpec((1, 1, d), lambda i: (pt(i) // lb, 0, 0)),
            pl.BlockSpec((1, 1, d), lambda i: (pt(i) // lb, 0, 0)),
            pl.BlockSpec((sg, 1, d), lambda i: (st(i), 0, 0)),
            pl.BlockSpec((sg, 1, d), lambda i: (st(i), 0, 0)),
            pl.BlockSpec((LANE, d), lambda i: (0, 0)),
        ],
        out_specs=(pl.BlockSpec((rows, d), lambda i: (i, 0)),
                   pl.BlockSpec((rows, LANE), lambda i: (i, 0))),
        compiler_params=_cparams(("arbitrary",)),
        name="prologue",
    )(xp, xs, gain, scp, shp, scs, shs, w_rt)


def _inproj_kernel(n_cast, n_xpose, dest_ref, src_ref, a_ref, w_ref, *refs):
    del dest_ref, src_ref
    n_in = n_cast + n_xpose
    side_in, o_ref, last_ref, side_out = refs[:n_in], refs[n_in], refs[n_in + 1], refs[n_in + 2:]
    w = w_ref[...].astype(BF16)
    acc = lax.dot_general(a_ref[...], w, NT_DIMS, preferred_element_type=F32)
    o_ref[...] = acc.astype(o_ref.dtype)
    last_ref[...] = acc

    for src, dst in zip(side_in[:n_cast], side_out[:n_cast]):
        dst[...] = src[...].astype(dst.dtype)
    for src, dst in zip(side_in[n_cast:], side_out[n_cast:]):
        dst[0] = jnp.transpose(src[0])


def _in_proj(a, wt, dest, src, tm, tn, out_dtype, casts, xposes, n_side):
    m, k = a.shape
    nt = len(dest)
    assert (m // tm) * nt >= n_side
    chunk = lambda i, j, d, s: jnp.minimum(i * nt + j, n_side - 1)
    side_in, side_out, side_shapes = [], [], []
    for arr in casts:
        blk = (arr.shape[0] // n_side, arr.shape[1])
        assert blk[0] % BF16_ROWS == 0
        spec = pl.BlockSpec(blk, lambda i, j, d, s: (chunk(i, j, d, s), 0))
        side_in.append(spec)
        side_out.append(spec)
        side_shapes.append(jax.ShapeDtypeStruct(arr.shape, BF16))
    for arr in xposes:
        _, r, c = arr.shape
        side_in.append(pl.BlockSpec((1, r, c), lambda i, j, d, s: (chunk(i, j, d, s), 0, 0)))
        side_out.append(pl.BlockSpec((1, c, r), lambda i, j, d, s: (chunk(i, j, d, s), 0, 0)))
        side_shapes.append(jax.ShapeDtypeStruct((n_side, c, r), arr.dtype))
    return pl.pallas_call(
        functools.partial(_inproj_kernel, len(casts), len(xposes)),
        out_shape=[jax.ShapeDtypeStruct((m, nt * tn), out_dtype), jax.ShapeDtypeStruct((m, tn), F32)] + side_shapes,
        grid_spec=pltpu.PrefetchScalarGridSpec(
            num_scalar_prefetch=2,
            grid=(m // tm, nt),
            in_specs=[
                pl.BlockSpec((tm, k), lambda i, j, d, s: (i, 0), pipeline_mode=pl.Buffered(1)),
                pl.BlockSpec((pl.Element(tn), pl.Element(k)),
                             lambda i, j, d, s: (pl.multiple_of(s[j], GATE_RANK), 0)),
            ] + side_in,
            out_specs=[pl.BlockSpec((tm, tn), lambda i, j, d, s: (i, d[j])),
                       pl.BlockSpec((tm, tn), lambda i, j, d, s: (i, 0))] + side_out,
        ),
        compiler_params=_cparams(("arbitrary", "arbitrary"), VMEM_LIMIT_RESIDENT),
        name="in_proj",
    )(jnp.asarray(dest, jnp.int32), jnp.asarray(src, jnp.int32), a, wt, *casts, *xposes)


def _swa_prompt_kernel(sink_ref, q_ref, za_ref, kp_ref, kc_ref, vp_ref, vc_ref, o_ref):
    n = pl.program_id(1)
    blk = ATTN_BLOCK
    pairs = GROUP // 2
    pw = 2 * HEAD_DIM
    hw = GROUP * HEAD_DIM
    k = jnp.concatenate([kp_ref[...], kc_ref[...]], axis=0)
    v = jnp.concatenate([vp_ref[...], vc_ref[...]], axis=0)
    rows = pairs * blk
    i = lax.broadcasted_iota(jnp.int32, (rows, 2 * blk), 0) % blk
    j = lax.broadcasted_iota(jnp.int32, (rows, 2 * blk), 1)
    visible = (j >= i) & (j <= i + WINDOW) & ((j >= blk) | (n > 0))
    bias = jnp.where(visible, 0.0, NEG_INF)
    pair_id = lax.broadcasted_iota(jnp.int32, (rows, 1), 0) // blk
    zeros = jnp.zeros((2 * blk, HEAD_DIM), BF16)
    for h in range(N_KV):
        kh = k[:, h * HEAD_DIM:(h + 1) * HEAD_DIM] * (HEAD_DIM ** -0.5)
        vh = v[:, h * HEAD_DIM:(h + 1) * HEAD_DIM]
        q2 = jnp.concatenate([q_ref[:, h * hw + jj * pw:h * hw + (jj + 1) * pw] for jj in range(pairs)], axis=0)
        acc = None
        k_pads = [jnp.concatenate([kh, zeros], axis=1), jnp.concatenate([zeros, kh], axis=1)]
        scores = [lax.dot_general(q2, kp, NT_DIMS, preferred_element_type=F32) + bias for kp in k_pads]
        probs = []
        for half in range(2):
            sink = jnp.zeros((rows, 1), F32)
            for jj in range(pairs):
                sink = jnp.where(pair_id == jj, sink_ref[h * GROUP + 2 * jj + half], sink)
            s = scores[half]
            m = jnp.maximum(jnp.max(s, axis=-1, keepdims=True), sink)
            p = jnp.exp(s - m)
            denom = jnp.sum(p, axis=-1, keepdims=True) + jnp.exp(sink - m)
            probs.append((p.astype(BF16), denom))
        for half in range(2):
            kv_parts = (lambda t: [t, zeros]) if half == 0 else (lambda t: [zeros, t])
            v_pad = jnp.concatenate(kv_parts(vh), axis=1)
            pb, denom = probs[half]
            o = jnp.dot(pb, v_pad, preferred_element_type=F32) / denom
            acc = o if acc is None else acc + o
        o_h = jnp.concatenate([acc[jj * blk:(jj + 1) * blk] for jj in range(pairs)], axis=1)
        za = za_ref[:, h * hw:(h + 1) * hw].astype(F32)
        o_ref[:, h * hw:(h + 1) * hw] = (o_h * _silu(za)).astype(BF16)


def _swa_prompt(sink, p, b, l):
    nb = l // ATTN_BLOCK
    kcol = OFF_KA // KV_W
    vcol = OFF_VA // KV_W
    blk_q = (ATTN_BLOCK, WIDTH_A)
    blk_kv = (ATTN_BLOCK, KV_W)
    cur = lambda bi, n: bi * nb + n
    prev = lambda bi, n: bi * nb + jnp.maximum(n - 1, 0)
    return pl.pallas_call(
        _swa_prompt_kernel,
        out_shape=jax.ShapeDtypeStruct((b * l, WIDTH_A), BF16),
        grid=(b, nb),
        in_specs=[
            pl.BlockSpec(memory_space=pltpu.SMEM),
            pl.BlockSpec(blk_q, lambda bi, n: (cur(bi, n), OFF_QA // WIDTH_A)),
            pl.BlockSpec(blk_q, lambda bi, n: (cur(bi, n), OFF_ZA // WIDTH_A)),
            pl.BlockSpec(blk_kv, lambda bi, n: (prev(bi, n), kcol)),
            pl.BlockSpec(blk_kv, lambda bi, n: (cur(bi, n), kcol)),
            pl.BlockSpec(blk_kv, lambda bi, n: (prev(bi, n), vcol)),
            pl.BlockSpec(blk_kv, lambda bi, n: (cur(bi, n), vcol)),
        ],
        out_specs=pl.BlockSpec(blk_q, lambda bi, n: (cur(bi, n), 0)),
        compiler_params=_cparams(("arbitrary", "arbitrary")),
        name="swa_prompt",
    )(sink, p, p, p, p, p, p)


def _swa_sample_kernel(g_blk, t_len, w_len, sink_ref, q_ref, za_ref, kv_ref, kc_ref, vc_ref,
                       o_ref, kw_ref, vw_ref):
    kpad = 2 * WINDOW - w_len - t_len
    nkeys = 2 * WINDOW
    rows = GROUP * t_len
    r = lax.broadcasted_iota(jnp.int32, (rows, nkeys), 0)
    j = lax.broadcasted_iota(jnp.int32, (rows, nkeys), 1)
    t = r % t_len
    visible = (j <= w_len + t) & (j >= w_len + t - WINDOW)
    bias = jnp.where(visible, 0.0, NEG_INF)[None]
    rg = lax.broadcasted_iota(jnp.int32, (rows, 1), 0) // t_len
    sink_cols = []
    for h in range(N_KV):
        col = jnp.zeros((rows, 1), F32)
        for g in range(GROUP):
            col = jnp.where(rg == g, sink_ref[h * GROUP + g], col)
        sink_cols.append(col)
    sink = jnp.concatenate(sink_cols * g_blk, axis=0).reshape(g_blk * N_KV, rows, 1)
    zpad = jnp.zeros((kpad, KV_W), F32)
    q_all = q_ref[...].astype(F32)
    za_all = za_ref[...].astype(F32)

    scores, values = [], []
    for s in range(g_blk):
        kc = kc_ref[s]
        vc = vc_ref[s]
        kn = kv_ref[s, :, :KV_W]
        vn = kv_ref[s, :, KV_W:]
        kw_ref[s] = jnp.concatenate([kc[t_len:], kn], axis=0)
        vw_ref[s] = jnp.concatenate([vc[t_len:], vn], axis=0)
        k_all = jnp.concatenate([kc, kn, zpad], axis=0).astype(BF16) * (HEAD_DIM ** -0.5)
        v_all = jnp.concatenate([vc, vn, zpad], axis=0).astype(BF16)
        q = q_all[s * t_len:(s + 1) * t_len]
        for h in range(N_KV):
            qh = jnp.concatenate(
                [q[:, (h * GROUP + g) * HEAD_DIM:(h * GROUP + g + 1) * HEAD_DIM] for g in range(GROUP)],
                axis=0).astype(BF16)
            scores.append(lax.dot_general(qh, k_all[:, h * HEAD_DIM:(h + 1) * HEAD_DIM], NT_DIMS,
                                          preferred_element_type=F32))
            values.append(v_all[:, h * HEAD_DIM:(h + 1) * HEAD_DIM])

    sc = jnp.concatenate(scores, axis=0).reshape(g_blk * N_KV, rows, nkeys) + bias
    m = jnp.maximum(jnp.max(sc, axis=-1, keepdims=True), sink)
    p = jnp.exp(sc - m)
    inv = 1.0 / (jnp.sum(p, axis=-1, keepdims=True) + jnp.exp(sink - m))
    pb = p.astype(BF16)

    seq_outs = []
    for s in range(g_blk):
        pieces = []
        for h in range(N_KV):
            c = s * N_KV + h
            oh = jnp.dot(pb[c], values[c], preferred_element_type=F32) * inv[c]
            for g in range(GROUP):
                pieces.append(oh[g * t_len:(g + 1) * t_len])
        o = jnp.concatenate(pieces, axis=1)
        seq_outs.append(o * _silu(za_all[s * t_len:(s + 1) * t_len]))
    o_ref[...] = jnp.concatenate(seq_outs, axis=0).astype(BF16)


def _cumsum_rows(x):
    c = x.shape[0]
    row = lax.broadcasted_iota(jnp.int32, (c, 1), 0)
    sh = 1
    while sh < c:
        x = x + jnp.where(row >= sh, pltpu.roll(x, sh, 0), 0.0)
        sh *= 2
    return x


def _gla_chunk(q, k, v, la, s_old, causal):
    c = q.shape[0]
    b = _cumsum_rows(la)
    b_last = b[c - 1:c, :]
    decay_col = jnp.transpose(jnp.broadcast_to(jnp.exp(b_last), (LANE, b_last.shape[1])))
    q_t = (q * jnp.exp(b) * (DK_B ** -0.5)).astype(BF16)
    k_t = (k * jnp.exp(-b)).astype(BF16)
    k_d = (k * jnp.exp(b_last - b)).astype(BF16)
    outs, states = [], []
    for h in range(N_HEADS_B):
        dk = slice(h * DK_B, (h + 1) * DK_B)
        dv = slice(h * DV_B, (h + 1) * DV_B)
        att = lax.dot_general(q_t[:, dk], k_t[:, dk], NT_DIMS, preferred_element_type=F32)
        att = jnp.where(causal, att, 0.0).astype(BF16)
        outs.append(jnp.dot(att, v[:, dv], preferred_element_type=F32)
                    + jnp.dot(q_t[:, dk], s_old[h].astype(BF16), preferred_element_type=F32))
        decay = jnp.concatenate([decay_col[dk]] * (DV_B // LANE), axis=1)
        states.append(decay * s_old[h] + lax.dot_general(k_d[:, dk], v[:, dv], TN_DIMS,
                                                         preferred_element_type=F32))
    return outs, states


def _gla_block(q, k, v, la, s_old, causal, n):
    c = q.shape[0] // n
    rows = [slice(i * c, (i + 1) * c) for i in range(n)]
    b = [_cumsum_rows(la[r]) for r in rows]
    t = [bi[c - 1:c, :] for bi in b]
    pre = [None]
    for i in range(n):
        pre.append(t[i] if pre[-1] is None else pre[-1] + t[i])
    decayed = lambda x, e: x if e is None else x * jnp.exp(e)
    scale = DK_B ** -0.5
    q_t = [q[r] * jnp.exp(bi) * scale for r, bi in zip(rows, b)]
    k_t = [(k[r] * jnp.exp(-bi)).astype(BF16) for r, bi in zip(rows, b)]
    k_d = [k[r] * jnp.exp(ti - bi) for r, bi, ti in zip(rows, b, t)]
    q_read = jnp.concatenate([decayed(q_t[i], pre[i]).astype(BF16) for i in range(n)], axis=0)
    k_upd = jnp.concatenate([decayed(k_d[j], None if j == n - 1 else pre[n] - pre[j + 1]).astype(BF16)
                             for j in range(n)], axis=0)
    q_b = [x.astype(BF16) for x in q_t]
    q_x = {(i, j): q_b[i] if i == j + 1 else (q_t[i] * jnp.exp(pre[i] - pre[j + 1])).astype(BF16)
           for j in range(n) for i in range(j + 1, n)}
    k_db = [x.astype(BF16) for x in k_d]
    decay_col = jnp.transpose(jnp.broadcast_to(jnp.exp(pre[n]), (LANE, q.shape[1])))
    nt = lambda x, y: lax.dot_general(x, y, NT_DIMS, preferred_element_type=F32)
    outs, states = [], []
    for h in range(N_HEADS_B):
        dk = slice(h * DK_B, (h + 1) * DK_B)
        dv = slice(h * DV_B, (h + 1) * DV_B)
        att = [nt(q_b[j][:, dk], k_t[j][:, dk]) for j in range(n)]
        cross = {ij: nt(qx[:, dk], k_db[ij[1]][:, dk]) for ij, qx in q_x.items()}
        lhs = [jnp.concatenate([jnp.where(causal, att[j], 0.0)] + [cross[i, j] for i in range(j + 1, n)],
                               axis=0).astype(BF16) for j in range(n)]
        o = (jnp.dot(lhs[0], v[rows[0], dv], preferred_element_type=F32)
             + jnp.dot(q_read[:, dk], s_old[h].astype(BF16), preferred_element_type=F32))
        done = [o[rows[0]]]
        for j in range(1, n):
            o = o[c:] + jnp.dot(lhs[j], v[rows[j], dv], preferred_element_type=F32)
            done.append(o[:c])
        outs.append(jnp.concatenate(done, axis=0))
        decay = jnp.concatenate([decay_col[dk]] * (DV_B // LANE), axis=1)
        states.append(decay * s_old[h] + lax.dot_general(k_upd[:, dk], v[:, dv], TN_DIMS,
                                                         preferred_element_type=F32))
    return outs, states


def _log_decay(r, w2, b2):
    z = jnp.dot(r.astype(BF16), w2, preferred_element_type=F32) + b2
    ls = jnp.minimum(z, 0.0) - jnp.log(1.0 + jnp.exp(-jnp.abs(z)))
    return ls / GLA_TAU


def _gla_finish(o, zb, gain):
    on = o * lax.rsqrt(jnp.mean(o * o, axis=-1, keepdims=True) + EPS) * gain
    return on * _silu(zb)


def _causal_mask(c):
    ri = lax.broadcasted_iota(jnp.int32, (c, c), 0)
    ci = lax.broadcasted_iota(jnp.int32, (c, c), 1)
    return ri >= ci


def _gla_prompt_kernel(nb, n_blk, *refs):
    w2_ref, b2_ref, gain_ref, o_ref, s_ref = refs[5 * nb:]

    @pl.when(pl.program_id(0) == 0)
    def _():
        s_ref[...] = jnp.zeros_like(s_ref)

    c = n_blk * GLA_CHUNK
    causal = _causal_mask(GLA_CHUNK)
    for bi in range(nb):
        q_ref, k_ref, v_ref, zb_ref, r_ref = refs[5 * bi:5 * bi + 5]
        states = [s_ref[bi, h] for h in range(N_HEADS_B)]
        for sub in range(o_ref.shape[1] // c):
            rs = slice(sub * c, (sub + 1) * c)
            la = _log_decay(r_ref[rs, :], w2_ref[...], b2_ref[...])
            outs, states = _gla_block(q_ref[rs, :].astype(F32), k_ref[rs, :].astype(F32), v_ref[rs, :], la,
                                      states, causal, n_blk)
            for h in range(N_HEADS_B):
                dv = slice(h * DV_B, (h + 1) * DV_B)
                o_ref[bi, rs, dv] = _gla_finish(outs[h], zb_ref[rs, dv].astype(F32), gain_ref[:, dv]).astype(BF16)
        for h in range(N_HEADS_B):
            s_ref[bi, h] = states[h]


def _gla_prompt(p, r, w2, b2, gain, b, l, chunks_per_block=4):
    c = GLA_CHUNK * chunks_per_block
    nc = l // c
    in_specs, args = [], []
    for bi in range(b):
        for arr, width, off in ((p, KEY_B, OFF_QB), (p, KEY_B, OFF_KB), (p, WIDTH_B, OFF_VB),
                                (p, WIDTH_B, OFF_ZB), (r, LANE, 0)):
            in_specs.append(pl.BlockSpec((c, width), functools.partial(
                lambda ci, row0, col: (row0 + ci, col), row0=bi * nc, col=off // width)))
            args.append(arr)
    in_specs += [pl.BlockSpec((LANE, KEY_B), lambda ci: (0, 0)),
                 pl.BlockSpec((1, KEY_B), lambda ci: (0, 0)),
                 pl.BlockSpec((1, WIDTH_B), lambda ci: (0, 0))]
    return pl.pallas_call(
        functools.partial(_gla_prompt_kernel, b, chunks_per_block),
        out_shape=(jax.ShapeDtypeStruct((b, l, WIDTH_B), BF16),
                   jax.ShapeDtypeStruct((b, N_HEADS_B, DK_B, DV_B), F32)),
        grid=(nc,),
        in_specs=in_specs,
        out_specs=(pl.BlockSpec((b, c, WIDTH_B), lambda ci: (0, ci, 0)),
                   pl.BlockSpec((b, N_HEADS_B, DK_B, DV_B), lambda ci: (0, 0, 0, 0))),
        compiler_params=_cparams(("arbitrary",)),
        name="gla_prompt",
    )(*args, w2, b2, gain)


def _gla_sample_kernel(g_blk, t_len, q_ref, k_ref, v_ref, zb_ref, r_ref, w2_ref, b2_ref, gain_ref, s0_ref,
                       o_ref, s_ref):
    c = 2 * t_len
    causal = _causal_mask(c)

    def pad(x):
        return jnp.concatenate([x, jnp.zeros_like(x)], axis=0)

    live = lax.broadcasted_iota(jnp.int32, (c, 1), 0) < t_len
    q_all = q_ref[...].astype(F32)
    k_all = k_ref[...].astype(F32)
    v_all = v_ref[...].astype(F32)
    zb_all = zb_ref[...].astype(F32)
    seq_outs = []
    for s in range(g_blk):
        rs = slice(s * t_len, (s + 1) * t_len)
        la = jnp.where(live, _log_decay(pad(r_ref[s]), w2_ref[...], b2_ref[...]), 0.0)
        outs, states = _gla_chunk(pad(q_all[rs]), pad(k_all[rs]), pad(v_all[rs]).astype(BF16), la,
                                  [s0_ref[s, h] for h in range(N_HEADS_B)], causal)
        fin = []
        for h in range(N_HEADS_B):
            dv = slice(h * DV_B, (h + 1) * DV_B)
            s_ref[s, h] = states[h]
            fin.append(_gla_finish(outs[h][:t_len], zb_all[rs, dv], gain_ref[:, dv]))
        seq_outs.append(jnp.concatenate(fin, axis=1))
    o_ref[...] = jnp.concatenate(seq_outs, axis=0).astype(BF16)


def _sample_kernel(g_blk, t_len, w_len, *refs):
    n_swa_in, n_gla_in, n_swa_out = 6, 9, 3
    swa_in, gla_in = refs[:n_swa_in], refs[n_swa_in:n_swa_in + n_gla_in]
    outs = refs[n_swa_in + n_gla_in:]
    _swa_sample_kernel(g_blk, t_len, w_len, *swa_in, *outs[:n_swa_out])
    _gla_sample_kernel(g_blk, t_len, *gla_in, *outs[n_swa_out:])


def _sample_branches(sink, p, r3, kv3, row0, n, t_len, cache_k, cache_v, w2, b2, gain, state, g_blk=4):
    rows = g_blk * t_len
    assert rows % BF16_ROWS == 0
    w_len = cache_k.shape[1]
    blk_q = (rows, WIDTH_A)
    blk_c = (g_blk, w_len, KV_W)
    st_blk = (g_blk, N_HEADS_B, DK_B, DV_B)
    r0 = row0 // rows
    col = lambda width, off: (lambda i: (r0 + i, off // width))
    seq3 = lambda i: (r0 + i, 0, 0)
    own3 = lambda i: (i, 0, 0)
    const = lambda i: (0, 0)
    return pl.pallas_call(
        functools.partial(_sample_kernel, g_blk, t_len, w_len),
        out_shape=(jax.ShapeDtypeStruct((n * t_len, WIDTH_A), BF16),
                   jax.ShapeDtypeStruct((n, w_len, KV_W), F32),
                   jax.ShapeDtypeStruct((n, w_len, KV_W), F32),
                   jax.ShapeDtypeStruct((n * t_len, WIDTH_B), BF16),
                   jax.ShapeDtypeStruct(state.shape, F32)),
        grid=(n // g_blk,),
        in_specs=[
            pl.BlockSpec(memory_space=pltpu.SMEM),
            pl.BlockSpec(blk_q, col(WIDTH_A, OFF_QA)),
            pl.BlockSpec(blk_q, col(WIDTH_A, OFF_ZA)),
            pl.BlockSpec((g_blk, t_len, 2 * KV_W), seq3),
            pl.BlockSpec(blk_c, own3),
            pl.BlockSpec(blk_c, own3),
            pl.BlockSpec((rows, KEY_B), col(KEY_B, OFF_QB)),
            pl.BlockSpec((rows, KEY_B), col(KEY_B, OFF_KB)),
            pl.BlockSpec((rows, WIDTH_B), col(WIDTH_B, OFF_VB)),
            pl.BlockSpec((rows, WIDTH_B), col(WIDTH_B, OFF_ZB)),
            pl.BlockSpec((g_blk, t_len, LANE), seq3),
            pl.BlockSpec((LANE, KEY_B), const),
            pl.BlockSpec((1, KEY_B), const),
            pl.BlockSpec((1, WIDTH_B), const),
            pl.BlockSpec(st_blk, lambda i: (i, 0, 0, 0)),
        ],
        out_specs=(pl.BlockSpec(blk_q, lambda i: (i, 0)),
                   pl.BlockSpec(blk_c, own3),
                   pl.BlockSpec(blk_c, own3),
                   pl.BlockSpec((rows, WIDTH_B), lambda i: (i, 0)),
                   pl.BlockSpec(st_blk, lambda i: (i, 0, 0, 0))),
        compiler_params=_cparams(("arbitrary",)),
        name="sample_branches",
    )(sink, p, p, kv3, cache_k, cache_v, p, p, p, p, r3, w2, b2, gain, state)


def _merge_kernel(nchunk, n_ptiles, ap_ref, bp_ref, as_ref, bs_ref, wa_ref, wb_ref, ga_ref, gb_ref, o_ref):
    cw = o_ref.shape[1] // nchunk
    is_prompt = pl.program_id(0) < n_ptiles
    a = jnp.where(is_prompt, ap_ref[...], as_ref[...])
    b = jnp.where(is_prompt, bp_ref[...], bs_ref[...])
    for c in range(nchunk):
        cs = slice(c * cw, (c + 1) * cw)
        ua = jnp.dot(a, wa_ref[:, cs], preferred_element_type=F32)
        ub = jnp.dot(b, wb_ref[:, cs], preferred_element_type=F32)
        merged = (jax.nn.sigmoid(ga_ref[:, cs].astype(F32)) * ua
                  + jax.nn.sigmoid(gb_ref[:, cs].astype(F32)) * ub)
        o_ref[:, cs] = merged.astype(BF16)


def _merge(ap, bp, a_s, b_s, w_pa, w_pb, p, tm):
    n_pt = ap.shape[0] // tm
    n_st = a_s.shape[0] // tm
    d = D_MODEL
    resident = dict(pipeline_mode=pl.Buffered(1))
    pt = lambda i: (jnp.minimum(i, n_pt - 1), 0)
    st = lambda i: (jnp.maximum(i - n_pt, 0), 0)
    return pl.pallas_call(
        functools.partial(_merge_kernel, 4, n_pt),
        out_shape=jax.ShapeDtypeStruct(((n_pt + n_st) * tm, d), BF16),
        grid=(n_pt + n_st,),
        in_specs=[
            pl.BlockSpec((tm, WIDTH_A), pt),
            pl.BlockSpec((tm, WIDTH_B), pt),
            pl.BlockSpec((tm, WIDTH_A), st),
            pl.BlockSpec((tm, WIDTH_B), st),
            pl.BlockSpec((WIDTH_A, d), lambda i: (0, 0), **resident),
            pl.BlockSpec((WIDTH_B, d), lambda i: (0, 0), **resident),
            pl.BlockSpec((tm, d), lambda i: (i, OFF_GA // d)),
            pl.BlockSpec((tm, d), lambda i: (i, OFF_GB // d)),
        ],
        out_specs=pl.BlockSpec((tm, d), lambda i: (i, 0)),
        compiler_params=_cparams(("arbitrary",), VMEM_LIMIT_RESIDENT),
        name="merge",
    )(ap, bp, a_s, b_s, w_pa, w_pb, p, p)


def _out_kernel(nchunk, m_ref, w_ref, x_ref, gate_ref, fg_ref, o_ref):
    g_blk, r_blk, d = x_ref.shape
    cw = d // nchunk
    ssq = jnp.zeros((g_blk, r_blk, 1), F32)
    for c in range(nchunk):
        cs = slice(c * cw, (c + 1) * cw)
        y = jnp.dot(m_ref[...], w_ref[:, cs], preferred_element_type=F32)
        xn = x_ref[:, :, cs] + gate_ref[:, :, cs] * y.reshape(g_blk, r_blk, cw)
        o_ref[:, :, cs] = xn
        ssq = ssq + jnp.sum(xn * xn, axis=-1, keepdims=True)
    inv = lax.rsqrt(ssq * (1.0 / d) + EPS)
    for c in range(nchunk):
        cs = slice(c * cw, (c + 1) * cw)
        o_ref[:, :, cs] = o_ref[:, :, cs] * inv * fg_ref[:, :, cs]


def _out_proj(merged, row0, w_o, x3, gate3, fin_gain, g_blk, r_blk):
    n, l, d = x3.shape
    rows = g_blk * r_blk
    lb = l // r_blk
    t0 = row0 // rows
    return pl.pallas_call(
        functools.partial(_out_kernel, 4),
        out_shape=jax.ShapeDtypeStruct((n, l, d), F32),
        grid=(n // g_blk, lb),
        in_specs=[
            pl.BlockSpec((rows, d), lambda i, r: (t0 + i * lb + r, 0)),
            pl.BlockSpec((d, d), lambda i, r: (0, 0), pipeline_mode=pl.Buffered(1)),
            pl.BlockSpec((g_blk, r_blk, d), lambda i, r: (i, r, 0)),
            pl.BlockSpec((g_blk, 1, d), lambda i, r: (i, 0, 0)),
            pl.BlockSpec((1, 1, d), lambda i, r: (0, 0, 0)),
        ],
        out_specs=pl.BlockSpec((g_blk, r_blk, d), lambda i, r: (i, r, 0)),
        compiler_params=_cparams(("arbitrary", "arbitrary"), VMEM_LIMIT_RESIDENT),
        name="out_proj",
    )(merged, w_o, x3, gate3, fin_gain)


def kernel(x_prompt, x_sample, cache_k_win, cache_v_win, state_gla, c_prompt, c_sample,
           w_ada, b_ada, norm_gain, w_in, attn_sink, w_alpha2, b_alpha, gla_norm_gain,
           w_proj_a, w_proj_b, w_out, final_norm_gain):
    assert w_ada.shape[0] == 1, "single-layer step"
    bp, lp, d = x_prompt.shape
    ns, ts, _ = x_sample.shape
    w_len = cache_k_win.shape[2]
    rows_p = bp * lp
    rows_s = ns * ts
    rows = rows_p + rows_s

    wt = jnp.transpose(w_in[0])
    w_rt = jnp.pad(wt[SRC_MAIN:SRC_GATES], ((0, LANE - GATE_RANK), (0, 0))).astype(BF16)
    w2 = jnp.pad(w_alpha2[0], ((0, LANE - GATE_RANK), (0, 0))).astype(BF16)
    b2 = b_alpha[0].reshape(1, KEY_B)
    gla_gain = gla_norm_gain[0].reshape(1, WIDTH_B)
    cache_t = lambda c: jnp.transpose(c[0], (0, 2, 3, 1)).reshape(ns, KV_W, w_len)
    gain3 = norm_gain[0].reshape(1, 1, d)
    fin3 = final_norm_gain.reshape(1, 1, d)
    sink = attn_sink[0]

    m_all = ns + bp
    m_pad = -(-m_all // 8) * 8
    c_all = jnp.concatenate([c_sample, c_prompt, jnp.zeros((m_pad - m_all, d), F32)], axis=0)
    mod = _adaln(c_all, w_ada[0], b_ada[0].reshape(1, 3 * d))
    shift, scale, gate = mod[:, :d], mod[:, d:2 * d], mod[:, 2 * d:]
    sl_s, sl_p = slice(0, ns), slice(ns, ns + bp)
    as3 = lambda a, n: a.reshape(n, 1, d)

    h, r = _prologue(x_prompt, x_sample, gain3, as3(scale[sl_p], bp), as3(shift[sl_p], bp),
                     as3(scale[sl_s], ns), as3(shift[sl_s], ns), w_rt, 512)
    p, kv, w_pa, w_pb, w_o, cache_k, cache_v = _in_proj(
        h, wt, PROJ_DEST, PROJ_SRC, rows // 4, PROJ_TN, BF16,
        casts=(w_proj_a[0], w_proj_b[0], w_out[0]),
        xposes=(cache_t(cache_k_win), cache_t(cache_v_win)), n_side=ns)
    kv_p = jnp.stack([kv[(bi + 1) * lp - WINDOW:(bi + 1) * lp] for bi in range(bp)])
    kv_p = kv_p.reshape(bp, WINDOW, 2, N_KV, HEAD_DIM)
    k_win_p = kv_p[:, :, 0][None]
    v_win_p = kv_p[:, :, 1][None]
    kv3 = kv.reshape(rows // ts, ts, 2 * KV_W)
    r3 = r.reshape(rows // ts, ts, LANE)

    ua_p = _swa_prompt(sink, p, bp, lp)
    ub_p, s_p = _gla_prompt(p, r, w2, b2, gla_gain, bp, lp)

    ua_s, k_win_s, v_win_s, ub_s, s_s = _sample_branches(sink, p, r3, kv3, rows_p, ns, ts, cache_k, cache_v,
                                                         w2, b2, gla_gain, state_gla[0])

    merged = _merge(ua_p, ub_p.reshape(rows_p, WIDTH_B), ua_s, ub_s, w_pa, w_pb, p, 256)
    y_p = _out_proj(merged, 0, w_o, x_prompt, as3(gate[sl_p], bp), fin3, 1, 256)
    y_s = _out_proj(merged, rows_p, w_o, x_sample, as3(gate[sl_s], ns), fin3, 256 // ts, ts)

    return (y_p, y_s, k_win_p, v_win_p, s_p[None],
            k_win_s.reshape(1, ns, w_len, N_KV, HEAD_DIM), v_win_s.reshape(1, ns, w_len, N_KV, HEAD_DIM),
            s_s[None])
```

```python
import functools

import jax
import jax.numpy as jnp
from jax import lax
from jax.experimental import pallas as pl
from jax.experimental.pallas import tpu as pltpu

F32 = jnp.float32
BF16 = jnp.bfloat16

D_MODEL = 4096
WIDTH_A = 2048
HEAD_DIM = 64
N_HEADS_A = 32
N_KV = 4
GROUP = 8
WINDOW = 128
ATTN_BLOCK = 128
WIDTH_B = 2048
N_HEADS_B = 4
DV_B = 512
KEY_B = 1024
DK_B = 256
GATE_RANK = 16
GLA_TAU = 16.0
GLA_CHUNK = 64
EPS = 1e-6
NEG_INF = -1e30
KV_W = N_KV * HEAD_DIM

V7X_VMEM_BYTES = 64 * 1024 * 1024
VMEM_LIMIT = V7X_VMEM_BYTES - 8 * 1024 * 1024
VMEM_LIMIT_RESIDENT = V7X_VMEM_BYTES - 4 * 1024 * 1024
LANE = 128
BF16_ROWS = 16
STATE_RING = 3

SRC_KA = WIDTH_A
SRC_MAIN = WIDTH_A + 2 * KV_W + WIDTH_A + 2 * KEY_B + 2 * WIDTH_B
SRC_GATES = SRC_MAIN + GATE_RANK
PROJ_TN = 512
OFF_GA = 0
OFF_GB = 4096
OFF_QA = 8192
OFF_ZA = 10240
OFF_VB = 12288
OFF_ZB = 14336
OFF_QB = 16384
OFF_KB = 17408
OFF_KA = 18432
OFF_VA = 18688
PROJ_COLS = 18944
_PROJ_TILES = ([(j * PROJ_TN, 16 + t) for j, t in zip((0, 1, 2, 3, 5, 6, 7, 8, 9, 10, 11, 12, 13, 14, 15, 16,
                                                        17, 18, 19, 20),
                                                       (0, 1, 2, 3, 4, 5, 6, 7, 16, 17, 18, 19, 8, 9, 10, 11,
                                                        12, 13, 14, 15))]
               + [(SRC_GATES + j * PROJ_TN, j) for j in range(16)]
               + [(SRC_KA, 16 + 20)])
PROJ_SRC = tuple(s for s, _ in _PROJ_TILES)
PROJ_DEST = tuple(d for _, d in _PROJ_TILES)

NT_DIMS = (((1,), (1,)), ((), ()))
TN_DIMS = (((0,), (0,)), ((), ()))


def _cparams(sem, vmem_limit=VMEM_LIMIT):
    return pltpu.CompilerParams(dimension_semantics=sem, vmem_limit_bytes=vmem_limit)


def _silu(x):
    return x * jax.nn.sigmoid(x)


def _adaln_kernel(c_ref, w_ref, b_ref, o_ref):
    a = _silu(c_ref[...]).astype(BF16)
    w = w_ref[...].astype(BF16)
    o_ref[...] = jnp.dot(a, w, preferred_element_type=F32) + b_ref[...]


def _adaln(c_all, w_ada, b_ada, tn=512):
    m, k = c_all.shape
    n = w_ada.shape[1]
    return pl.pallas_call(
        _adaln_kernel,
        out_shape=jax.ShapeDtypeStruct((m, n), F32),
        grid=(n // tn,),
        in_specs=[
            pl.BlockSpec((m, k), lambda j: (0, 0)),
            pl.BlockSpec((k, tn), lambda j: (0, j)),
            pl.BlockSpec((1, tn), lambda j: (0, j)),
        ],
        out_specs=pl.BlockSpec((m, tn), lambda j: (0, j)),
        compiler_params=_cparams(("arbitrary",)),
        name="adaln",
    )(c_all, w_ada, b_ada)


def _prologue_kernel(n_ptiles, xp_ref, xs_ref, g_ref, scp_ref, shp_ref, scs_ref, shs_ref, wr_ref,
                     h_ref, r_ref):
    i = pl.program_id(0)

    def emit(x, scale, shift):
        ms = jnp.mean(x * x, axis=-1, keepdims=True)
        y = x * lax.rsqrt(ms + EPS) * g_ref[...]
        h = y * (1.0 + scale) + shift
        hb = h.reshape(h.shape[0] * h.shape[1], h.shape[2]).astype(BF16)
        h_ref[...] = hb
        r_ref[...] = lax.dot_general(hb, wr_ref[...], NT_DIMS, preferred_element_type=F32)

    @pl.when(i < n_ptiles)
    def _():
        emit(xp_ref[...], scp_ref[...], shp_ref[...])

    @pl.when(i >= n_ptiles)
    def _():
        emit(xs_ref[...], scs_ref[...], shs_ref[...])


def _prologue(xp, xs, gain, scp, shp, scs, shs, w_rt, rows):
    b, l, d = xp.shape
    n, t, _ = xs.shape
    lb = l // rows
    n_pt = b * lb
    sg = rows // t
    n_st = n // sg
    pt = lambda i: jnp.minimum(i, n_pt - 1)
    st = lambda i: jnp.maximum(i - n_pt, 0)
    return pl.pallas_call(
        functools.partial(_prologue_kernel, n_pt),
        out_shape=(jax.ShapeDtypeStruct((b * l + n * t, d), BF16),
                   jax.ShapeDtypeStruct((b * l + n * t, LANE), F32)),
        grid=(n_pt + n_st,),
        in_specs=[
            pl.BlockSpec((1, rows, d), lambda i: (pt(i) // lb, pt(i) % lb, 0)),
            pl.BlockSpec((sg, t, d), lambda i: (st(i), 0, 0)),
            pl.BlockSpec((1, 1, d), lambda i: (0, 0, 0)),
            pl.BlockSpec((1, 1, d), lambda i: (pt(i) // lb, 0, 0)),
            pl.BlockSpec((1, 1, d), lambda i: (pt(i) // lb, 0, 0)),
            pl.BlockSpec((sg, 1, d), lambda i: (st(i), 0, 0)),
            pl.BlockSpec((sg, 1, d), lambda i: (st(i), 0, 0)),
            pl.BlockSpec((LANE, d), lambda i: (0, 0)),
        ],
        out_specs=(pl.BlockSpec((rows, d), lambda i: (i, 0)),
                   pl.BlockSpec((rows, LANE), lambda i: (i, 0))),
        compiler_params=_cparams(("arbitrary",)),
        name="prologue",
    )(xp, xs, gain, scp, shp, scs, shs, w_rt)


def _inproj_kernel(n_cast, n_xpose, dest_ref, src_ref, a_ref, w_ref, *refs):
    del dest_ref, src_ref
    n_in = n_cast + n_xpose
    side_in, o_ref, last_ref, side_out = refs[:n_in], refs[n_in], refs[n_in + 1], refs[n_in + 2:]
    w = w_ref[...].astype(BF16)
    acc = lax.dot_general(a_ref[...], w, NT_DIMS, preferred_element_type=F32)
    o_ref[...] = acc.astype(o_ref.dtype)
    last_ref[...] = acc

    for src, dst in zip(side_in[:n_cast], side_out[:n_cast]):
        dst[...] = src[...].astype(dst.dtype)
    for src, dst in zip(side_in[n_cast:], side_out[n_cast:]):
        dst[0] = jnp.transpose(src[0])


def _in_proj(a, wt, dest, src, tm, tn, out_dtype, casts, xposes, n_side):
    m, k = a.shape
    nt = len(dest)
    assert (m // tm) * nt >= n_side
    chunk = lambda i, j, d, s: jnp.minimum(i * nt + j, n_side - 1)
    side_in, side_out, side_shapes = [], [], []
    for arr in casts:
        blk = (arr.shape[0] // n_side, arr.shape[1])
        assert blk[0] % BF16_ROWS == 0
        spec = pl.BlockSpec(blk, lambda i, j, d, s: (chunk(i, j, d, s), 0))
        side_in.append(spec)
        side_out.append(spec)
        side_shapes.append(jax.ShapeDtypeStruct(arr.shape, BF16))
    for arr in xposes:
        _, r, c = arr.shape
        side_in.append(pl.BlockSpec((1, r, c), lambda i, j, d, s: (chunk(i, j, d, s), 0, 0)))
        side_out.append(pl.BlockSpec((1, c, r), lambda i, j, d, s: (chunk(i, j, d, s), 0, 0)))
        side_shapes.append(jax.ShapeDtypeStruct((n_side, c, r), arr.dtype))
    return pl.pallas_call(
        functools.partial(_inproj_kernel, len(casts), len(xposes)),
        out_shape=[jax.ShapeDtypeStruct((m, nt * tn), out_dtype), jax.ShapeDtypeStruct((m, tn), F32)] + side_shapes,
        grid_spec=pltpu.PrefetchScalarGridSpec(
            num_scalar_prefetch=2,
            grid=(m // tm, nt),
            in_specs=[
                pl.BlockSpec((tm, k), lambda i, j, d, s: (i, 0), pipeline_mode=pl.Buffered(1)),
                pl.BlockSpec((pl.Element(tn), pl.Element(k)),
                             lambda i, j, d, s: (pl.multiple_of(s[j], GATE_RANK), 0)),
            ] + side_in,
            out_specs=[pl.BlockSpec((tm, tn), lambda i, j, d, s: (i, d[j])),
                       pl.BlockSpec((tm, tn), lambda i, j, d, s: (i, 0))] + side_out,
        ),
        compiler_params=_cparams(("arbitrary", "arbitrary"), VMEM_LIMIT_RESIDENT),
        name="in_proj",
    )(jnp.asarray(dest, jnp.int32), jnp.asarray(src, jnp.int32), a, wt, *casts, *xposes)


def _swa_prompt_kernel(sink_ref, q_ref, za_ref, kp_ref, kc_ref, vp_ref, vc_ref, o_ref):
    n = pl.program_id(1)
    blk = ATTN_BLOCK
    pairs = GROUP // 2
    pw = 2 * HEAD_DIM
    hw = GROUP * HEAD_DIM
    k = jnp.concatenate([kp_ref[...], kc_ref[...]], axis=0)
    v = jnp.concatenate([vp_ref[...], vc_ref[...]], axis=0)
    rows = pairs * blk
    i = lax.broadcasted_iota(jnp.int32, (rows, 2 * blk), 0) % blk
    j = lax.broadcasted_iota(jnp.int32, (rows, 2 * blk), 1)
    visible = (j >= i) & (j <= i + WINDOW) & ((j >= blk) | (n > 0))
    bias = jnp.where(visible, 0.0, NEG_INF)
    pair_id = lax.broadcasted_iota(jnp.int32, (rows, 1), 0) // blk
    zeros = jnp.zeros((2 * blk, HEAD_DIM), BF16)
    for h in range(N_KV):
        kh = k[:, h * HEAD_DIM:(h + 1) * HEAD_DIM] * (HEAD_DIM ** -0.5)
        vh = v[:, h * HEAD_DIM:(h + 1) * HEAD_DIM]
        q2 = jnp.concatenate([q_ref[:, h * hw + jj * pw:h * hw + (jj + 1) * pw] for jj in range(pairs)], axis=0)
        acc = None
        k_pads = [jnp.concatenate([kh, zeros], axis=1), jnp.concatenate([zeros, kh], axis=1)]
        scores = [lax.dot_general(q2, kp, NT_DIMS, preferred_element_type=F32) + bias for kp in k_pads]
        probs = []
        for half in range(2):
            sink = jnp.zeros((rows, 1), F32)
            for jj in range(pairs):
                sink = jnp.where(pair_id == jj, sink_ref[h * GROUP + 2 * jj + half], sink)
            s = scores[half]
            m = jnp.maximum(jnp.max(s, axis=-1, keepdims=True), sink)
            p = jnp.exp(s - m)
            denom = jnp.sum(p, axis=-1, keepdims=True) + jnp.exp(sink - m)
            probs.append((p.astype(BF16), denom))
        for half in range(2):
            kv_parts = (lambda t: [t, zeros]) if half == 0 else (lambda t: [zeros, t])
            v_pad = jnp.concatenate(kv_parts(vh), axis=1)
            pb, denom = probs[half]
            o = jnp.dot(pb, v_pad, preferred_element_type=F32) / denom
            acc = o if acc is None else acc + o
        o_h = jnp.concatenate([acc[jj * blk:(jj + 1) * blk] for jj in range(pairs)], axis=1)
        za = za_ref[:, h * hw:(h + 1) * hw].astype(F32)
        o_ref[:, h * hw:(h + 1) * hw] = (o_h * _silu(za)).astype(BF16)


def _swa_prompt(sink, p, b, l):
    nb = l // ATTN_BLOCK
    kcol = OFF_KA // KV_W
    vcol = OFF_VA // KV_W
    blk_q = (ATTN_BLOCK, WIDTH_A)
    blk_kv = (ATTN_BLOCK, KV_W)
    cur = lambda bi, n: bi * nb + n
    prev = lambda bi, n: bi * nb + jnp.maximum(n - 1, 0)
    return pl.pallas_call(
        _swa_prompt_kernel,
        out_shape=jax.ShapeDtypeStruct((b * l, WIDTH_A), BF16),
        grid=(b, nb),
        in_specs=[
            pl.BlockSpec(memory_space=pltpu.SMEM),
            pl.BlockSpec(blk_q, lambda bi, n: (cur(bi, n), OFF_QA // WIDTH_A)),
            pl.BlockSpec(blk_q, lambda bi, n: (cur(bi, n), OFF_ZA // WIDTH_A)),
            pl.BlockSpec(blk_kv, lambda bi, n: (prev(bi, n), kcol)),
            pl.BlockSpec(blk_kv, lambda bi, n: (cur(bi, n), kcol)),
            pl.BlockSpec(blk_kv, lambda bi, n: (prev(bi, n), vcol)),
            pl.BlockSpec(blk_kv, lambda bi, n: (cur(bi, n), vcol)),
        ],
        out_specs=pl.BlockSpec(blk_q, lambda bi, n: (cur(bi, n), 0)),
        compiler_params=_cparams(("arbitrary", "arbitrary")),
        name="swa_prompt",
    )(sink, p, p, p, p, p, p)


def _swa_sample_kernel(g_blk, t_len, w_len, sink_ref, q_ref, za_ref, kv_ref, kc_ref, vc_ref,
                       o_ref, kw_ref, vw_ref):
    kpad = 2 * WINDOW - w_len - t_len
    nkeys = 2 * WINDOW
    rows = GROUP * t_len
    r = lax.broadcasted_iota(jnp.int32, (rows, nkeys), 0)
    j = lax.broadcasted_iota(jnp.int32, (rows, nkeys), 1)
    t = r % t_len
    visible = (j <= w_len + t) & (j >= w_len + t - WINDOW)
    bias = jnp.where(visible, 0.0, NEG_INF)[None]
    rg = lax.broadcasted_iota(jnp.int32, (rows, 1), 0) // t_len
    sink_cols = []
    for h in range(N_KV):
        col = jnp.zeros((rows, 1), F32)
        for g in range(GROUP):
            col = jnp.where(rg == g, sink_ref[h * GROUP + g], col)
        sink_cols.append(col)
    sink = jnp.concatenate(sink_cols * g_blk, axis=0).reshape(g_blk * N_KV, rows, 1)
    zpad = jnp.zeros((kpad, KV_W), F32)
    q_all = q_ref[...].astype(F32)
    za_all = za_ref[...].astype(F32)

    scores, values = [], []
    for s in range(g_blk):
        kc = kc_ref[s]
        vc = vc_ref[s]
        kn = kv_ref[s, :, :KV_W]
        vn = kv_ref[s, :, KV_W:]
        kw_ref[s] = jnp.concatenate([kc[t_len:], kn], axis=0)
        vw_ref[s] = jnp.concatenate([vc[t_len:], vn], axis=0)
        k_all = jnp.concatenate([kc, kn, zpad], axis=0).astype(BF16) * (HEAD_DIM ** -0.5)
        v_all = jnp.concatenate([vc, vn, zpad], axis=0).astype(BF16)
        q = q_all[s * t_len:(s + 1) * t_len]
        for h in range(N_KV):
            qh = jnp.concatenate(
                [q[:, (h * GROUP + g) * HEAD_DIM:(h * GROUP + g + 1) * HEAD_DIM] for g in range(GROUP)],
                axis=0).astype(BF16)
            scores.append(lax.dot_general(qh, k_all[:, h * HEAD_DIM:(h + 1) * HEAD_DIM], NT_DIMS,
                                          preferred_element_type=F32))
            values.append(v_all[:, h * HEAD_DIM:(h + 1) * HEAD_DIM])

    sc = jnp.concatenate(scores, axis=0).reshape(g_blk * N_KV, rows, nkeys) + bias
    m = jnp.maximum(jnp.max(sc, axis=-1, keepdims=True), sink)
    p = jnp.exp(sc - m)
    inv = 1.0 / (jnp.sum(p, axis=-1, keepdims=True) + jnp.exp(sink - m))
    pb = p.astype(BF16)

    seq_outs = []
    for s in range(g_blk):
        pieces = []
        for h in range(N_KV):
            c = s * N_KV + h
            oh = jnp.dot(pb[c], values[c], preferred_element_type=F32) * inv[c]
            for g in range(GROUP):
                pieces.append(oh[g * t_len:(g + 1) * t_len])
        o = jnp.concatenate(pieces, axis=1)
        seq_outs.append(o * _silu(za_all[s * t_len:(s + 1) * t_len]))
    o_ref[...] = jnp.concatenate(seq_outs, axis=0).astype(BF16)


def _cumsum_rows(x):
    c = x.shape[0]
    row = lax.broadcasted_iota(jnp.int32, (c, 1), 0)
    sh = 1
    while sh < c:
        x = x + jnp.where(row >= sh, pltpu.roll(x, sh, 0), 0.0)
        sh *= 2
    return x


def _gla_chunk(q, k, v, la, s_old, causal):
    c = q.shape[0]
    b = _cumsum_rows(la)
    b_last = b[c - 1:c, :]
    decay_col = jnp.transpose(jnp.broadcast_to(jnp.exp(b_last), (LANE, b_last.shape[1])))
    q_t = (q * jnp.exp(b) * (DK_B ** -0.5)).astype(BF16)
    k_t = (k * jnp.exp(-b)).astype(BF16)
    k_d = (k * jnp.exp(b_last - b)).astype(BF16)
    outs, states = [], []
    for h in range(N_HEADS_B):
        dk = slice(h * DK_B, (h + 1) * DK_B)
        dv = slice(h * DV_B, (h + 1) * DV_B)
        att = lax.dot_general(q_t[:, dk], k_t[:, dk], NT_DIMS, preferred_element_type=F32)
        att = jnp.where(causal, att, 0.0).astype(BF16)
        outs.append(jnp.dot(att, v[:, dv], preferred_element_type=F32)
                    + jnp.dot(q_t[:, dk], s_old[h].astype(BF16), preferred_element_type=F32))
        decay = jnp.concatenate([decay_col[dk]] * (DV_B // LANE), axis=1)
        states.append(decay * s_old[h] + lax.dot_general(k_d[:, dk], v[:, dv], TN_DIMS,
                                                         preferred_element_type=F32))
    return outs, states


def _gla_block(q, k, v, la, s_old, causal, n):
    c = q.shape[0] // n
    rows = [slice(i * c, (i + 1) * c) for i in range(n)]
    b = [_cumsum_rows(la[r]) for r in rows]
    t = [bi[c - 1:c, :] for bi in b]
    pre = [None]
    for i in range(n):
        pre.append(t[i] if pre[-1] is None else pre[-1] + t[i])
    decayed = lambda x, e: x if e is None else x * jnp.exp(e)
    scale = DK_B ** -0.5
    q_t = [q[r] * jnp.exp(bi) * scale for r, bi in zip(rows, b)]
    k_t = [(k[r] * jnp.exp(-bi)).astype(BF16) for r, bi in zip(rows, b)]
    k_d = [k[r] * jnp.exp(ti - bi) for r, bi, ti in zip(rows, b, t)]
    q_read = jnp.concatenate([decayed(q_t[i], pre[i]).astype(BF16) for i in range(n)], axis=0)
    k_upd = jnp.concatenate([decayed(k_d[j], None if j == n - 1 else pre[n] - pre[j + 1]).astype(BF16)
                             for j in range(n)], axis=0)
    q_b = [x.astype(BF16) for x in q_t]
    q_x = {(i, j): q_b[i] if i == j + 1 else (q_t[i] * jnp.exp(pre[i] - pre[j + 1])).astype(BF16)
           for j in range(n) for i in range(j + 1, n)}
    k_db = [x.astype(BF16) for x in k_d]
    decay_col = jnp.transpose(jnp.broadcast_to(jnp.exp(pre[n]), (LANE, q.shape[1])))
    nt = lambda x, y: lax.dot_general(x, y, NT_DIMS, preferred_element_type=F32)
    outs, states = [], []
    for h in range(N_HEADS_B):
        dk = slice(h * DK_B, (h + 1) * DK_B)
        dv = slice(h * DV_B, (h + 1) * DV_B)
        att = [nt(q_b[j][:, dk], k_t[j][:, dk]) for j in range(n)]
        cross = {ij: nt(qx[:, dk], k_db[ij[1]][:, dk]) for ij, qx in q_x.items()}
        lhs = [jnp.concatenate([jnp.where(causal, att[j], 0.0)] + [cross[i, j] for i in range(j + 1, n)],
                               axis=0).astype(BF16) for j in range(n)]
        o = (jnp.dot(lhs[0], v[rows[0], dv], preferred_element_type=F32)
             + jnp.dot(q_read[:, dk], s_old[h].astype(BF16), preferred_element_type=F32))
        done = [o[rows[0]]]
        for j in range(1, n):
            o = o[c:] + jnp.dot(lhs[j], v[rows[j], dv], preferred_element_type=F32)
            done.append(o[:c])
        outs.append(jnp.concatenate(done, axis=0))
        decay = jnp.concatenate([decay_col[dk]] * (DV_B // LANE), axis=1)
        states.append(decay * s_old[h] + lax.dot_general(k_upd[:, dk], v[:, dv], TN_DIMS,
                                                         preferred_element_type=F32))
    return outs, states


def _log_decay(r, w2, b2):
    z = jnp.dot(r.astype(BF16), w2, preferred_element_type=F32) + b2
    ls = jnp.minimum(z, 0.0) - jnp.log(1.0 + jnp.exp(-jnp.abs(z)))
    return ls / GLA_TAU


def _gla_finish(o, zb, gain):
    on = o * lax.rsqrt(jnp.mean(o * o, axis=-1, keepdims=True) + EPS) * gain
    return on * _silu(zb)


def _causal_mask(c):
    ri = lax.broadcasted_iota(jnp.int32, (c, c), 0)
    ci = lax.broadcasted_iota(jnp.int32, (c, c), 1)
    return ri >= ci


def _gla_prompt_kernel(nb, n_blk, *refs):
    w2_ref, b2_ref, gain_ref, o_ref, s_ref = refs[5 * nb:]

    @pl.when(pl.program_id(0) == 0)
    def _():
        s_ref[...] = jnp.zeros_like(s_ref)

    c = n_blk * GLA_CHUNK
    causal = _causal_mask(GLA_CHUNK)
    for bi in range(nb):
        q_ref, k_ref, v_ref, zb_ref, r_ref = refs[5 * bi:5 * bi + 5]
        states = [s_ref[bi, h] for h in range(N_HEADS_B)]
        for sub in range(o_ref.shape[1] // c):
            rs = slice(sub * c, (sub + 1) * c)
            la = _log_decay(r_ref[rs, :], w2_ref[...], b2_ref[...])
            outs, states = _gla_block(q_ref[rs, :].astype(F32), k_ref[rs, :].astype(F32), v_ref[rs, :], la,
                                      states, causal, n_blk)
            for h in range(N_HEADS_B):
                dv = slice(h * DV_B, (h + 1) * DV_B)
                o_ref[bi, rs, dv] = _gla_finish(outs[h], zb_ref[rs, dv].astype(F32), gain_ref[:, dv]).astype(BF16)
        for h in range(N_HEADS_B):
            s_ref[bi, h] = states[h]


def _gla_prompt(p, r, w2, b2, gain, b, l, chunks_per_block=4):
    c = GLA_CHUNK * chunks_per_block
    nc = l // c
    in_specs, args = [], []
    for bi in range(b):
        for arr, width, off in ((p, KEY_B, OFF_QB), (p, KEY_B, OFF_KB), (p, WIDTH_B, OFF_VB),
                                (p, WIDTH_B, OFF_ZB), (r, LANE, 0)):
            in_specs.append(pl.BlockSpec((c, width), functools.partial(
                lambda ci, row0, col: (row0 + ci, col), row0=bi * nc, col=off // width)))
            args.append(arr)
    in_specs += [pl.BlockSpec((LANE, KEY_B), lambda ci: (0, 0)),
                 pl.BlockSpec((1, KEY_B), lambda ci: (0, 0)),
                 pl.BlockSpec((1, WIDTH_B), lambda ci: (0, 0))]
    return pl.pallas_call(
        functools.partial(_gla_prompt_kernel, b, chunks_per_block),
        out_shape=(jax.ShapeDtypeStruct((b, l, WIDTH_B), BF16),
                   jax.ShapeDtypeStruct((b, N_HEADS_B, DK_B, DV_B), F32)),
        grid=(nc,),
        in_specs=in_specs,
        out_specs=(pl.BlockSpec((b, c, WIDTH_B), lambda ci: (0, ci, 0)),
                   pl.BlockSpec((b, N_HEADS_B, DK_B, DV_B), lambda ci: (0, 0, 0, 0))),
        compiler_params=_cparams(("arbitrary",)),
        name="gla_prompt",
    )(*args, w2, b2, gain)


def _gla_sample_kernel(g_blk, t_len, q_ref, k_ref, v_ref, zb_ref, r_ref, w2_ref, b2_ref, gain_ref, s0_ref,
                       o_ref, s_ref):
    c = 2 * t_len
    causal = _causal_mask(c)

    def pad(x):
        return jnp.concatenate([x, jnp.zeros_like(x)], axis=0)

    live = lax.broadcasted_iota(jnp.int32, (c, 1), 0) < t_len
    q_all = q_ref[...].astype(F32)
    k_all = k_ref[...].astype(F32)
    v_all = v_ref[...].astype(F32)
    zb_all = zb_ref[...].astype(F32)
    seq_outs = []
    for s in range(g_blk):
        rs = slice(s * t_len, (s + 1) * t_len)
        la = jnp.where(live, _log_decay(pad(r_ref[s]), w2_ref[...], b2_ref[...]), 0.0)
        outs, states = _gla_chunk(pad(q_all[rs]), pad(k_all[rs]), pad(v_all[rs]).astype(BF16), la,
                                  [s0_ref[s, h] for h in range(N_HEADS_B)], causal)
        fin = []
        for h in range(N_HEADS_B):
            dv = slice(h * DV_B, (h + 1) * DV_B)
            s_ref[s, h] = states[h]
            fin.append(_gla_finish(outs[h][:t_len], zb_all[rs, dv], gain_ref[:, dv]))
        seq_outs.append(jnp.concatenate(fin, axis=1))
    o_ref[...] = jnp.concatenate(seq_outs, axis=0).astype(BF16)


def _sample_kernel(g_blk, t_len, w_len, *refs):
    n_swa_in, n_gla_in, n_swa_out, n_out = 6, 9, 3, 5
    swa_in, gla_in = refs[:n_swa_in], refs[n_swa_in:n_swa_in + n_gla_in]
    outs = refs[n_swa_in + n_gla_in:n_swa_in + n_gla_in + n_out]
    ring, sem = refs[n_swa_in + n_gla_in + n_out:]
    state_hbm = gla_in[-1]

    step, n_steps = pl.program_id(0), pl.num_programs(0)

    def fetch(s):
        slot = lax.rem(s, STATE_RING)
        return pltpu.make_async_copy(state_hbm.at[pl.ds(s * g_blk, g_blk)], ring.at[slot], sem.at[slot])

    @pl.when(step == 0)
    def _():
        for s in range(STATE_RING - 1):
            fetch(s).start()

    @pl.when(step + (STATE_RING - 1) < n_steps)
    def _():
        fetch(step + (STATE_RING - 1)).start()

    fetch(step).wait()
    _swa_sample_kernel(g_blk, t_len, w_len, *swa_in, *outs[:n_swa_out])
    _gla_sample_kernel(g_blk, t_len, *gla_in[:-1], ring.at[lax.rem(step, STATE_RING)], *outs[n_swa_out:])


def _sample_branches(sink, p, r3, kv3, row0, n, t_len, cache_k, cache_v, w2, b2, gain, state, g_blk=4):
    rows = g_blk * t_len
    assert rows % BF16_ROWS == 0 and n // g_blk >= STATE_RING
    w_len = cache_k.shape[1]
    blk_q = (rows, WIDTH_A)
    blk_c = (g_blk, w_len, KV_W)
    st_blk = (g_blk, N_HEADS_B, DK_B, DV_B)
    r0 = row0 // rows
    col = lambda width, off: (lambda i: (r0 + i, off // width))
    seq3 = lambda i: (r0 + i, 0, 0)
    own3 = lambda i: (i, 0, 0)
    const = lambda i: (0, 0)
    return pl.pallas_call(
        functools.partial(_sample_kernel, g_blk, t_len, w_len),
        out_shape=(jax.ShapeDtypeStruct((n * t_len, WIDTH_A), BF16),
                   jax.ShapeDtypeStruct((n, w_len, KV_W), F32),
                   jax.ShapeDtypeStruct((n, w_len, KV_W), F32),
                   jax.ShapeDtypeStruct((n * t_len, WIDTH_B), BF16),
                   jax.ShapeDtypeStruct(state.shape, F32)),
        grid=(n // g_blk,),
        in_specs=[
            pl.BlockSpec(memory_space=pltpu.SMEM),
            pl.BlockSpec(blk_q, col(WIDTH_A, OFF_QA)),
            pl.BlockSpec(blk_q, col(WIDTH_A, OFF_ZA)),
            pl.BlockSpec((g_blk, t_len, 2 * KV_W), seq3),
            pl.BlockSpec(blk_c, own3),
            pl.BlockSpec(blk_c, own3),
            pl.BlockSpec((rows, KEY_B), col(KEY_B, OFF_QB)),
            pl.BlockSpec((rows, KEY_B), col(KEY_B, OFF_KB)),
            pl.BlockSpec((rows, WIDTH_B), col(WIDTH_B, OFF_VB)),
            pl.BlockSpec((rows, WIDTH_B), col(WIDTH_B, OFF_ZB)),
            pl.BlockSpec((g_blk, t_len, LANE), seq3),
            pl.BlockSpec((LANE, KEY_B), const),
            pl.BlockSpec((1, KEY_B), const),
            pl.BlockSpec((1, WIDTH_B), const),
            pl.BlockSpec(memory_space=pl.ANY),
        ],
        out_specs=(pl.BlockSpec(blk_q, lambda i: (i, 0)),
                   pl.BlockSpec(blk_c, own3),
                   pl.BlockSpec(blk_c, own3),
                   pl.BlockSpec((rows, WIDTH_B), lambda i: (i, 0)),
                   pl.BlockSpec(st_blk, lambda i: (i, 0, 0, 0))),
        scratch_shapes=[pltpu.VMEM((STATE_RING,) + st_blk, F32), pltpu.SemaphoreType.DMA((STATE_RING,))],
        compiler_params=_cparams(("arbitrary",)),
        name="sample_branches",
    )(sink, p, p, kv3, cache_k, cache_v, p, p, p, p, r3, w2, b2, gain, state)


def _merge_kernel(nchunk, n_ptiles, ap_ref, bp_ref, as_ref, bs_ref, wa_ref, wb_ref, ga_ref, gb_ref, o_ref):
    cw = o_ref.shape[1] // nchunk
    is_prompt = pl.program_id(0) < n_ptiles
    a = jnp.where(is_prompt, ap_ref[...], as_ref[...])
    b = jnp.where(is_prompt, bp_ref[...], bs_ref[...])
    for c in range(nchunk):
        cs = slice(c * cw, (c + 1) * cw)
        ua = jnp.dot(a, wa_ref[:, cs], preferred_element_type=F32)
        ub = jnp.dot(b, wb_ref[:, cs], preferred_element_type=F32)
        merged = (jax.nn.sigmoid(ga_ref[:, cs].astype(F32)) * ua
                  + jax.nn.sigmoid(gb_ref[:, cs].astype(F32)) * ub)
        o_ref[:, cs] = merged.astype(BF16)


def _merge(ap, bp, a_s, b_s, w_pa, w_pb, p, tm):
    n_pt = ap.shape[0] // tm
    n_st = a_s.shape[0] // tm
    d = D_MODEL
    resident = dict(pipeline_mode=pl.Buffered(1))
    pt = lambda i: (jnp.minimum(i, n_pt - 1), 0)
    st = lambda i: (jnp.maximum(i - n_pt, 0), 0)
    return pl.pallas_call(
        functools.partial(_merge_kernel, 4, n_pt),
        out_shape=jax.ShapeDtypeStruct(((n_pt + n_st) * tm, d), BF16),
        grid=(n_pt + n_st,),
        in_specs=[
            pl.BlockSpec((tm, WIDTH_A), pt),
            pl.BlockSpec((tm, WIDTH_B), pt),
            pl.BlockSpec((tm, WIDTH_A), st),
            pl.BlockSpec((tm, WIDTH_B), st),
            pl.BlockSpec((WIDTH_A, d), lambda i: (0, 0), **resident),
            pl.BlockSpec((WIDTH_B, d), lambda i: (0, 0), **resident),
            pl.BlockSpec((tm, d), lambda i: (i, OFF_GA // d)),
            pl.BlockSpec((tm, d), lambda i: (i, OFF_GB // d)),
        ],
        out_specs=pl.BlockSpec((tm, d), lambda i: (i, 0)),
        compiler_params=_cparams(("arbitrary",), VMEM_LIMIT_RESIDENT),
        name="merge",
    )(ap, bp, a_s, b_s, w_pa, w_pb, p, p)


def _out_kernel(nchunk, m_ref, w_ref, x_ref, gate_ref, fg_ref, o_ref):
    g_blk, r_blk, d = x_ref.shape
    cw = d // nchunk
    ssq = jnp.zeros((g_blk, r_blk, 1), F32)
    for c in range(nchunk):
        cs = slice(c * cw, (c + 1) * cw)
        y = jnp.dot(m_ref[...], w_ref[:, cs], preferred_element_type=F32)
        xn = x_ref[:, :, cs] + gate_ref[:, :, cs] * y.reshape(g_blk, r_blk, cw)
        o_ref[:, :, cs] = xn
        ssq = ssq + jnp.sum(xn * xn, axis=-1, keepdims=True)
    inv = lax.rsqrt(ssq * (1.0 / d) + EPS)
    for c in range(nchunk):
        cs = slice(c * cw, (c + 1) * cw)
        o_ref[:, :, cs] = o_ref[:, :, cs] * inv * fg_ref[:, :, cs]


def _out_proj(merged, row0, w_o, x3, gate3, fin_gain, g_blk, r_blk):
    n, l, d = x3.shape
    rows = g_blk * r_blk
    lb = l // r_blk
    t0 = row0 // rows
    return pl.pallas_call(
        functools.partial(_out_kernel, 4),
        out_shape=jax.ShapeDtypeStruct((n, l, d), F32),
        grid=(n // g_blk, lb),
        in_specs=[
            pl.BlockSpec((rows, d), lambda i, r: (t0 + i * lb + r, 0)),
            pl.BlockSpec((d, d), lambda i, r: (0, 0), pipeline_mode=pl.Buffered(1)),
            pl.BlockSpec((g_blk, r_blk, d), lambda i, r: (i, r, 0)),
            pl.BlockSpec((g_blk, 1, d), lambda i, r: (i, 0, 0)),
            pl.BlockSpec((1, 1, d), lambda i, r: (0, 0, 0)),
        ],
        out_specs=pl.BlockSpec((g_blk, r_blk, d), lambda i, r: (i, r, 0)),
        compiler_params=_cparams(("arbitrary", "arbitrary"), VMEM_LIMIT_RESIDENT),
        name="out_proj",
    )(merged, w_o, x3, gate3, fin_gain)


def kernel(x_prompt, x_sample, cache_k_win, cache_v_win, state_gla, c_prompt, c_sample,
           w_ada, b_ada, norm_gain, w_in, attn_sink, w_alpha2, b_alpha, gla_norm_gain,
           w_proj_a, w_proj_b, w_out, final_norm_gain):
    assert w_ada.shape[0] == 1, "single-layer step"
    bp, lp, d = x_prompt.shape
    ns, ts, _ = x_sample.shape
    w_len = cache_k_win.shape[2]
    rows_p = bp * lp
    rows_s = ns * ts
    rows = rows_p + rows_s

    wt = jnp.transpose(w_in[0])
    w_rt = jnp.pad(wt[SRC_MAIN:SRC_GATES], ((0, LANE - GATE_RANK), (0, 0))).astype(BF16)
    w2 = jnp.pad(w_alpha2[0], ((0, LANE - GATE_RANK), (0, 0))).astype(BF16)
    b2 = b_alpha[0].reshape(1, KEY_B)
    gla_gain = gla_norm_gain[0].reshape(1, WIDTH_B)
    cache_t = lambda c: jnp.transpose(c[0], (0, 2, 3, 1)).reshape(ns, KV_W, w_len)
    gain3 = norm_gain[0].reshape(1, 1, d)
    fin3 = final_norm_gain.reshape(1, 1, d)
    sink = attn_sink[0]

    m_all = ns + bp
    m_pad = -(-m_all // 8) * 8
    c_all = jnp.concatenate([c_sample, c_prompt, jnp.zeros((m_pad - m_all, d), F32)], axis=0)
    mod = _adaln(c_all, w_ada[0], b_ada[0].reshape(1, 3 * d))
    shift, scale, gate = mod[:, :d], mod[:, d:2 * d], mod[:, 2 * d:]
    sl_s, sl_p = slice(0, ns), slice(ns, ns + bp)
    as3 = lambda a, n: a.reshape(n, 1, d)

    h, r = _prologue(x_prompt, x_sample, gain3, as3(scale[sl_p], bp), as3(shift[sl_p], bp),
                     as3(scale[sl_s], ns), as3(shift[sl_s], ns), w_rt, 512)
    p, kv, w_pa, w_pb, w_o, cache_k, cache_v = _in_proj(
        h, wt, PROJ_DEST, PROJ_SRC, rows // 4, PROJ_TN, BF16,
        casts=(w_proj_a[0], w_proj_b[0], w_out[0]),
        xposes=(cache_t(cache_k_win), cache_t(cache_v_win)), n_side=ns)
    kv_p = jnp.stack([kv[(bi + 1) * lp - WINDOW:(bi + 1) * lp] for bi in range(bp)])
    kv_p = kv_p.reshape(bp, WINDOW, 2, N_KV, HEAD_DIM)
    k_win_p = kv_p[:, :, 0][None]
    v_win_p = kv_p[:, :, 1][None]
    kv3 = kv.reshape(rows // ts, ts, 2 * KV_W)
    r3 = r.reshape(rows // ts, ts, LANE)

    ua_p = _swa_prompt(sink, p, bp, lp)
    ub_p, s_p = _gla_prompt(p, r, w2, b2, gla_gain, bp, lp)

    ua_s, k_win_s, v_win_s, ub_s, s_s = _sample_branches(sink, p, r3, kv3, rows_p, ns, ts, cache_k, cache_v,
                                                         w2, b2, gla_gain, state_gla[0])

    merged = _merge(ua_p, ub_p.reshape(rows_p, WIDTH_B), ua_s, ub_s, w_pa, w_pb, p, 256)
    y_p = _out_proj(merged, 0, w_o, x_prompt, as3(gate[sl_p], bp), fin3, 1, 256)
    y_s = _out_proj(merged, rows_p, w_o, x_sample, as3(gate[sl_s], ns), fin3, 256 // ts, ts)

    return (y_p, y_s, k_win_p, v_win_p, s_p[None],
            k_win_s.reshape(1, ns, w_len, N_KV, HEAD_DIM), v_win_s.reshape(1, ns, w_len, N_KV, HEAD_DIM),
            s_s[None])
```
